```python
import math
import jax, jax.numpy as jnp
from jax import lax
import numpy as np

D_MODEL = 1024
BATCH = 8
SEQ = 4096
DEPTH = 2

GRID_W = 64
CTX_LEN = 256
ROPE_BASE = 10000.0
EPS = 1e-6
NEG_INF = -1e30

A_HEADS = 8
A_KV_HEADS = 2
A_GROUP = A_HEADS // A_KV_HEADS
A_HEAD_DIM = 64
A_WIDTH = A_HEADS * A_HEAD_DIM
WINDOW = 128

B_HEADS = 8
B_NOPE = 64
B_ROPE = 32
B_VDIM = 64
B_WIDTH = B_HEADS * B_VDIM
Q_LORA = 256
KV_LORA = 256
Q_BLOCK = 128

POOL_WINDOWS = (2, 4, 8, 16)
C_GROUPS = 4
C_GROUP_DIM = 128
C_WIDTH = C_GROUPS * C_GROUP_DIM

N_BRANCH = 3
IN_SPLITS = (A_WIDTH, A_KV_HEADS * A_HEAD_DIM, A_KV_HEADS * A_HEAD_DIM,
             Q_LORA, KV_LORA, B_ROPE, C_WIDTH, N_BRANCH * D_MODEL)
IN_WIDTH = sum(IN_SPLITS)

N_GROUPS = 4
EXPERTS_PER_GROUP = 8
N_EXPERTS = N_GROUPS * EXPERTS_PER_GROUP
TOP_K = 2
EXPERT_FF = 256
EXPERT_BLOCK = 128

kernel_name = 'hybrid_gated_swa_mla_pool_hmoe_dit'


def rmsnorm(x, g):
    xf = x.astype(jnp.float32)
    y = xf * lax.rsqrt(jnp.mean(xf * xf, axis=-1, keepdims=True) + EPS)
    return (y * g.astype(jnp.float32)).astype(x.dtype)


def modulate(h, g, shift, scale):
    return rmsnorm(h, g) * (1 + scale) + shift


def axial_rope_tables(rows, rot_dim):
    row = jnp.repeat(jnp.arange(rows), GRID_W).astype(jnp.float32)
    col = jnp.tile(jnp.arange(GRID_W), rows).astype(jnp.float32)
    axis_dim = rot_dim // 2
    inv = ROPE_BASE ** (-jnp.arange(0, axis_dim, 2, dtype=jnp.float32) / axis_dim)
    ang = jnp.concatenate([row[:, None] * inv, col[:, None] * inv], axis=-1)
    return jnp.cos(ang), jnp.sin(ang)


def apply_rope(x, cos, sin):
    xf = x.astype(jnp.float32)
    x1, x2 = xf[..., 0::2], xf[..., 1::2]
    c, s = cos[:, None, :], sin[:, None, :]
    out = jnp.stack([x1 * c - x2 * s, x1 * s + x2 * c], axis=-1).reshape(x.shape)
    return out.astype(x.dtype)


def split_in(p):
    idx, acc = [], 0
    for w in IN_SPLITS[:-1]:
        acc += w
        idx.append(acc)
    return jnp.split(p, idx, axis=-1)


def window_attn(q, k, v, kc, vc, sink):
    B, N = q.shape[:2]
    nb = N // WINDOW
    scale = A_HEAD_DIM ** -0.5
    qb = q.reshape(B, nb, WINDOW, A_KV_HEADS, A_GROUP, A_HEAD_DIM)
    pad = ((0, 0), (WINDOW, WINDOW), (0, 0), (0, 0))
    kp = jnp.pad(k, pad).reshape(B, nb + 2, WINDOW, A_KV_HEADS, A_HEAD_DIM)
    vp = jnp.pad(v, pad).reshape(B, nb + 2, WINDOW, A_KV_HEADS, A_HEAD_DIM)
    kband = jnp.concatenate([kp[:, :-2], kp[:, 1:-1], kp[:, 2:]], axis=2)
    vband = jnp.concatenate([vp[:, :-2], vp[:, 1:-1], vp[:, 2:]], axis=2)
    s_loc = jnp.einsum('bnqhgd,bnkhd->bnhgqk', qb, kband).astype(jnp.float32) * scale
    s_ctx = jnp.einsum('bnqhgd,bchd->bnhgqc', qb, kc).astype(jnp.float32) * scale
    rel = jnp.arange(3 * WINDOW)[None, :] - WINDOW - jnp.arange(WINDOW)[:, None]
    band = jnp.abs(rel) <= WINDOW
    kpos = jnp.arange(nb)[:, None] * WINDOW - WINDOW + jnp.arange(3 * WINDOW)[None, :]
    valid = (kpos >= 0) & (kpos < N)
    mask = band[None, :, :] & valid[:, None, :]
    s_loc = jnp.where(mask[None, :, None, None, :, :], s_loc, NEG_INF)
    sk = jnp.broadcast_to(sink.astype(jnp.float32).reshape(A_KV_HEADS, A_GROUP)[None, None, :, :, None, None],
                          s_loc.shape[:-1] + (1,))
    p = jax.nn.softmax(jnp.concatenate([s_loc, s_ctx, sk], axis=-1), axis=-1)
    p_loc = p[..., :3 * WINDOW].astype(v.dtype)
    p_ctx = p[..., 3 * WINDOW:3 * WINDOW + kc.shape[1]].astype(v.dtype)
    o = (jnp.einsum('bnhgqk,bnkhd->bnqhgd', p_loc, vband)
         + jnp.einsum('bnhgqc,bchd->bnqhgd', p_ctx, vc))
    return o.reshape(B, N, A_WIDTH)


def ctx_gqa_attn(q, k, v, sink):
    B, L = q.shape[:2]
    qg = q.reshape(B, L, A_KV_HEADS, A_GROUP, A_HEAD_DIM)
    s = jnp.einsum('bqhgd,bkhd->bhgqk', qg, k).astype(jnp.float32) * (A_HEAD_DIM ** -0.5)
    sk = jnp.broadcast_to(sink.astype(jnp.float32).reshape(A_KV_HEADS, A_GROUP)[None, :, :, None, None],
                          s.shape[:-1] + (1,))
    p = jax.nn.softmax(jnp.concatenate([s, sk], axis=-1), axis=-1)[..., :-1].astype(v.dtype)
    o = jnp.einsum('bhgqk,bkhd->bqhgd', p, v)
    return o.reshape(B, L, A_WIDTH)


def mla_qkv(cq, ckv, kr, q_norm_g, w_uq, kv_norm_g, w_ukv, rope):
    B, L, _ = cq.shape
    q = (rmsnorm(cq, q_norm_g) @ w_uq).reshape(B, L, B_HEADS, B_NOPE + B_ROPE)
    kv = (rmsnorm(ckv, kv_norm_g) @ w_ukv).reshape(B, L, B_HEADS, B_NOPE + B_VDIM)
    q_nope, q_rope = q[..., :B_NOPE], q[..., B_NOPE:]
    k_nope, v = kv[..., :B_NOPE], kv[..., B_NOPE:]
    k_rope = kr[:, :, None, :]
    if rope is not None:
        q_rope = apply_rope(q_rope, *rope)
        k_rope = apply_rope(k_rope, *rope)
    q = jnp.concatenate([q_nope, q_rope], axis=-1)
    k = jnp.concatenate([k_nope, jnp.broadcast_to(k_rope, (B, L, B_HEADS, B_ROPE))], axis=-1)
    return q, k, v


def dense_attn(q, k, v):
    s = jnp.einsum('bqhd,bkhd->bhqk', q, k).astype(jnp.float32) * (q.shape[-1] ** -0.5)
    p = jax.nn.softmax(s, axis=-1).astype(v.dtype)
    o = jnp.einsum('bhqk,bkhd->bqhd', p, v)
    return o.reshape(q.shape[0], q.shape[1], -1)


def blocked_dense_attn(q, k, v):
    B, N, H, dq = q.shape
    nb = N // Q_BLOCK
    qb = q.reshape(B, nb, Q_BLOCK, H, dq).transpose(1, 0, 2, 3, 4)
    o = lax.map(lambda qblk: dense_attn(qblk, k, v), qb)
    return o.transpose(1, 0, 2, 3).reshape(B, N, -1)


def pool_mixer(u, w_pool, pool_scale):
    B, L, _ = u.shape
    uf = u.astype(jnp.float32)
    cs = jnp.concatenate([jnp.zeros((B, 1, C_WIDTH), jnp.float32), jnp.cumsum(uf, axis=1)], axis=1)
    t = jnp.arange(L)
    parts = []
    for g, w in enumerate(POOL_WINDOWS):
        r = w // 2
        lo = jnp.maximum(t - r, 0)
        hi = jnp.minimum(t + r + 1, L)
        sl = slice(g * C_GROUP_DIM, (g + 1) * C_GROUP_DIM)
        win_sum = cs[:, hi, sl] - cs[:, lo, sl]
        parts.append(win_sum / (hi - lo).astype(jnp.float32)[:, None] - uf[:, :, sl])
    pooled = jnp.concatenate(parts, axis=-1).reshape(B, L, C_GROUPS, C_GROUP_DIM).astype(u.dtype)
    mixed = jnp.einsum('blgc,gcd->blgd', pooled, w_pool).reshape(B, L, C_WIDTH)
    return mixed * pool_scale


def merge_branches(gates, oa, ob, oc, lp):
    ga, gb, gc = jnp.split(gates, N_BRANCH, axis=-1)
    y = (jax.nn.sigmoid(ga) * (oa @ lp['w_br_a'])
         + jax.nn.sigmoid(gb) * (ob @ lp['w_br_b'])
         + jax.nn.sigmoid(gc) * (oc @ lp['w_br_c']))
    return y @ lp['w_out']


def mixer_block(hx, hc, lp, rope_a, rope_b, last):
    B, N, _ = hx.shape
    Lc = hc.shape[1]
    qa, ka, va, cq, ckv, kr, u, gates = split_in(hx @ lp['w_in'])
    qa_c, ka_c, va_c, cq_c, ckv_c, kr_c, u_c, gates_c = split_in(hc @ lp['w_in'])
    qa = apply_rope(qa.reshape(B, N, A_HEADS, A_HEAD_DIM), *rope_a)
    ka = apply_rope(ka.reshape(B, N, A_KV_HEADS, A_HEAD_DIM), *rope_a)
    va = va.reshape(B, N, A_KV_HEADS, A_HEAD_DIM)
    ka_c = ka_c.reshape(B, Lc, A_KV_HEADS, A_HEAD_DIM)
    va_c = va_c.reshape(B, Lc, A_KV_HEADS, A_HEAD_DIM)
    oa = window_attn(qa, ka, va, ka_c, va_c, lp['sink'])
    qb, kb, vb = mla_qkv(cq, ckv, kr, lp['q_norm_g'], lp['w_uq'], lp['kv_norm_g'], lp['w_ukv'], rope_b)
    qb_c, kb_c, vb_c = mla_qkv(cq_c, ckv_c, kr_c, lp['q_norm_g'], lp['w_uq'], lp['kv_norm_g'], lp['w_ukv'], None)
    ob = blocked_dense_attn(qb, jnp.concatenate([kb_c, kb], axis=1), jnp.concatenate([vb_c, vb], axis=1))
    oc = pool_mixer(u, lp['w_pool'], lp['pool_scale'])
    out_x = merge_branches(gates, oa, ob, oc, lp)
    if last:
        return out_x, None
    oa_c = ctx_gqa_attn(qa_c.reshape(B, Lc, A_HEADS, A_HEAD_DIM), ka_c, va_c, lp['sink'])
    ob_c = dense_attn(qb_c, kb_c, vb_c)
    oc_c = pool_mixer(u_c, lp['w_pool'], lp['pool_scale'])
    out_c = merge_branches(gates_c, oa_c, ob_c, oc_c, lp)
    return out_x, out_c


def hier_moe(h, w_rg, b_rg, w_re, b_re, w_gu, w_dn):
    T, D = h.shape
    g_logits = (h @ w_rg + b_rg).astype(jnp.float32)
    e_logits = (h @ w_re + b_re).astype(jnp.float32).reshape(T, N_GROUPS, EXPERTS_PER_GROUP)
    g_sel = jnp.argmax(g_logits, axis=-1)
    g_p = jnp.take_along_axis(jax.nn.softmax(g_logits, axis=-1), g_sel[:, None], axis=1)
    e_grp = jnp.take_along_axis(e_logits, g_sel[:, None, None], axis=1)[:, 0]
    top_v, top_i = lax.top_k(e_grp, TOP_K)
    gate = jax.nn.softmax(top_v, axis=-1) * g_p
    expert = g_sel[:, None] * EXPERTS_PER_GROUP + top_i
    A = T * TOP_K
    flat_e = expert.reshape(A)
    order = jnp.argsort(flat_e)
    e_sorted = flat_e[order]
    tok_sorted = order // TOP_K
    gate_sorted = gate.reshape(A)[order]
    counts = jax.ops.segment_sum(jnp.ones((A,), jnp.int32), flat_e, num_segments=N_EXPERTS)
    padded = (counts + EXPERT_BLOCK - 1) // EXPERT_BLOCK * EXPERT_BLOCK
    pad_end = jnp.cumsum(padded)
    pad_start = pad_end - padded
    start = jnp.cumsum(counts) - counts
    slot = pad_start[e_sorted] + (jnp.arange(A) - start[e_sorted])
    n_blocks = -(-(A + N_EXPERTS * (EXPERT_BLOCK - 1)) // EXPERT_BLOCK)
    P = n_blocks * EXPERT_BLOCK
    buf = jnp.zeros((P, D), h.dtype).at[slot].set(h[tok_sorted])
    blk_e = jnp.minimum(jnp.searchsorted(pad_end, jnp.arange(n_blocks) * EXPERT_BLOCK, side='right'),
                        N_EXPERTS - 1)

    def run_block(args):
        xb, e = args
        gt, up = jnp.split(xb @ w_gu[e], 2, axis=-1)
        return (jax.nn.silu(gt) * up) @ w_dn[e]

    yb = lax.map(run_block, (buf.reshape(n_blocks, EXPERT_BLOCK, D), blk_e))
    y = yb.reshape(P, D)[slot] * gate_sorted[:, None].astype(h.dtype)
    return jax.ops.segment_sum(y, tok_sorted, num_segments=T)


def setup_inputs(seed: int = 0) -> dict:
    key = jax.random.key(seed)
    ks = jax.random.split(key, 32)
    nrm = jax.random.normal
    f32 = jnp.float32
    D = D_MODEL
    return {
        'x': nrm(ks[0], (BATCH, SEQ, D), f32),
        'c': nrm(ks[1], (BATCH, D), f32),
        'ctx': nrm(ks[2], (BATCH, CTX_LEN, D), f32),
        'c_ctx': nrm(ks[3], (D,), f32),
        'w_mod': nrm(ks[4], (DEPTH, D, 6 * D), f32) * (0.5 * D ** -0.5),
        'b_mod': nrm(ks[5], (DEPTH, 6 * D), f32) * 0.01,
        'norm_mix_g': 1.0 + 0.02 * nrm(ks[6], (DEPTH, D), f32),
        'norm_ffn_g': 1.0 + 0.02 * nrm(ks[7], (DEPTH, D), f32),
        'w_in': nrm(ks[8], (DEPTH, D, IN_WIDTH), f32) * D ** -0.5,
        'sink': 0.5 * nrm(ks[9], (DEPTH, A_HEADS), f32),
        'q_norm_g': 1.0 + 0.02 * nrm(ks[10], (DEPTH, Q_LORA), f32),
        'w_uq': nrm(ks[11], (DEPTH, Q_LORA, B_HEADS * (B_NOPE + B_ROPE)), f32) * Q_LORA ** -0.5,
        'kv_norm_g': 1.0 + 0.02 * nrm(ks[12], (DEPTH, KV_LORA), f32),
        'w_ukv': nrm(ks[13], (DEPTH, KV_LORA, B_HEADS * (B_NOPE + B_VDIM)), f32) * KV_LORA ** -0.5,
        'w_pool': nrm(ks[14], (DEPTH, C_GROUPS, C_GROUP_DIM, C_GROUP_DIM), f32) * C_GROUP_DIM ** -0.5,
        'pool_scale': 1.0 + 0.02 * nrm(ks[15], (DEPTH, C_WIDTH), f32),
        'w_br_a': nrm(ks[16], (DEPTH, A_WIDTH, D), f32) * A_WIDTH ** -0.5,
        'w_br_b': nrm(ks[17], (DEPTH, B_WIDTH, D), f32) * B_WIDTH ** -0.5,
        'w_br_c': nrm(ks[18], (DEPTH, C_WIDTH, D), f32) * C_WIDTH ** -0.5,
        'w_out': nrm(ks[19], (DEPTH, D, D), f32) * D ** -0.5,
        'w_rg': nrm(ks[20], (DEPTH, D, N_GROUPS), f32) * D ** -0.5,
        'b_rg': 0.01 * nrm(ks[21], (DEPTH, N_GROUPS), f32),
        'w_re': nrm(ks[22], (DEPTH, D, N_EXPERTS), f32) * D ** -0.5,
        'b_re': 0.01 * nrm(ks[23], (DEPTH, N_EXPERTS), f32),
        'w_gu': nrm(ks[24], (DEPTH, N_EXPERTS, D, 2 * EXPERT_FF), f32) * D ** -0.5,
        'w_dn': nrm(ks[25], (DEPTH, N_EXPERTS, EXPERT_FF, D), f32) * EXPERT_FF ** -0.5,
        'final_g': 1.0 + 0.02 * nrm(ks[26], (D,), f32),
    }


def reference(x, c, ctx, c_ctx, w_mod, b_mod, norm_mix_g, norm_ffn_g, w_in, sink, q_norm_g, w_uq,
              kv_norm_g, w_ukv, w_pool, pool_scale, w_br_a, w_br_b, w_br_c, w_out, w_rg, b_rg, w_re,
              b_re, w_gu, w_dn, final_g):
    B, N, D = x.shape
    Lc = ctx.shape[1]
    ROWS = N // GRID_W
    rope_a = axial_rope_tables(ROWS, A_HEAD_DIM)
    rope_b = axial_rope_tables(ROWS, B_ROPE)
    h, hcs = x, ctx
    for l in range(DEPTH):
        last = l == DEPTH - 1
        lp = {'w_in': w_in[l], 'sink': sink[l], 'q_norm_g': q_norm_g[l], 'w_uq': w_uq[l],
              'kv_norm_g': kv_norm_g[l], 'w_ukv': w_ukv[l], 'w_pool': w_pool[l],
              'pool_scale': pool_scale[l], 'w_br_a': w_br_a[l], 'w_br_b': w_br_b[l],
              'w_br_c': w_br_c[l], 'w_out': w_out[l]}
        mod_x = jax.nn.silu(c) @ w_mod[l] + b_mod[l]
        mod_c = (jax.nn.silu(c_ctx) @ w_mod[l] + b_mod[l])[None]
        sh1, sc1, g1, sh2, sc2, g2 = jnp.split(mod_x[:, None, :], 6, axis=-1)
        sh1c, sc1c, g1c, sh2c, sc2c, g2c = jnp.split(mod_c[:, None, :], 6, axis=-1)
        hx = modulate(h, norm_mix_g[l], sh1, sc1)
        hc = modulate(hcs, norm_mix_g[l], sh1c, sc1c)
        mix_x, mix_c = mixer_block(hx, hc, lp, rope_a, rope_b, last)
        h = h + g1 * mix_x
        fx = modulate(h, norm_ffn_g[l], sh2, sc2).reshape(B * N, D)
        if last:
            ffn = hier_moe(fx, w_rg[l], b_rg[l], w_re[l], b_re[l], w_gu[l], w_dn[l])
            h = h + g2 * ffn.reshape(B, N, D)
        else:
            hcs = hcs + g1c * mix_c
            fc = modulate(hcs, norm_ffn_g[l], sh2c, sc2c).reshape(B * Lc, D)
            ffn = hier_moe(jnp.concatenate([fx, fc], axis=0), w_rg[l], b_rg[l], w_re[l], b_re[l],
                           w_gu[l], w_dn[l])
            h = h + g2 * ffn[:B * N].reshape(B, N, D)
            hcs = hcs + g2c * ffn[B * N:].reshape(B, Lc, D)
    return rmsnorm(h, final_g)
```

```python
import functools
import math

import jax
import jax.numpy as jnp
import numpy as np
from jax import lax
from jax.experimental import pallas as pl
from jax.experimental.pallas import tpu as pltpu

GRID_W = 64
ROPE_BASE = 10000.0
EPS = 1e-6
NEG_INF = -1e30

A_HEADS = 8
A_KV_HEADS = 2
A_GROUP = A_HEADS // A_KV_HEADS
A_HEAD_DIM = 64
A_WIDTH = A_HEADS * A_HEAD_DIM
A_KV_WIDTH = A_KV_HEADS * A_HEAD_DIM
WINDOW = 128

B_HEADS = 8
B_NOPE = 64
B_ROPE = 32
B_VDIM = 64
B_WIDTH = B_HEADS * B_VDIM
Q_LORA = 256
KV_LORA = 256

POOL_WINDOWS = (2, 4, 8, 16)
C_GROUPS = 4
C_GROUP_DIM = 128
C_WIDTH = C_GROUPS * C_GROUP_DIM

N_BRANCH = 3
N_GROUPS = 4
EXPERTS_PER_GROUP = 8
N_EXPERTS = N_GROUPS * EXPERTS_PER_GROUP
TOP_K = 2
EXPERT_FF = 256

LANES = 128
BF16_SUBLANES = 16
VMEM_LIMIT_BYTES = 56 * 1024 * 1024

B_HEAD_PAD = LANES
B_QK_WIDTH = B_HEADS * B_HEAD_PAD
POOL_HALO = BF16_SUBLANES
ROW_TILE = 256
MLA_Q_TILE = 256
MOE_BLOCK = 256
ROUTER_LANES = LANES

_SEG = {}
_off = 0
for _name, _w in (("qa", A_WIDTH), ("ka", A_KV_WIDTH), ("va", A_KV_WIDTH), ("cq", Q_LORA),
                  ("ckv", KV_LORA), ("kr", LANES), ("u", C_WIDTH), ("gates", None)):
    _SEG[_name] = _off
    if _w is not None:
        _off += _w
W1_GATES_OFF = _SEG["gates"]

F32 = jnp.float32
BF16 = jnp.bfloat16


def _dot(a, b):
    return jnp.dot(a, b, preferred_element_type=F32)


def _dot_nt(a, b):
    return lax.dot_general(a, b, (((1,), (1,)), ((), ())), preferred_element_type=F32)


def _cparams(sem):
    return pltpu.CompilerParams(dimension_semantics=sem, vmem_limit_bytes=VMEM_LIMIT_BYTES)


def _rms(x, g):
    return x * lax.rsqrt(jnp.mean(x * x, axis=-1, keepdims=True) + EPS) * g


def _modvec_kernel(c_ref, w_ref, b_ref, o_ref):
    c = c_ref[...]
    a = (c * jax.nn.sigmoid(c)).astype(BF16)
    o_ref[...] = _dot(a, w_ref[...].astype(BF16)) + b_ref[...]


def _modvec(cvec, w_mod, b_mod):
    rows, d = cvec.shape
    n = w_mod.shape[1]
    tn = d
    return pl.pallas_call(
        _modvec_kernel,
        grid=(n // tn,),
        in_specs=[pl.BlockSpec((rows, d), lambda j: (0, 0)),
                  pl.BlockSpec((d, tn), lambda j: (0, j)),
                  pl.BlockSpec((1, tn), lambda j: (0, j))],
        out_specs=pl.BlockSpec((rows, tn), lambda j: (0, j)),
        out_shape=jax.ShapeDtypeStruct((rows, n), F32),
        compiler_params=_cparams(("arbitrary",)),
        name="modvec",
    )(cvec, w_mod, b_mod.reshape(1, n))


def _rope_tile(x, cos, sin, half, period):
    lane = lax.broadcasted_iota(jnp.int32, x.shape, 1)
    fwd = pltpu.roll(x, LANES - half, 1)
    bwd = pltpu.roll(x, half, 1)
    partner = jnp.where((lane % period) < half, fwd, bwd)
    return x * cos + partner * sin


def _inproj_kernel(tmod_ref, trope_ref, h_ref, mod_ref, g_ref, w1_ref, qng_ref, kvng_ref, wuq_ref,
                   wuk_ref, wuv_ref, ca_ref, sa_ref, cb_ref, sb_ref,
                   qa_o, ka_o, va_o, qb_o, kb_o, vb_o, u_o, gs_o):
    del tmod_ref, trope_ref
    x = h_ref[...]
    m = mod_ref[0]
    hx = _rms(x, g_ref[...]) * (1.0 + m[1:2]) + m[0:1]
    hb = hx.astype(BF16)

    def seg(name, width):
        off = _SEG[name]
        return _dot(hb, w1_ref[:, off:off + width])

    ca, sa = ca_ref[...], sa_ref[...]
    cb, sb = cb_ref[...], sb_ref[...]

    qa = seg("qa", A_WIDTH)
    for t in range(A_WIDTH // LANES):
        sl = slice(t * LANES, (t + 1) * LANES)
        qa_o[:, sl] = _rope_tile(qa[:, sl], ca, sa, A_HEAD_DIM // 2, A_HEAD_DIM).astype(BF16)
    ka_o[...] = _rope_tile(seg("ka", A_KV_WIDTH), ca, sa, A_HEAD_DIM // 2, A_HEAD_DIM).astype(BF16)
    va_o[...] = seg("va", A_KV_WIDTH).astype(BF16)

    cqn = _rms(seg("cq", Q_LORA), qng_ref[...]).astype(BF16)
    ckvn = _rms(seg("ckv", KV_LORA), kvng_ref[...]).astype(BF16)
    qb = _dot(cqn, wuq_ref[...])
    kb = _dot(ckvn, wuk_ref[...])
    kr = _rope_tile(seg("kr", LANES), cb, sb, B_ROPE // 2, LANES)
    qscale = (B_NOPE + B_ROPE) ** -0.5
    for t in range(B_HEADS):
        sl = slice(t * LANES, (t + 1) * LANES)
        qb_o[:, sl] = (_rope_tile(qb[:, sl], cb, sb, B_ROPE // 2, LANES) * qscale).astype(BF16)
        kb_o[:, sl] = (kb[:, sl] + kr).astype(BF16)
    vb_o[...] = _dot(ckvn, wuv_ref[...]).astype(BF16)

    u_o[...] = seg("u", C_WIDTH).astype(BF16)
    ngate = gs_o.shape[1]
    for t in range(ngate // 512):
        off = W1_GATES_OFF + t * 512
        gs_o[:, t * 512:(t + 1) * 512] = jax.nn.sigmoid(_dot(hb, w1_ref[:, off:off + 512])).astype(BF16)


def _inproj(h_all, mod, norm_g, lw, tables, tile_mod, tile_rope):
    t_all, d = h_all.shape
    tm = ROW_TILE
    nt = t_all // tm
    w1 = lw["w1"]
    ngate = N_BRANCH * d
    row = lambda i, *_: (i, 0)
    const = lambda i, *_: (0, 0)
    rope = lambda i, tmod, trope: (trope[i], 0)
    widths = (A_WIDTH, A_KV_WIDTH, A_KV_WIDTH, B_QK_WIDTH, B_QK_WIDTH, B_WIDTH, C_WIDTH, ngate)
    grid_spec = pltpu.PrefetchScalarGridSpec(
        num_scalar_prefetch=2,
        grid=(nt,),
        in_specs=[pl.BlockSpec((tm, d), row),
                  pl.BlockSpec((1, 8, d), lambda i, tmod, trope: (tmod[i], 0, 0)),
                  pl.BlockSpec((1, d), const),
                  pl.BlockSpec(w1.shape, const),
                  pl.BlockSpec((1, Q_LORA), const),
                  pl.BlockSpec((1, KV_LORA), const),
                  pl.BlockSpec(lw["wuq"].shape, const),
                  pl.BlockSpec(lw["wuk"].shape, const),
                  pl.BlockSpec(lw["wuv"].shape, const),
                  pl.BlockSpec((tm, LANES), rope),
                  pl.BlockSpec((tm, LANES), rope),
                  pl.BlockSpec((tm, LANES), rope),
                  pl.BlockSpec((tm, LANES), rope)],
        out_specs=[pl.BlockSpec((tm, w), row) for w in widths],
    )
    return pl.pallas_call(
        _inproj_kernel,
        grid_spec=grid_spec,
        out_shape=[jax.ShapeDtypeStruct((t_all, w), BF16) for w in widths],
        compiler_params=_cparams(("arbitrary",)),
        name="inproj",
    )(tile_mod, tile_rope, h_all, mod, norm_g, w1, lw["qng"], lw["kvng"], lw["wuq"], lw["wuk"],
      lw["wuv"], tables["ca"], tables["sa"], tables["cb"], tables["sb"])


def _gqa_kernel(sink_ref, q_ref, *refs, local, nblk):
    if local:
        kp_ref, kc_ref, kn_ref, vp_ref, vc_ref, vn_ref, kx_ref, vx_ref, o_ref = refs
    else:
        kx_ref, vx_ref, o_ref = refs
    q = q_ref[...]
    rows = q.shape[0]
    if local:
        j = pl.program_id(1)
        r = lax.broadcasted_iota(jnp.int32, (rows, WINDOW), 0)
        c = lax.broadcasted_iota(jnp.int32, (rows, WINDOW), 1)
        mask_prev = jnp.logical_and(c >= r, j > 0)
        mask_next = jnp.logical_and(c <= r, j < nblk - 1)
        kcat = jnp.concatenate([kp_ref[...], kc_ref[...], kn_ref[...], kx_ref[...]], axis=0)
        vcat = jnp.concatenate([vp_ref[...], vc_ref[...], vn_ref[...], vx_ref[...]], axis=0)
    else:
        kcat, vcat = kx_ref[...], vx_ref[...]
    for h in range(A_HEADS):
        g = h // A_GROUP
        qh = q[:, h * A_HEAD_DIM:(h + 1) * A_HEAD_DIM]
        kg = kcat[:, g * A_HEAD_DIM:(g + 1) * A_HEAD_DIM]
        vg = vcat[:, g * A_HEAD_DIM:(g + 1) * A_HEAD_DIM]
        s = _dot_nt(qh, kg)
        if local:
            s = jnp.concatenate([jnp.where(mask_prev, s[:, :WINDOW], NEG_INF),
                                 s[:, WINDOW:2 * WINDOW],
                                 jnp.where(mask_next, s[:, 2 * WINDOW:3 * WINDOW], NEG_INF),
                                 s[:, 3 * WINDOW:]], axis=1)
        sk = sink_ref[h]
        mx = jnp.maximum(jnp.max(s, axis=-1, keepdims=True), sk)
        e = jnp.exp(s - mx)
        denom = jnp.sum(e, axis=-1, keepdims=True) + jnp.exp(sk - mx)
        o = _dot(e.astype(BF16), vg) / denom
        o_ref[:, h * A_HEAD_DIM:(h + 1) * A_HEAD_DIM] = o.astype(BF16)


def _window_attn(qa, ka, va, sink, batch, n, lc):
    t = batch * n
    nblk = n // WINDOW
    cblk = t // lc
    qmap = lambda b, j: (b * nblk + j, 0)
    prev = lambda b, j: (b * nblk + jnp.maximum(j - 1, 0), 0)
    nxt = lambda b, j: (b * nblk + jnp.minimum(j + 1, nblk - 1), 0)
    ctx = lambda b, j: (cblk + b, 0)
    kvw = A_KV_WIDTH
    return pl.pallas_call(
        functools.partial(_gqa_kernel, local=True, nblk=nblk),
        grid=(batch, nblk),
        in_specs=[pl.BlockSpec(memory_space=pltpu.SMEM),
                  pl.BlockSpec((WINDOW, A_WIDTH), qmap),
                  pl.BlockSpec((WINDOW, kvw), prev), pl.BlockSpec((WINDOW, kvw), qmap),
                  pl.BlockSpec((WINDOW, kvw), nxt),
                  pl.BlockSpec((WINDOW, kvw), prev), pl.BlockSpec((WINDOW, kvw), qmap),
                  pl.BlockSpec((WINDOW, kvw), nxt),
                  pl.BlockSpec((lc, kvw), ctx), pl.BlockSpec((lc, kvw), ctx)],
        out_specs=pl.BlockSpec((WINDOW, A_WIDTH), qmap),
        out_shape=jax.ShapeDtypeStruct((t, A_WIDTH), BF16),
        compiler_params=_cparams(("arbitrary", "arbitrary")),
        name="window_attn",
    )(sink, qa, ka, ka, ka, va, va, va, ka, va)


def _ctx_gqa_attn(qa, ka, va, sink, batch, n, lc):
    cblk = batch * n // lc
    cmap = lambda b: (cblk + b, 0)
    return pl.pallas_call(
        functools.partial(_gqa_kernel, local=False, nblk=0),
        grid=(batch,),
        in_specs=[pl.BlockSpec(memory_space=pltpu.SMEM),
                  pl.BlockSpec((lc, A_WIDTH), cmap),
                  pl.BlockSpec((lc, A_KV_WIDTH), cmap), pl.BlockSpec((lc, A_KV_WIDTH), cmap)],
        out_specs=pl.BlockSpec((lc, A_WIDTH), lambda b: (b, 0)),
        out_shape=jax.ShapeDtypeStruct((batch * lc, A_WIDTH), BF16),
        compiler_params=_cparams(("arbitrary",)),
        name="ctx_gqa_attn",
    )(sink, qa, ka, va)


def _mla_kernel(q_ref, kc_ref, vc_ref, *refs, with_latent):
    if with_latent:
        kx_ref, vx_ref, o_ref = refs
    else:
        (o_ref,) = refs
    for h in range(B_HEADS):
        ql = slice(h * B_HEAD_PAD, (h + 1) * B_HEAD_PAD)
        vl = slice(h * B_VDIM, (h + 1) * B_VDIM)
        qh = q_ref[:, ql]
        s_c = _dot_nt(qh, kc_ref[:, ql])
        mx = jnp.max(s_c, axis=-1, keepdims=True)
        if with_latent:
            s_x = _dot_nt(qh, kx_ref[:, ql])
            mx = jnp.maximum(mx, jnp.max(s_x, axis=-1, keepdims=True))
        e_c = jnp.exp(s_c - mx)
        denom = jnp.sum(e_c, axis=-1, keepdims=True)
        o = _dot(e_c.astype(BF16), vc_ref[:, vl])
        if with_latent:
            e_x = jnp.exp(s_x - mx)
            denom = denom + jnp.sum(e_x, axis=-1, keepdims=True)
            o = o + _dot(e_x.astype(BF16), vx_ref[:, vl])
        o_ref[:, vl] = (o / denom).astype(BF16)


def _mla_attn(qb, kb, vb, batch, n, lc):
    t = batch * n
    tq = min(MLA_Q_TILE, n)
    nq = n // tq
    cblk = t // lc
    qmap = lambda b, j: (b * nq + j, 0)
    ctx = lambda b, j: (cblk + b, 0)
    lat = lambda b, j: (b, 0)
    return pl.pallas_call(
        functools.partial(_mla_kernel, with_latent=True),
        grid=(batch, nq),
        in_specs=[pl.BlockSpec((tq, B_QK_WIDTH), qmap),
                  pl.BlockSpec((lc, B_QK_WIDTH), ctx), pl.BlockSpec((lc, B_WIDTH), ctx),
                  pl.BlockSpec((n, B_QK_WIDTH), lat), pl.BlockSpec((n, B_WIDTH), lat)],
        out_specs=pl.BlockSpec((tq, B_WIDTH), qmap),
        out_shape=jax.ShapeDtypeStruct((t, B_WIDTH), BF16),
        compiler_params=_cparams(("arbitrary", "arbitrary")),
        name="mla_attn",
    )(qb, kb, vb, kb, vb)


def _mla_ctx_attn(qb, kb, vb, batch, n, lc):
    cblk = batch * n // lc
    cmap = lambda b: (cblk + b, 0)
    return pl.pallas_call(
        functools.partial(_mla_kernel, with_latent=False),
        grid=(batch,),
        in_specs=[pl.BlockSpec((lc, B_QK_WIDTH), cmap),
                  pl.BlockSpec((lc, B_QK_WIDTH), cmap), pl.BlockSpec((lc, B_WIDTH), cmap)],
        out_specs=pl.BlockSpec((lc, B_WIDTH), lambda b: (b, 0)),
        out_shape=jax.ShapeDtypeStruct((batch * lc, B_WIDTH), BF16),
        compiler_params=_cparams(("arbitrary",)),
        name="mla_ctx_attn",
    )(qb, kb, vb)


def _merge_kernel(tmod_ref, tpos_ref, tlen_ref, h_ref, mod_ref, g_ref, oa_ref, ob_ref, u_ref, up_ref,
                  un_ref, gs_ref, wpool_ref, pscale_ref, wa_ref, wb_ref, wc_ref, wo_ref, wr_ref, br_ref,
                  hn_o, fx_o, lg_o):
    del tmod_ref
    i = pl.program_id(0)
    pos0 = tpos_ref[i]
    seq_len = tlen_ref[i]
    tm = u_ref.shape[0]
    d = h_ref.shape[1]

    u = u_ref[...]
    kdim = tm + LANES
    zpad = jnp.zeros((LANES - 2 * POOL_HALO, u.shape[1]), BF16)
    ucat = jnp.concatenate([up_ref[...], u, un_ref[...], zpad], axis=0)
    uf = u.astype(F32)
    r_i = lax.broadcasted_iota(jnp.int32, (tm, kdim), 0)
    c_i = lax.broadcasted_iota(jnp.int32, (tm, kdim), 1)
    rel = c_i - POOL_HALO - r_i
    kpos = pos0 + c_i - POOL_HALO
    valid = jnp.logical_and(kpos >= 0, kpos < seq_len)
    tpos = pos0 + lax.broadcasted_iota(jnp.int32, (tm, 1), 0)
    oc_parts = []
    for gi, w in enumerate(POOL_WINDOWS):
        rad = w // 2
        sl = slice(gi * C_GROUP_DIM, (gi + 1) * C_GROUP_DIM)
        band = jnp.where(jnp.logical_and(jnp.abs(rel) <= rad, valid), 1.0, 0.0).astype(BF16)
        win_sum = _dot(band, ucat[:, sl])
        cnt = (jnp.minimum(tpos + rad + 1, seq_len) - jnp.maximum(tpos - rad, 0)).astype(F32)
        pooled = (win_sum / cnt - uf[:, sl]).astype(BF16)
        oc_parts.append(_dot(pooled, wpool_ref[gi]))
    oc = (jnp.concatenate(oc_parts, axis=1) * pscale_ref[...]).astype(BF16)

    y = gs_ref[:, 0:d].astype(F32) * _dot(oa_ref[...], wa_ref[...])
    y = y + gs_ref[:, d:2 * d].astype(F32) * _dot(ob_ref[...], wb_ref[...])
    y = y + gs_ref[:, 2 * d:3 * d].astype(F32) * _dot(oc, wc_ref[...])
    mix = _dot(y.astype(BF16), wo_ref[...])

    m = mod_ref[0]
    hn = h_ref[...] + m[2:3] * mix
    hn_o[...] = hn
    fx = _rms(hn, g_ref[...]) * (1.0 + m[4:5]) + m[3:4]
    fx_hi = fx.astype(BF16)
    fx_lo = (fx - fx_hi.astype(F32)).astype(BF16)
    fx_o[...] = fx_hi
    lg_o[...] = (_dot(fx_hi, wr_ref[0]) + _dot(fx_lo, wr_ref[0]) + _dot(fx_hi, wr_ref[1])) + br_ref[...]


def _merge(h_all, mod, norm_g, oa, ob, u, gs, lw, rows, tile_mod, tile_pos, tile_len):
    t_all, d = h_all.shape
    tm = ROW_TILE
    nt = rows // tm
    hb = tm // POOL_HALO
    nhalo = u.shape[0] // POOL_HALO
    row = lambda i, *_: (i, 0)
    const = lambda i, *_: (0, 0)
    const3 = lambda i, *_: (0, 0, 0)
    grid_spec = pltpu.PrefetchScalarGridSpec(
        num_scalar_prefetch=3,
        grid=(nt,),
        in_specs=[pl.BlockSpec((tm, d), row),
                  pl.BlockSpec((1, 8, d), lambda i, tmod, *_: (tmod[i], 0, 0)),
                  pl.BlockSpec((1, d), const),
                  pl.BlockSpec((tm, A_WIDTH), row),
                  pl.BlockSpec((tm, B_WIDTH), row),
                  pl.BlockSpec((tm, C_WIDTH), row),
                  pl.BlockSpec((POOL_HALO, C_WIDTH), lambda i, *_: (jnp.maximum(i * hb - 1, 0), 0)),
                  pl.BlockSpec((POOL_HALO, C_WIDTH), lambda i, *_: (jnp.minimum((i + 1) * hb, nhalo - 1), 0)),
                  pl.BlockSpec((tm, N_BRANCH * d), row),
                  pl.BlockSpec(lw["wpool"].shape, const3),
                  pl.BlockSpec((1, C_WIDTH), const),
                  pl.BlockSpec(lw["wa"].shape, const),
                  pl.BlockSpec(lw["wb"].shape, const),
                  pl.BlockSpec(lw["wc"].shape, const),
                  pl.BlockSpec(lw["wo"].shape, const),
                  pl.BlockSpec(lw["wr"].shape, const3),
                  pl.BlockSpec((1, ROUTER_LANES), const)],
        out_specs=[pl.BlockSpec((tm, d), row), pl.BlockSpec((tm, d), row),
                   pl.BlockSpec((tm, ROUTER_LANES), row)],
    )
    return pl.pallas_call(
        _merge_kernel,
        grid_spec=grid_spec,
        out_shape=[jax.ShapeDtypeStruct((rows, d), F32), jax.ShapeDtypeStruct((rows, d), BF16),
                   jax.ShapeDtypeStruct((rows, ROUTER_LANES), F32)],
        compiler_params=_cparams(("arbitrary",)),
        name="merge",
    )(tile_mod, tile_pos, tile_len, h_all, mod, norm_g, oa, ob, u, u, u, gs, lw["wpool"],
      lw["pscale"], lw["wa"], lw["wb"], lw["wc"], lw["wo"], lw["wr"], lw["br"])


def _moe_kernel(blk_e_ref, nused_ref, x_ref, gate_ref, wgu_ref, wdn_ref, o_ref):
    del blk_e_ref
    i = pl.program_id(0)

    @pl.when(i < nused_ref[0])
    def _():
        gu = _dot(x_ref[...], wgu_ref[0].astype(BF16))
        gt, up = gu[:, :EXPERT_FF], gu[:, EXPERT_FF:]
        act = (gt * jax.nn.sigmoid(gt) * up).astype(BF16)
        y = _dot(act, wdn_ref[0].astype(BF16))
        o_ref[...] = (y * gate_ref[...]).astype(o_ref.dtype)

    @pl.when(i >= nused_ref[0])
    def _():
        o_ref[...] = jnp.zeros_like(o_ref)


def _moe(xg, gate_slot, blk_e, n_used, w_gu, w_dn):
    p, d = xg.shape
    nb = p // MOE_BLOCK
    grid_spec = pltpu.PrefetchScalarGridSpec(
        num_scalar_prefetch=2,
        grid=(nb,),
        in_specs=[pl.BlockSpec((MOE_BLOCK, d), lambda i, *_: (i, 0)),
                  pl.BlockSpec((MOE_BLOCK, 1), lambda i, *_: (i, 0)),
                  pl.BlockSpec((1, d, 2 * EXPERT_FF), lambda i, be, nu: (be[i], 0, 0)),
                  pl.BlockSpec((1, EXPERT_FF, d), lambda i, be, nu: (be[i], 0, 0))],
        out_specs=pl.BlockSpec((MOE_BLOCK, d), lambda i, *_: (i, 0)),
    )
    return pl.pallas_call(
        _moe_kernel,
        grid_spec=grid_spec,
        out_shape=jax.ShapeDtypeStruct((p, d), BF16),
        compiler_params=_cparams(("arbitrary",)),
        name="moe_experts",
    )(blk_e, n_used, xg, gate_slot, w_gu, w_dn)


def _resid_kernel(tmod_ref, h_ref, mod_ref, y0_ref, y1_ref, fg_ref, o_ref, *, final):
    del tmod_ref
    m = mod_ref[0]
    hn = h_ref[...] + m[5:6] * (y0_ref[...].astype(F32) + y1_ref[...].astype(F32))
    if final:
        hn = _rms(hn, fg_ref[...])
    o_ref[...] = hn


def _resid(h, mod, y0, y1, final_g, tile_mod, final):
    rows, d = h.shape
    tm = ROW_TILE
    row = lambda i, *_: (i, 0)
    grid_spec = pltpu.PrefetchScalarGridSpec(
        num_scalar_prefetch=1,
        grid=(rows // tm,),
        in_specs=[pl.BlockSpec((tm, d), row),
                  pl.BlockSpec((1, 8, d), lambda i, tmod: (tmod[i], 0, 0)),
                  pl.BlockSpec((tm, d), row), pl.BlockSpec((tm, d), row),
                  pl.BlockSpec((1, d), lambda i, *_: (0, 0))],
        out_specs=pl.BlockSpec((tm, d), row),
    )
    return pl.pallas_call(
        functools.partial(_resid_kernel, final=final),
        grid_spec=grid_spec,
        out_shape=jax.ShapeDtypeStruct((rows, d), F32),
        compiler_params=_cparams(("arbitrary",)),
        name="ffn_residual",
    )(tile_mod, h, mod, y0, y1, final_g)


def _deinterleave(n):
    return np.concatenate([np.arange(0, n, 2), np.arange(1, n, 2)])


def _rope_tables(n, tm):
    rows = n // GRID_W
    row = np.repeat(np.arange(rows), GRID_W).astype(np.float32)
    col = np.tile(np.arange(GRID_W), rows).astype(np.float32)

    def cs(rot_dim):
        axis_dim = rot_dim // 2
        inv = jnp.asarray(ROPE_BASE, F32) ** (-jnp.arange(0, axis_dim, 2, dtype=F32) / axis_dim)
        ang = jnp.concatenate([jnp.asarray(row)[:, None] * inv, jnp.asarray(col)[:, None] * inv], axis=-1)
        return jnp.cos(ang), jnp.sin(ang)

    cos_a, sin_a = cs(A_HEAD_DIM)
    ca = jnp.tile(jnp.concatenate([cos_a, cos_a], axis=1), (1, LANES // A_HEAD_DIM))
    sa = jnp.tile(jnp.concatenate([-sin_a, sin_a], axis=1), (1, LANES // A_HEAD_DIM))
    cos_b, sin_b = cs(B_ROPE)
    ones = jnp.ones((n, B_NOPE), F32)
    tail = LANES - B_NOPE - B_ROPE
    cb = jnp.concatenate([ones, cos_b, cos_b, jnp.ones((n, tail), F32)], axis=1)
    sb = jnp.concatenate([0 * ones, -sin_b, sin_b, jnp.zeros((n, tail), F32)], axis=1)
    ident_c = jnp.ones((tm, LANES), F32)
    ident_s = jnp.zeros((tm, LANES), F32)
    return {"ca": jnp.concatenate([ca, ident_c]), "sa": jnp.concatenate([sa, ident_s]),
            "cb": jnp.concatenate([cb, ident_c]), "sb": jnp.concatenate([sb, ident_s])}


def _layer_weights(l, w_in, q_norm_g, w_uq, kv_norm_g, w_ukv, w_pool, pool_scale, w_br_a, w_br_b,
                   w_br_c, w_out, w_rg, b_rg, w_re, b_re):
    d = w_in.shape[1]
    wi = w_in[l]
    splits = np.cumsum([A_WIDTH, A_KV_WIDTH, A_KV_WIDTH, Q_LORA, KV_LORA, B_ROPE, C_WIDTH])
    qa, ka, va, cq, ckv, kr, u, gates = jnp.split(wi, splits, axis=1)
    pa = _deinterleave(A_HEAD_DIM)
    qa = qa.reshape(d, A_HEADS, A_HEAD_DIM)[:, :, pa].reshape(d, A_WIDTH) * (A_HEAD_DIM ** -0.5)
    ka = ka.reshape(d, A_KV_HEADS, A_HEAD_DIM)[:, :, pa].reshape(d, A_KV_WIDTH)
    pb = _deinterleave(B_ROPE)
    tail = LANES - B_NOPE - B_ROPE
    kr128 = jnp.concatenate([jnp.zeros((d, B_NOPE), F32), kr[:, pb], jnp.zeros((d, tail), F32)], axis=1)
    w1 = jnp.concatenate([qa, ka, va, cq, ckv, kr128, u, gates], axis=1).astype(BF16)

    uq = w_uq[l].reshape(Q_LORA, B_HEADS, B_NOPE + B_ROPE)
    uq = jnp.concatenate([uq[:, :, :B_NOPE], uq[:, :, B_NOPE:][:, :, pb],
                          jnp.zeros((Q_LORA, B_HEADS, tail), F32)], axis=2)
    ukv = w_ukv[l].reshape(KV_LORA, B_HEADS, B_NOPE + B_VDIM)
    uk = jnp.concatenate([ukv[:, :, :B_NOPE], jnp.zeros((KV_LORA, B_HEADS, LANES - B_NOPE), F32)], axis=2)
    uv = ukv[:, :, B_NOPE:]

    wr = jnp.concatenate([w_rg[l], w_re[l], jnp.zeros((d, ROUTER_LANES - N_GROUPS - N_EXPERTS), F32)], axis=1)
    wr_hi = wr.astype(BF16)
    wr_lo = (wr - wr_hi.astype(F32)).astype(BF16)
    br = jnp.concatenate([b_rg[l], b_re[l], jnp.zeros((ROUTER_LANES - N_GROUPS - N_EXPERTS,), F32)])
    return {
        "w1": w1,
        "qng": q_norm_g[l].reshape(1, Q_LORA), "kvng": kv_norm_g[l].reshape(1, KV_LORA),
        "wuq": uq.reshape(Q_LORA, B_QK_WIDTH).astype(BF16),
        "wuk": uk.reshape(KV_LORA, B_QK_WIDTH).astype(BF16),
        "wuv": uv.reshape(KV_LORA, B_WIDTH).astype(BF16),
        "wpool": w_pool[l].astype(BF16), "pscale": pool_scale[l].reshape(1, C_WIDTH),
        "wa": w_br_a[l].astype(BF16), "wb": w_br_b[l].astype(BF16), "wc": w_br_c[l].astype(BF16),
        "wo": w_out[l].astype(BF16),
        "wr": jnp.stack([wr_hi, wr_lo]), "br": br.reshape(1, ROUTER_LANES),
    }


def _route(logits):
    t = logits.shape[0]
    g_logits = logits[:, :N_GROUPS]
    e_logits = logits[:, N_GROUPS:N_GROUPS + N_EXPERTS].reshape(t, N_GROUPS, EXPERTS_PER_GROUP)
    g_sel = jnp.argmax(g_logits, axis=-1)
    g_p = jnp.take_along_axis(jax.nn.softmax(g_logits, axis=-1), g_sel[:, None], axis=1)
    e_grp = jnp.take_along_axis(e_logits, g_sel[:, None, None], axis=1)[:, 0]
    top_v, top_i = lax.top_k(e_grp, TOP_K)
    gate = jax.nn.softmax(top_v, axis=-1) * g_p
    expert = (g_sel[:, None] * EXPERTS_PER_GROUP + top_i).astype(jnp.int32)
    a = t * TOP_K
    flat_e = expert.reshape(a)
    onehot = (flat_e[:, None] == jnp.arange(N_EXPERTS, dtype=jnp.int32)[None, :]).astype(jnp.int32)
    csum = jnp.cumsum(onehot, axis=0)
    rank = jnp.take_along_axis(csum, flat_e[:, None], axis=1)[:, 0] - 1
    counts = csum[-1]
    padded = (counts + MOE_BLOCK - 1) // MOE_BLOCK * MOE_BLOCK
    pad_end = jnp.cumsum(padded)
    pad_start = pad_end - padded
    slot = (pad_start[flat_e] + rank).astype(jnp.int32)
    n_blocks = -(-(a + N_EXPERTS * (MOE_BLOCK - 1)) // MOE_BLOCK)
    p = n_blocks * MOE_BLOCK
    tok_of_slot = jnp.zeros((p,), jnp.int32).at[slot].set(jnp.arange(a, dtype=jnp.int32) // TOP_K)
    gate_of_slot = jnp.zeros((p,), F32).at[slot].set(gate.reshape(a))
    blk_e = jnp.minimum(jnp.searchsorted(pad_end, jnp.arange(n_blocks, dtype=jnp.int32) * MOE_BLOCK,
                                         side="right"), N_EXPERTS - 1).astype(jnp.int32)
    n_used = (pad_end[-1] // MOE_BLOCK).astype(jnp.int32).reshape(1)
    return slot.reshape(t, TOP_K), tok_of_slot, gate_of_slot.reshape(p, 1), blk_e, n_used


def kernel(x, c, ctx, c_ctx, w_mod, b_mod, norm_mix_g, norm_ffn_g, w_in, sink, q_norm_g, w_uq, kv_norm_g,
           w_ukv, w_pool, pool_scale, w_br_a, w_br_b, w_br_c, w_out, w_rg, b_rg, w_re, b_re, w_gu, w_dn,
           final_g):
    batch, n, d = x.shape
    lc = ctx.shape[1]
    depth = w_mod.shape[0]
    tm = ROW_TILE
    assert n % tm == 0 and lc % tm == 0 and n % WINDOW == 0 and (batch * n) % lc == 0
    t = batch * n
    tc = batch * lc
    nt_lat, nt_ctx = t // tm, tc // tm
    per_b, per_c = n // tm, lc // tm

    lat_i = np.arange(nt_lat)
    ctx_i = np.arange(nt_ctx)
    tile_mod = jnp.asarray(np.concatenate([lat_i // per_b, np.full(nt_ctx, batch)]), jnp.int32)
    tile_rope = jnp.asarray(np.concatenate([lat_i % per_b, np.full(nt_ctx, per_b)]), jnp.int32)
    tile_pos = jnp.asarray(np.concatenate([(lat_i % per_b) * tm, (ctx_i % per_c) * tm]), jnp.int32)
    tile_len = jnp.asarray(np.concatenate([np.full(nt_lat, n), np.full(nt_ctx, lc)]), jnp.int32)
    tables = _rope_tables(n, tm)

    mod_rows = 16
    cvec = jnp.concatenate([c, c_ctx[None, :], jnp.zeros((mod_rows - batch - 1, d), F32)], axis=0)
    h_all = jnp.concatenate([x.reshape(t, d), ctx.reshape(tc, d)], axis=0)

    out = None
    for l in range(depth):
        last = l == depth - 1
        lw = _layer_weights(l, w_in, q_norm_g, w_uq, kv_norm_g, w_ukv, w_pool, pool_scale, w_br_a,
                            w_br_b, w_br_c, w_out, w_rg, b_rg, w_re, b_re)
        mod = _modvec(cvec, w_mod[l], b_mod[l]).reshape(mod_rows, 6, d)
        mod = jnp.concatenate([mod, jnp.zeros((mod_rows, 2, d), F32)], axis=1)

        qa, ka, va, qb, kb, vb, u, gs = _inproj(h_all, mod, norm_mix_g[l].reshape(1, d), lw, tables,
                                                tile_mod, tile_rope)
        oa = _window_attn(qa, ka, va, sink[l], batch, n, lc)
        ob = _mla_attn(qb, kb, vb, batch, n, lc)
        if last:
            rows = t
        else:
            rows = t + tc
            oa = jnp.concatenate([oa, _ctx_gqa_attn(qa, ka, va, sink[l], batch, n, lc)], axis=0)
            ob = jnp.concatenate([ob, _mla_ctx_attn(qb, kb, vb, batch, n, lc)], axis=0)
        hn, fx, logits = _merge(h_all, mod, norm_ffn_g[l].reshape(1, d), oa, ob, u, gs, lw, rows,
                                tile_mod, tile_pos, tile_len)

        slot, tok_of_slot, gate_slot, blk_e, n_used = _route(logits)
        xg = jnp.take(fx, tok_of_slot, axis=0)
        yb = _moe(xg, gate_slot, blk_e, n_used, w_gu[l], w_dn[l])
        y0 = jnp.take(yb, slot[:, 0], axis=0)
        y1 = jnp.take(yb, slot[:, 1], axis=0)
        res = _resid(hn, mod, y0, y1, final_g.reshape(1, d), tile_mod, last)
        if last:
            out = res
        else:
            h_all = res
    return out.reshape(batch, n, d)
```

```python
import functools
import math

import jax
import jax.numpy as jnp
import numpy as np
from jax import lax
from jax.experimental import pallas as pl
from jax.experimental.pallas import tpu as pltpu

GRID_W = 64
ROPE_BASE = 10000.0
EPS = 1e-6
NEG_INF = -1e30

A_HEADS = 8
A_KV_HEADS = 2
A_GROUP = A_HEADS // A_KV_HEADS
A_HEAD_DIM = 64
A_WIDTH = A_HEADS * A_HEAD_DIM
A_KV_WIDTH = A_KV_HEADS * A_HEAD_DIM
WINDOW = 128

B_HEADS = 8
B_NOPE = 64
B_ROPE = 32
B_VDIM = 64
B_WIDTH = B_HEADS * B_VDIM
Q_LORA = 256
KV_LORA = 256

POOL_WINDOWS = (2, 4, 8, 16)
C_GROUPS = 4
C_GROUP_DIM = 128
C_WIDTH = C_GROUPS * C_GROUP_DIM

N_BRANCH = 3
N_GROUPS = 4
EXPERTS_PER_GROUP = 8
N_EXPERTS = N_GROUPS * EXPERTS_PER_GROUP
TOP_K = 2
EXPERT_FF = 256

LANES = 128
BF16_SUBLANES = 16
VMEM_LIMIT_BYTES = 56 * 1024 * 1024

B_HEAD_PAD = LANES
B_QK_WIDTH = B_HEADS * B_HEAD_PAD
POOL_HALO = BF16_SUBLANES
ROW_TILE = 256
MLA_Q_TILE = 256
MLA_KEY_CHUNK = 512
WIN_Q_TILE = 512
MOE_BLOCK = 256
ROUTER_LANES = LANES

_SEG = {}
_off = 0
for _name, _w in (("qa", A_WIDTH), ("ka", A_KV_WIDTH), ("va", A_KV_WIDTH), ("cq", Q_LORA),
                  ("ckv", KV_LORA), ("kr", LANES), ("u", C_WIDTH), ("gates", None)):
    _SEG[_name] = _off
    if _w is not None:
        _off += _w
W1_GATES_OFF = _SEG["gates"]

F32 = jnp.float32
BF16 = jnp.bfloat16


def _dot(a, b):
    return jnp.dot(a, b, preferred_element_type=F32)


def _dot_nt(a, b):
    return lax.dot_general(a, b, (((1,), (1,)), ((), ())), preferred_element_type=F32)


def _cparams(sem):
    return pltpu.CompilerParams(dimension_semantics=sem, vmem_limit_bytes=VMEM_LIMIT_BYTES)


def _rms(x, g):
    return x * lax.rsqrt(jnp.mean(x * x, axis=-1, keepdims=True) + EPS) * g


def _modvec_kernel(c_ref, w_ref, b_ref, o_ref):
    c = c_ref[...]
    a = (c * jax.nn.sigmoid(c)).astype(BF16)
    o_ref[0] = _dot(a, w_ref[0].astype(BF16)) + b_ref[0]


def _modvec(cvec, w_mod, b_mod):
    rows, d = cvec.shape
    depth, _, n = w_mod.shape
    tn = d
    return pl.pallas_call(
        _modvec_kernel,
        grid=(depth, n // tn),
        in_specs=[pl.BlockSpec((rows, d), lambda l, j: (0, 0)),
                  pl.BlockSpec((1, d, tn), lambda l, j: (l, 0, j)),
                  pl.BlockSpec((1, 1, tn), lambda l, j: (l, 0, j))],
        out_specs=pl.BlockSpec((1, rows, tn), lambda l, j: (l, 0, j)),
        out_shape=jax.ShapeDtypeStruct((depth, rows, n), F32),
        compiler_params=_cparams(("arbitrary", "arbitrary")),
        name="modvec",
    )(cvec, w_mod, b_mod.reshape(depth, 1, n))


def _rope_tile(x, cos, sin, half):
    lane = lax.broadcasted_iota(jnp.int32, x.shape, 1)
    fwd = pltpu.roll(x, LANES - half, 1)
    bwd = pltpu.roll(x, half, 1)
    partner = jnp.where((lane % (2 * half)) < half, fwd, bwd)
    return x * cos + partner * sin


def _inproj_kernel(tmod_ref, trope_ref, h_ref, mod_ref, g_ref, w1_ref, qng_ref, kvng_ref, wuq_ref,
                   wuk_ref, wuv_ref, ca_ref, sa_ref, cb_ref, sb_ref,
                   qa_o, ka_o, va_o, qb_o, kb_o, vb_o, u_o, gs_o):
    del tmod_ref, trope_ref
    x = h_ref[...]
    m = mod_ref[0]
    hx = _rms(x, g_ref[...]) * (1.0 + m[1:2]) + m[0:1]
    hb = hx.astype(BF16)

    def seg(name, width):
        off = _SEG[name]
        return _dot(hb, w1_ref[:, off:off + width])

    ca, sa = ca_ref[...], sa_ref[...]
    cb, sb = cb_ref[...], sb_ref[...]

    qa = seg("qa", A_WIDTH)
    for t in range(A_WIDTH // LANES):
        sl = slice(t * LANES, (t + 1) * LANES)
        qa_o[:, sl] = _rope_tile(qa[:, sl], ca, sa, A_HEAD_DIM // 2).astype(BF16)
    ka_o[...] = _rope_tile(seg("ka", A_KV_WIDTH), ca, sa, A_HEAD_DIM // 2).astype(BF16)
    va_o[...] = seg("va", A_KV_WIDTH).astype(BF16)

    cqn = _rms(seg("cq", Q_LORA), qng_ref[...]).astype(BF16)
    ckvn = _rms(seg("ckv", KV_LORA), kvng_ref[...]).astype(BF16)
    qb = _dot(cqn, wuq_ref[...])
    kb = _dot(ckvn, wuk_ref[...])
    kr = _rope_tile(seg("kr", LANES), cb, sb, B_ROPE // 2)
    qscale = (B_NOPE + B_ROPE) ** -0.5 * math.log2(math.e)
    for t in range(B_HEADS):
        sl = slice(t * LANES, (t + 1) * LANES)
        qb_o[:, sl] = (_rope_tile(qb[:, sl], cb, sb, B_ROPE // 2) * qscale).astype(BF16)
        kb_o[:, sl] = (kb[:, sl] + kr).astype(BF16)
    vb_o[...] = _dot(ckvn, wuv_ref[...]).astype(BF16)

    u_o[...] = seg("u", C_WIDTH).astype(BF16)
    ngate = gs_o.shape[1]
    for t in range(ngate // 512):
        off = W1_GATES_OFF + t * 512
        gs_o[:, t * 512:(t + 1) * 512] = jax.nn.sigmoid(_dot(hb, w1_ref[:, off:off + 512])).astype(BF16)


def _inproj(h_all, mod, norm_g, lw, tables, tile_mod, tile_rope):
    t_all, d = h_all.shape
    tm = ROW_TILE
    nt = t_all // tm
    w1 = lw["w1"]
    ngate = N_BRANCH * d
    row = lambda i, *_: (i, 0)
    const = lambda i, *_: (0, 0)
    rope = lambda i, tmod, trope: (trope[i], 0)
    widths = (A_WIDTH, A_KV_WIDTH, A_KV_WIDTH, B_QK_WIDTH, B_QK_WIDTH, B_WIDTH, C_WIDTH, ngate)
    grid_spec = pltpu.PrefetchScalarGridSpec(
        num_scalar_prefetch=2,
        grid=(nt,),
        in_specs=[pl.BlockSpec((tm, d), row),
                  pl.BlockSpec((1, 8, d), lambda i, tmod, trope: (tmod[i], 0, 0)),
                  pl.BlockSpec((1, d), const),
                  pl.BlockSpec(w1.shape, const),
                  pl.BlockSpec((1, Q_LORA), const),
                  pl.BlockSpec((1, KV_LORA), const),
                  pl.BlockSpec(lw["wuq"].shape, const),
                  pl.BlockSpec(lw["wuk"].shape, const),
                  pl.BlockSpec(lw["wuv"].shape, const),
                  pl.BlockSpec((tm, LANES), rope),
                  pl.BlockSpec((tm, LANES), rope),
                  pl.BlockSpec((tm, LANES), rope),
                  pl.BlockSpec((tm, LANES), rope)],
        out_specs=[pl.BlockSpec((tm, w), row) for w in widths],
    )
    return pl.pallas_call(
        _inproj_kernel,
        grid_spec=grid_spec,
        out_shape=[jax.ShapeDtypeStruct((t_all, w), BF16) for w in widths],
        compiler_params=_cparams(("arbitrary",)),
        name="inproj",
    )(tile_mod, tile_rope, h_all, mod, norm_g, w1, lw["qng"], lw["kvng"], lw["wuq"], lw["wuk"],
      lw["wuv"], tables["ca"], tables["sa"], tables["cb"], tables["sb"])


def _gqa_kernel(sink_ref, q_ref, *refs, local, ntile):
    if local:
        kp_ref, kc_ref, kn_ref, vp_ref, vc_ref, vn_ref, kx_ref, vx_ref, o_ref = refs
    else:
        kx_ref, vx_ref, o_ref = refs
    q = q_ref[...]
    tq = q.shape[0]
    if local:
        j = pl.program_id(1)
        nloc = tq + 2 * WINDOW
        r = lax.broadcasted_iota(jnp.int32, (tq, nloc), 0)
        c = lax.broadcasted_iota(jnp.int32, (tq, nloc), 1)
        ok = jnp.abs(c - WINDOW - r) <= WINDOW
        ok = jnp.logical_and(ok, jnp.logical_or(c >= WINDOW, j > 0))
        ok = jnp.logical_and(ok, jnp.logical_or(c < WINDOW + tq, j < ntile - 1))
        bias = jnp.where(ok, 0.0, NEG_INF)
        kcat = jnp.concatenate([kp_ref[...], kc_ref[...], kn_ref[...], kx_ref[...]], axis=0)
        vcat = jnp.concatenate([vp_ref[...], vc_ref[...], vn_ref[...], vx_ref[...]], axis=0)
    else:
        kcat, vcat = kx_ref[...], vx_ref[...]
    for h in range(A_HEADS):
        g = h // A_GROUP
        qh = q[:, h * A_HEAD_DIM:(h + 1) * A_HEAD_DIM]
        kg = kcat[:, g * A_HEAD_DIM:(g + 1) * A_HEAD_DIM]
        vg = vcat[:, g * A_HEAD_DIM:(g + 1) * A_HEAD_DIM]
        s = _dot_nt(qh, kg)
        if local:
            s = jnp.concatenate([s[:, :nloc] + bias, s[:, nloc:]], axis=1)
        sk = sink_ref[h]
        mx = jnp.maximum(jnp.max(s, axis=-1, keepdims=True), sk)
        e = jnp.exp(s - mx)
        denom = jnp.sum(e, axis=-1, keepdims=True) + jnp.exp(sk - mx)
        o = _dot(e.astype(BF16), vg) / denom
        o_ref[:, h * A_HEAD_DIM:(h + 1) * A_HEAD_DIM] = o.astype(BF16)


def _window_attn(qa, ka, va, sink, batch, n, lc):
    t = batch * n
    tq = min(WIN_Q_TILE, n)
    ntile = n // tq
    sub = tq // WINDOW
    nblk = n // WINDOW
    cblk = t // lc
    qmap = lambda b, j: (b * ntile + j, 0)
    prev = lambda b, j: (b * nblk + jnp.maximum(j * sub - 1, 0), 0)
    nxt = lambda b, j: (b * nblk + jnp.minimum((j + 1) * sub, nblk - 1), 0)
    ctx = lambda b, j: (cblk + b, 0)
    kvw = A_KV_WIDTH
    return pl.pallas_call(
        functools.partial(_gqa_kernel, local=True, ntile=ntile),
        grid=(batch, ntile),
        in_specs=[pl.BlockSpec(memory_space=pltpu.SMEM),
                  pl.BlockSpec((tq, A_WIDTH), qmap),
                  pl.BlockSpec((WINDOW, kvw), prev), pl.BlockSpec((tq, kvw), qmap),
                  pl.BlockSpec((WINDOW, kvw), nxt),
                  pl.BlockSpec((WINDOW, kvw), prev), pl.BlockSpec((tq, kvw), qmap),
                  pl.BlockSpec((WINDOW, kvw), nxt),
                  pl.BlockSpec((lc, kvw), ctx), pl.BlockSpec((lc, kvw), ctx)],
        out_specs=pl.BlockSpec((tq, A_WIDTH), qmap),
        out_shape=jax.ShapeDtypeStruct((t, A_WIDTH), BF16),
        compiler_params=_cparams(("arbitrary", "arbitrary")),
        name="window_attn",
    )(sink, qa, ka, ka, ka, va, va, va, ka, va)


def _ctx_gqa_attn(qa, ka, va, sink, batch, n, lc):
    cblk = batch * n // lc
    cmap = lambda b: (cblk + b, 0)
    return pl.pallas_call(
        functools.partial(_gqa_kernel, local=False, ntile=0),
        grid=(batch,),
        in_specs=[pl.BlockSpec(memory_space=pltpu.SMEM),
                  pl.BlockSpec((lc, A_WIDTH), cmap),
                  pl.BlockSpec((lc, A_KV_WIDTH), cmap), pl.BlockSpec((lc, A_KV_WIDTH), cmap)],
        out_specs=pl.BlockSpec((lc, A_WIDTH), lambda b: (b, 0)),
        out_shape=jax.ShapeDtypeStruct((batch * lc, A_WIDTH), BF16),
        compiler_params=_cparams(("arbitrary",)),
        name="ctx_gqa_attn",
    )(sink, qa, ka, va)


def _mla_kernel(q_ref, kc_ref, vc_ref, *refs, with_latent):
    if with_latent:
        kx_ref, vx_ref, o_ref, s_ref = refs
    else:
        o_ref, s_ref = refs
    lc = kc_ref.shape[0]
    chunks = [(kc_ref, vc_ref, 0, lc, 0)]
    if with_latent:
        n = kx_ref.shape[0]
        for r0 in range(0, n, MLA_KEY_CHUNK):
            chunks.append((kx_ref, vx_ref, r0, min(MLA_KEY_CHUNK, n - r0), lc + r0))

    def lane_tiles(x):
        return [x[:, t * LANES:(t + 1) * LANES] for t in range(x.shape[1] // LANES)]

    for h in range(B_HEADS):
        buf = h % 2
        ql = slice(h * B_HEAD_PAD, (h + 1) * B_HEAD_PAD)
        vl = slice(h * B_VDIM, (h + 1) * B_VDIM)
        qh = q_ref[:, ql]
        m_lanes = None
        for k_ref, _, r0, nr, c0 in chunks:
            s = _dot_nt(qh, k_ref[r0:r0 + nr, ql])
            s_ref[buf, :, c0:c0 + nr] = s
            for part in lane_tiles(s):
                m_lanes = part if m_lanes is None else jnp.maximum(m_lanes, part)
        mx = jnp.max(m_lanes, axis=-1, keepdims=True)
        l_lanes = None
        acc = None
        for _, v_ref, r0, nr, c0 in chunks:
            e = jnp.exp2(s_ref[buf, :, c0:c0 + nr] - mx)
            for part in lane_tiles(e):
                l_lanes = part if l_lanes is None else l_lanes + part
            pv = _dot(e.astype(BF16), v_ref[r0:r0 + nr, vl])
            acc = pv if acc is None else acc + pv
        denom = jnp.sum(l_lanes, axis=-1, keepdims=True)
        o_ref[:, vl] = (acc / denom).astype(BF16)


def _mla_attn(qb, kb, vb, batch, n, lc):
    t = batch * n
    tq = min(MLA_Q_TILE, n)
    nq = n // tq
    cblk = t // lc
    qmap = lambda b, j: (b * nq + j, 0)
    ctx = lambda b, j: (cblk + b, 0)
    lat = lambda b, j: (b, 0)
    return pl.pallas_call(
        functools.partial(_mla_kernel, with_latent=True),
        grid=(batch, nq),
        in_specs=[pl.BlockSpec((tq, B_QK_WIDTH), qmap),
                  pl.BlockSpec((lc, B_QK_WIDTH), ctx), pl.BlockSpec((lc, B_WIDTH), ctx),
                  pl.BlockSpec((n, B_QK_WIDTH), lat), pl.BlockSpec((n, B_WIDTH), lat)],
        out_specs=pl.BlockSpec((tq, B_WIDTH), qmap),
        out_shape=jax.ShapeDtypeStruct((t, B_WIDTH), BF16),
        scratch_shapes=[pltpu.VMEM((2, tq, lc + n), F32)],
        compiler_params=_cparams(("arbitrary", "arbitrary")),
        name="mla_attn",
    )(qb, kb, vb, kb, vb)


def _mla_ctx_attn(qb, kb, vb, batch, n, lc):
    cblk = batch * n // lc
    cmap = lambda b: (cblk + b, 0)
    return pl.pallas_call(
        functools.partial(_mla_kernel, with_latent=False),
        grid=(batch,),
        in_specs=[pl.BlockSpec((lc, B_QK_WIDTH), cmap),
                  pl.BlockSpec((lc, B_QK_WIDTH), cmap), pl.BlockSpec((lc, B_WIDTH), cmap)],
        out_specs=pl.BlockSpec((lc, B_WIDTH), lambda b: (b, 0)),
        out_shape=jax.ShapeDtypeStruct((batch * lc, B_WIDTH), BF16),
        scratch_shapes=[pltpu.VMEM((2, lc, lc), F32)],
        compiler_params=_cparams(("arbitrary",)),
        name="mla_ctx_attn",
    )(qb, kb, vb)


def _route_tile(logits, run):
    tm = logits.shape[0]
    lane_i = lax.broadcasted_iota(jnp.int32, logits.shape, 1)
    lane = lane_i.astype(F32)
    nolane = float(LANES)

    def first_argmax(v):
        mx = jnp.max(v, axis=-1, keepdims=True)
        return mx, jnp.min(jnp.where(v == mx, lane, nolane), axis=-1, keepdims=True)

    gl = jnp.where(lane_i < N_GROUPS, logits, NEG_INF)
    gmax, g_sel = first_argmax(gl)
    g_p = 1.0 / jnp.sum(jnp.exp(gl - gmax), axis=-1, keepdims=True)
    lo = N_GROUPS + EXPERTS_PER_GROUP * g_sel
    el = jnp.where(jnp.logical_and(lane >= lo, lane < lo + EXPERTS_PER_GROUP), logits, NEG_INF)
    v1, i1 = first_argmax(el)
    sel1 = lane == i1
    v2, i2 = first_argmax(jnp.where(sel1, NEG_INF, el))
    sel2 = lane == i2
    t2 = jnp.exp(v2 - v1)
    gate1 = g_p / (1.0 + t2)
    gate2 = gate1 * t2

    onehot = jnp.where(jnp.logical_or(sel1, sel2), 1.0, 0.0)
    r_i = lax.broadcasted_iota(jnp.int32, (tm, tm), 0)
    c_i = lax.broadcasted_iota(jnp.int32, (tm, tm), 1)
    lower = jnp.where(c_i < r_i, 1.0, 0.0).astype(BF16)
    before = _dot(lower, onehot.astype(BF16)) + run
    rank1 = jnp.sum(jnp.where(sel1, before, 0.0), axis=-1, keepdims=True)
    rank2 = jnp.sum(jnp.where(sel2, before, 0.0), axis=-1, keepdims=True)
    info = jnp.zeros_like(logits)
    for k, val in enumerate((i1 - N_GROUPS, i2 - N_GROUPS, rank1, rank2, gate1, gate2)):
        info = jnp.where(lane_i == k, val, info)
    return info, run + jnp.sum(onehot, axis=0, keepdims=True)


def _merge_kernel(tmod_ref, tpos_ref, tlen_ref, h_ref, mod_ref, g_ref, oa_ref, ob_ref, u_ref, up_ref,
                  un_ref, gs_ref, wpool_ref, pscale_ref, wa_ref, wb_ref, wc_ref, wo_ref, wr_ref, br_ref,
                  hn_o, fx_o, info_o, cnt_o, run_ref):
    del tmod_ref
    i = pl.program_id(0)

    @pl.when(i == 0)
    def _():
        run_ref[...] = jnp.zeros_like(run_ref)

    pos0 = tpos_ref[i]
    seq_len = tlen_ref[i]
    tm = u_ref.shape[0]
    d = h_ref.shape[1]

    u = u_ref[...]
    kdim = tm + LANES
    zpad = jnp.zeros((LANES - 2 * POOL_HALO, u.shape[1]), BF16)
    ucat = jnp.concatenate([up_ref[...], u, un_ref[...], zpad], axis=0)
    uf = u.astype(F32)
    r_i = lax.broadcasted_iota(jnp.int32, (tm, kdim), 0)
    c_i = lax.broadcasted_iota(jnp.int32, (tm, kdim), 1)
    rel = c_i - POOL_HALO - r_i
    kpos = pos0 + c_i - POOL_HALO
    valid = jnp.logical_and(kpos >= 0, kpos < seq_len)
    tpos = pos0 + lax.broadcasted_iota(jnp.int32, (tm, 1), 0)
    oc_parts = []
    for gi, w in enumerate(POOL_WINDOWS):
        rad = w // 2
        sl = slice(gi * C_GROUP_DIM, (gi + 1) * C_GROUP_DIM)
        band = jnp.where(jnp.logical_and(jnp.abs(rel) <= rad, valid), 1.0, 0.0).astype(BF16)
        win_sum = _dot(band, ucat[:, sl])
        cnt = (jnp.minimum(tpos + rad + 1, seq_len) - jnp.maximum(tpos - rad, 0)).astype(F32)
        pooled = (win_sum / cnt - uf[:, sl]).astype(BF16)
        oc_parts.append(_dot(pooled, wpool_ref[gi]))
    oc = (jnp.concatenate(oc_parts, axis=1) * pscale_ref[...]).astype(BF16)

    y = gs_ref[:, 0:d].astype(F32) * _dot(oa_ref[...], wa_ref[...])
    y = y + gs_ref[:, d:2 * d].astype(F32) * _dot(ob_ref[...], wb_ref[...])
    y = y + gs_ref[:, 2 * d:3 * d].astype(F32) * _dot(oc, wc_ref[...])
    mix = _dot(y.astype(BF16), wo_ref[...])

    m = mod_ref[0]
    hn = h_ref[...] + m[2:3] * mix
    hn_o[...] = hn
    fx = _rms(hn, g_ref[...]) * (1.0 + m[4:5]) + m[3:4]
    fx_hi = fx.astype(BF16)
    fx_lo = (fx - fx_hi.astype(F32)).astype(BF16)
    fx_o[...] = fx_hi
    logits = (_dot(fx_hi, wr_ref[0]) + _dot(fx_lo, wr_ref[0]) + _dot(fx_hi, wr_ref[1])) + br_ref[...]
    info, run = _route_tile(logits, run_ref[...])
    info_o[...] = info
    run_ref[...] = run
    cnt_o[...] = run


def _merge(h_all, mod, norm_g, oa, ob, u, gs, lw, rows, tile_mod, tile_pos, tile_len):
    t_all, d = h_all.shape
    tm = ROW_TILE
    nt = rows // tm
    hb = tm // POOL_HALO
    nhalo = u.shape[0] // POOL_HALO
    row = lambda i, *_: (i, 0)
    const = lambda i, *_: (0, 0)
    const3 = lambda i, *_: (0, 0, 0)
    grid_spec = pltpu.PrefetchScalarGridSpec(
        num_scalar_prefetch=3,
        grid=(nt,),
        in_specs=[pl.BlockSpec((tm, d), row),
                  pl.BlockSpec((1, 8, d), lambda i, tmod, *_: (tmod[i], 0, 0)),
                  pl.BlockSpec((1, d), const),
                  pl.BlockSpec((tm, A_WIDTH), row),
                  pl.BlockSpec((tm, B_WIDTH), row),
                  pl.BlockSpec((tm, C_WIDTH), row),
                  pl.BlockSpec((POOL_HALO, C_WIDTH), lambda i, *_: (jnp.maximum(i * hb - 1, 0), 0)),
                  pl.BlockSpec((POOL_HALO, C_WIDTH), lambda i, *_: (jnp.minimum((i + 1) * hb, nhalo - 1), 0)),
                  pl.BlockSpec((tm, N_BRANCH * d), row),
                  pl.BlockSpec(lw["wpool"].shape, const3),
                  pl.BlockSpec((1, C_WIDTH), const),
                  pl.BlockSpec(lw["wa"].shape, const),
                  pl.BlockSpec(lw["wb"].shape, const),
                  pl.BlockSpec(lw["wc"].shape, const),
                  pl.BlockSpec(lw["wo"].shape, const),
                  pl.BlockSpec(lw["wr"].shape, const3),
                  pl.BlockSpec((1, ROUTER_LANES), const)],
        out_specs=[pl.BlockSpec((tm, d), row), pl.BlockSpec((tm, d), row),
                   pl.BlockSpec((tm, ROUTER_LANES), row), pl.BlockSpec((1, ROUTER_LANES), const)],
        scratch_shapes=[pltpu.VMEM((1, ROUTER_LANES), F32)],
    )
    return pl.pallas_call(
        _merge_kernel,
        grid_spec=grid_spec,
        out_shape=[jax.ShapeDtypeStruct((rows, d), F32), jax.ShapeDtypeStruct((rows, d), BF16),
                   jax.ShapeDtypeStruct((rows, ROUTER_LANES), F32),
                   jax.ShapeDtypeStruct((1, ROUTER_LANES), F32)],
        compiler_params=_cparams(("arbitrary",)),
        name="merge",
    )(tile_mod, tile_pos, tile_len, h_all, mod, norm_g, oa, ob, u, u, u, gs, lw["wpool"],
      lw["pscale"], lw["wa"], lw["wb"], lw["wc"], lw["wo"], lw["wr"], lw["br"])


def _moe_kernel(blk_e_ref, nused_ref, x_ref, wgu_ref, wdn_ref, o_ref):
    del blk_e_ref
    i = pl.program_id(0)

    @pl.when(i < nused_ref[0])
    def _():
        gu = _dot(x_ref[...], wgu_ref[0, 0].astype(BF16))
        gt, up = gu[:, :EXPERT_FF], gu[:, EXPERT_FF:]
        act = (gt * jax.nn.sigmoid(gt) * up).astype(BF16)
        o_ref[...] = _dot(act, wdn_ref[0, 0].astype(BF16)).astype(o_ref.dtype)

    @pl.when(i >= nused_ref[0])
    def _():
        o_ref[...] = jnp.zeros_like(o_ref)


def _moe(xg, blk_e, n_used, w_gu, w_dn, layer):
    p, d = xg.shape
    nb = p // MOE_BLOCK
    grid_spec = pltpu.PrefetchScalarGridSpec(
        num_scalar_prefetch=2,
        grid=(nb,),
        in_specs=[pl.BlockSpec((MOE_BLOCK, d), lambda i, *_: (i, 0)),
                  pl.BlockSpec((1, 1, d, 2 * EXPERT_FF), lambda i, be, nu: (layer, be[i], 0, 0)),
                  pl.BlockSpec((1, 1, EXPERT_FF, d), lambda i, be, nu: (layer, be[i], 0, 0))],
        out_specs=pl.BlockSpec((MOE_BLOCK, d), lambda i, *_: (i, 0)),
    )
    return pl.pallas_call(
        _moe_kernel,
        grid_spec=grid_spec,
        out_shape=jax.ShapeDtypeStruct((p, d), BF16),
        compiler_params=_cparams(("arbitrary",)),
        name="moe_experts",
    )(blk_e, n_used, xg, w_gu, w_dn)


def _resid_kernel(tmod_ref, h_ref, mod_ref, y_ref, info_ref, fg_ref, o_ref, *, final):
    del tmod_ref
    d = h_ref.shape[1]
    m = mod_ref[0]
    info = info_ref[...]
    ffn = (y_ref[:, 0:d].astype(F32) * info[:, 4:5] + y_ref[:, d:2 * d].astype(F32) * info[:, 5:6])
    hn = h_ref[...] + m[5:6] * ffn
    if final:
        hn = _rms(hn, fg_ref[...])
    o_ref[...] = hn


def _resid(h, mod, y2, info, final_g, tile_mod, final):
    rows, d = h.shape
    tm = ROW_TILE
    row = lambda i, *_: (i, 0)
    grid_spec = pltpu.PrefetchScalarGridSpec(
        num_scalar_prefetch=1,
        grid=(rows // tm,),
        in_specs=[pl.BlockSpec((tm, d), row),
                  pl.BlockSpec((1, 8, d), lambda i, tmod: (tmod[i], 0, 0)),
                  pl.BlockSpec((tm, TOP_K * d), row),
                  pl.BlockSpec((tm, ROUTER_LANES), row),
                  pl.BlockSpec((1, d), lambda i, *_: (0, 0))],
        out_specs=pl.BlockSpec((tm, d), row),
    )
    return pl.pallas_call(
        functools.partial(_resid_kernel, final=final),
        grid_spec=grid_spec,
        out_shape=jax.ShapeDtypeStruct((rows, d), F32),
        compiler_params=_cparams(("arbitrary",)),
        name="ffn_residual",
    )(tile_mod, h, mod, y2, info, final_g)


def _deinterleave(n):
    return np.concatenate([np.arange(0, n, 2), np.arange(1, n, 2)])


def _rope_tables(n, tm):
    rows = n // GRID_W
    row = np.repeat(np.arange(rows), GRID_W).astype(np.float32)
    col = np.tile(np.arange(GRID_W), rows).astype(np.float32)

    def cs(rot_dim):
        axis_dim = rot_dim // 2
        inv = jnp.asarray(ROPE_BASE, F32) ** (-jnp.arange(0, axis_dim, 2, dtype=F32) / axis_dim)
        ang = jnp.concatenate([jnp.asarray(row)[:, None] * inv, jnp.asarray(col)[:, None] * inv], axis=-1)
        return jnp.cos(ang), jnp.sin(ang)

    cos_a, sin_a = cs(A_HEAD_DIM)
    ca = jnp.tile(jnp.concatenate([cos_a, cos_a], axis=1), (1, LANES // A_HEAD_DIM))
    sa = jnp.tile(jnp.concatenate([-sin_a, sin_a], axis=1), (1, LANES // A_HEAD_DIM))
    cos_b, sin_b = cs(B_ROPE)
    ones = jnp.ones((n, B_NOPE), F32)
    tail = LANES - B_NOPE - B_ROPE
    cb = jnp.concatenate([ones, cos_b, cos_b, jnp.ones((n, tail), F32)], axis=1)
    sb = jnp.concatenate([0 * ones, -sin_b, sin_b, jnp.zeros((n, tail), F32)], axis=1)
    ident_c = jnp.ones((tm, LANES), F32)
    ident_s = jnp.zeros((tm, LANES), F32)
    return {"ca": jnp.concatenate([ca, ident_c]), "sa": jnp.concatenate([sa, ident_s]),
            "cb": jnp.concatenate([cb, ident_c]), "sb": jnp.concatenate([sb, ident_s])}


def _layer_weights(l, w_in, q_norm_g, w_uq, kv_norm_g, w_ukv, w_pool, pool_scale, w_br_a, w_br_b,
                   w_br_c, w_out, w_rg, b_rg, w_re, b_re):
    d = w_in.shape[1]
    wi = w_in[l]
    splits = np.cumsum([A_WIDTH, A_KV_WIDTH, A_KV_WIDTH, Q_LORA, KV_LORA, B_ROPE, C_WIDTH])
    qa, ka, va, cq, ckv, kr, u, gates = jnp.split(wi, splits, axis=1)
    pa = _deinterleave(A_HEAD_DIM)
    qa = qa.reshape(d, A_HEADS, A_HEAD_DIM)[:, :, pa].reshape(d, A_WIDTH) * (A_HEAD_DIM ** -0.5)
    ka = ka.reshape(d, A_KV_HEADS, A_HEAD_DIM)[:, :, pa].reshape(d, A_KV_WIDTH)
    pb = _deinterleave(B_ROPE)
    tail = LANES - B_NOPE - B_ROPE
    kr128 = jnp.concatenate([jnp.zeros((d, B_NOPE), F32), kr[:, pb], jnp.zeros((d, tail), F32)], axis=1)
    w1 = jnp.concatenate([qa, ka, va, cq, ckv, kr128, u, gates], axis=1).astype(BF16)

    uq = w_uq[l].reshape(Q_LORA, B_HEADS, B_NOPE + B_ROPE)
    uq = jnp.concatenate([uq[:, :, :B_NOPE], uq[:, :, B_NOPE:][:, :, pb],
                          jnp.zeros((Q_LORA, B_HEADS, tail), F32)], axis=2)
    ukv = w_ukv[l].reshape(KV_LORA, B_HEADS, B_NOPE + B_VDIM)
    uk = jnp.concatenate([ukv[:, :, :B_NOPE], jnp.zeros((KV_LORA, B_HEADS, LANES - B_NOPE), F32)], axis=2)
    uv = ukv[:, :, B_NOPE:]

    wr = jnp.concatenate([w_rg[l], w_re[l], jnp.zeros((d, ROUTER_LANES - N_GROUPS - N_EXPERTS), F32)], axis=1)
    wr_hi = wr.astype(BF16)
    wr_lo = (wr - wr_hi.astype(F32)).astype(BF16)
    br = jnp.concatenate([b_rg[l], b_re[l], jnp.zeros((ROUTER_LANES - N_GROUPS - N_EXPERTS,), F32)])
    return {
        "w1": w1,
        "qng": q_norm_g[l].reshape(1, Q_LORA), "kvng": kv_norm_g[l].reshape(1, KV_LORA),
        "wuq": uq.reshape(Q_LORA, B_QK_WIDTH).astype(BF16),
        "wuk": uk.reshape(KV_LORA, B_QK_WIDTH).astype(BF16),
        "wuv": uv.reshape(KV_LORA, B_WIDTH).astype(BF16),
        "wpool": w_pool[l].astype(BF16), "pscale": pool_scale[l].reshape(1, C_WIDTH),
        "wa": w_br_a[l].astype(BF16), "wb": w_br_b[l].astype(BF16), "wc": w_br_c[l].astype(BF16),
        "wo": w_out[l].astype(BF16),
        "wr": jnp.stack([wr_hi, wr_lo]), "br": br.reshape(1, ROUTER_LANES),
    }


def _block_layout(info, cnt):
    t = info.shape[0]
    a = t * TOP_K
    expert = info[:, 0:TOP_K].astype(jnp.int32)
    rank = info[:, TOP_K:2 * TOP_K].astype(jnp.int32)
    counts = cnt[0, N_GROUPS:N_GROUPS + N_EXPERTS].astype(jnp.int32)
    padded = (counts + MOE_BLOCK - 1) // MOE_BLOCK * MOE_BLOCK
    pad_end = jnp.cumsum(padded)
    pad_start = pad_end - padded
    eids = jnp.arange(N_EXPERTS, dtype=jnp.int32)
    base = jnp.sum(jnp.where(expert[:, :, None] == eids, pad_start, 0), axis=-1)
    slot = base + rank
    n_blocks = -(-(a + N_EXPERTS * (MOE_BLOCK - 1)) // MOE_BLOCK)
    p = n_blocks * MOE_BLOCK
    tok_of_slot = jnp.zeros((p,), jnp.int32).at[slot.reshape(a)].set(
        jnp.arange(a, dtype=jnp.int32) // TOP_K, unique_indices=True)
    blk_start = jnp.arange(n_blocks, dtype=jnp.int32) * MOE_BLOCK
    blk_e = jnp.minimum(jnp.sum((pad_end[None, :] <= blk_start[:, None]).astype(jnp.int32), axis=1),
                        N_EXPERTS - 1)
    n_used = (pad_end[-1] // MOE_BLOCK).reshape(1)
    return slot, tok_of_slot, blk_e, n_used


def kernel(x, c, ctx, c_ctx, w_mod, b_mod, norm_mix_g, norm_ffn_g, w_in, sink, q_norm_g, w_uq, kv_norm_g,
           w_ukv, w_pool, pool_scale, w_br_a, w_br_b, w_br_c, w_out, w_rg, b_rg, w_re, b_re, w_gu, w_dn,
           final_g):
    batch, n, d = x.shape
    lc = ctx.shape[1]
    depth = w_mod.shape[0]
    tm = ROW_TILE
    assert n % tm == 0 and lc % tm == 0 and n % WINDOW == 0 and (batch * n) % lc == 0
    t = batch * n
    tc = batch * lc
    nt_lat, nt_ctx = t // tm, tc // tm
    per_b, per_c = n // tm, lc // tm

    lat_i = np.arange(nt_lat)
    ctx_i = np.arange(nt_ctx)
    tile_mod = jnp.asarray(np.concatenate([lat_i // per_b, np.full(nt_ctx, batch)]), jnp.int32)
    tile_rope = jnp.asarray(np.concatenate([lat_i % per_b, np.full(nt_ctx, per_b)]), jnp.int32)
    tile_pos = jnp.asarray(np.concatenate([(lat_i % per_b) * tm, (ctx_i % per_c) * tm]), jnp.int32)
    tile_len = jnp.asarray(np.concatenate([np.full(nt_lat, n), np.full(nt_ctx, lc)]), jnp.int32)
    tables = _rope_tables(n, tm)

    mod_rows = 16
    cvec = jnp.concatenate([c, c_ctx[None, :], jnp.zeros((mod_rows - batch - 1, d), F32)], axis=0)
    h_all = jnp.concatenate([x.reshape(t, d), ctx.reshape(tc, d)], axis=0)

    mod_all = _modvec(cvec, w_mod, b_mod).reshape(depth, mod_rows, 6, d)
    mod_all = jnp.concatenate([mod_all, jnp.zeros((depth, mod_rows, 2, d), F32)], axis=2)

    out = None
    for l in range(depth):
        last = l == depth - 1
        lw = _layer_weights(l, w_in, q_norm_g, w_uq, kv_norm_g, w_ukv, w_pool, pool_scale, w_br_a,
                            w_br_b, w_br_c, w_out, w_rg, b_rg, w_re, b_re)
        mod = mod_all[l]

        qa, ka, va, qb, kb, vb, u, gs = _inproj(h_all, mod, norm_mix_g[l].reshape(1, d), lw, tables,
                                                tile_mod, tile_rope)
        oa = _window_attn(qa, ka, va, sink[l], batch, n, lc)
        ob = _mla_attn(qb, kb, vb, batch, n, lc)
        if last:
            rows = t
        else:
            rows = t + tc
            oa = jnp.concatenate([oa, _ctx_gqa_attn(qa, ka, va, sink[l], batch, n, lc)], axis=0)
            ob = jnp.concatenate([ob, _mla_ctx_attn(qb, kb, vb, batch, n, lc)], axis=0)
        hn, fx, info, cnt = _merge(h_all, mod, norm_ffn_g[l].reshape(1, d), oa, ob, u, gs, lw, rows,
                                   tile_mod, tile_pos, tile_len)

        slot, tok_of_slot, blk_e, n_used = _block_layout(info, cnt)
        xg = jnp.take(fx, tok_of_slot, axis=0)
        yb = _moe(xg, blk_e, n_used, w_gu, w_dn, l)
        y2 = jnp.take(yb, slot, axis=0).reshape(rows, TOP_K * d)
        res = _resid(hn, mod, y2, info, final_g.reshape(1, d), tile_mod, last)
        if last:
            out = res
        else:
            h_all = res
    return out.reshape(batch, n, d)
```

```python
import functools
import math

import jax
import jax.numpy as jnp
import numpy as np
from jax import lax
from jax.experimental import pallas as pl
from jax.experimental.pallas import tpu as pltpu

GRID_W = 64
ROPE_BASE = 10000.0
EPS = 1e-6
NEG_INF = -1e30

A_HEADS = 8
A_KV_HEADS = 2
A_GROUP = A_HEADS // A_KV_HEADS
A_HEAD_DIM = 64
A_WIDTH = A_HEADS * A_HEAD_DIM
A_KV_WIDTH = A_KV_HEADS * A_HEAD_DIM
WINDOW = 128

B_HEADS = 8
B_NOPE = 64
B_ROPE = 32
B_VDIM = 64
B_WIDTH = B_HEADS * B_VDIM
Q_LORA = 256
KV_LORA = 256

POOL_WINDOWS = (2, 4, 8, 16)
C_GROUPS = 4
C_GROUP_DIM = 128
C_WIDTH = C_GROUPS * C_GROUP_DIM

N_BRANCH = 3
N_GROUPS = 4
EXPERTS_PER_GROUP = 8
N_EXPERTS = N_GROUPS * EXPERTS_PER_GROUP
TOP_K = 2
EXPERT_FF = 256

LANES = 128
BF16_SUBLANES = 16
VMEM_LIMIT_BYTES = 56 * 1024 * 1024

B_HEAD_PAD = LANES
B_QK_WIDTH = B_HEADS * B_HEAD_PAD
POOL_HALO = BF16_SUBLANES
ROW_TILE = 256
MLA_Q_TILE = 256
MLA_KEY_CHUNK = 512
WIN_Q_TILE = 512
MOE_BLOCK = 256
ROUTER_LANES = LANES

_SEG = {}
_off = 0
for _name, _w in (("qa", A_WIDTH), ("ka", A_KV_WIDTH), ("va", A_KV_WIDTH), ("cq", Q_LORA),
                  ("ckv", KV_LORA), ("kr", LANES), ("u", C_WIDTH), ("gates", None)):
    _SEG[_name] = _off
    if _w is not None:
        _off += _w
W1_GATES_OFF = _SEG["gates"]

F32 = jnp.float32
BF16 = jnp.bfloat16


def _dot(a, b):
    return jnp.dot(a, b, preferred_element_type=F32)


def _dot_nt(a, b):
    return lax.dot_general(a, b, (((1,), (1,)), ((), ())), preferred_element_type=F32)


def _cparams(sem):
    return pltpu.CompilerParams(dimension_semantics=sem, vmem_limit_bytes=VMEM_LIMIT_BYTES)


def _rms(x, g):
    return x * lax.rsqrt(jnp.mean(x * x, axis=-1, keepdims=True) + EPS) * g


def _modvec_kernel(c_ref, w_ref, b_ref, o_ref):
    c = c_ref[...]
    a = (c * jax.nn.sigmoid(c)).astype(BF16)
    o_ref[0] = _dot(a, w_ref[0].astype(BF16)) + b_ref[0]


def _modvec(cvec, w_mod, b_mod):
    rows, d = cvec.shape
    depth, _, n = w_mod.shape
    tn = d
    return pl.pallas_call(
        _modvec_kernel,
        grid=(depth, n // tn),
        in_specs=[pl.BlockSpec((rows, d), lambda l, j: (0, 0)),
                  pl.BlockSpec((1, d, tn), lambda l, j: (l, 0, j)),
                  pl.BlockSpec((1, 1, tn), lambda l, j: (l, 0, j))],
        out_specs=pl.BlockSpec((1, rows, tn), lambda l, j: (l, 0, j)),
        out_shape=jax.ShapeDtypeStruct((depth, rows, n), F32),
        compiler_params=_cparams(("arbitrary", "arbitrary")),
        name="modvec",
    )(cvec, w_mod, b_mod.reshape(depth, 1, n))


def _rope_tile(x, cos, sin, half):
    lane = lax.broadcasted_iota(jnp.int32, x.shape, 1)
    fwd = pltpu.roll(x, LANES - half, 1)
    bwd = pltpu.roll(x, half, 1)
    partner = jnp.where((lane % (2 * half)) < half, fwd, bwd)
    return x * cos + partner * sin


def _inproj_kernel(tmod_ref, trope_ref, h_ref, mod_ref, g_ref, w1_ref, qng_ref, kvng_ref, wuq_ref,
                   wuk_ref, wuv_ref, ca_ref, sa_ref, cb_ref, sb_ref,
                   qa_o, ka_o, va_o, qb_o, kb_o, vb_o, u_o, gs_o):
    del tmod_ref, trope_ref
    x = h_ref[...]
    m = mod_ref[0]
    hx = _rms(x, g_ref[...]) * (1.0 + m[1:2]) + m[0:1]
    hb = hx.astype(BF16)

    def seg(name, width):
        off = _SEG[name]
        return _dot(hb, w1_ref[:, off:off + width])

    ca, sa = ca_ref[...], sa_ref[...]
    cb, sb = cb_ref[...], sb_ref[...]

    qa = seg("qa", A_WIDTH)
    for t in range(A_WIDTH // LANES):
        sl = slice(t * LANES, (t + 1) * LANES)
        qa_o[:, sl] = _rope_tile(qa[:, sl], ca, sa, A_HEAD_DIM // 2).astype(BF16)
    ka_o[...] = _rope_tile(seg("ka", A_KV_WIDTH), ca, sa, A_HEAD_DIM // 2).astype(BF16)
    va_o[...] = seg("va", A_KV_WIDTH).astype(BF16)

    cqn = _rms(seg("cq", Q_LORA), qng_ref[...]).astype(BF16)
    ckvn = _rms(seg("ckv", KV_LORA), kvng_ref[...]).astype(BF16)
    qb = _dot(cqn, wuq_ref[...])
    kb = _dot(ckvn, wuk_ref[...])
    kr = _rope_tile(seg("kr", LANES), cb, sb, B_ROPE // 2)
    qscale = (B_NOPE + B_ROPE) ** -0.5 * math.log2(math.e)
    for t in range(B_HEADS):
        sl = slice(t * LANES, (t + 1) * LANES)
        qb_o[:, sl] = (_rope_tile(qb[:, sl], cb, sb, B_ROPE // 2) * qscale).astype(BF16)
        kb_o[:, sl] = (kb[:, sl] + kr).astype(BF16)
    vb_o[...] = _dot(ckvn, wuv_ref[...]).astype(BF16)

    u_o[...] = seg("u", C_WIDTH).astype(BF16)
    ngate = gs_o.shape[1]
    for t in range(ngate // 512):
        off = W1_GATES_OFF + t * 512
        gs_o[:, t * 512:(t + 1) * 512] = jax.nn.sigmoid(_dot(hb, w1_ref[:, off:off + 512])).astype(BF16)


def _inproj(h_all, mod, norm_g, lw, tables, tile_mod, tile_rope):
    t_all, d = h_all.shape
    tm = ROW_TILE
    nt = t_all // tm
    w1 = lw["w1"]
    ngate = N_BRANCH * d
    row = lambda i, *_: (i, 0)
    const = lambda i, *_: (0, 0)
    rope = lambda i, tmod, trope: (trope[i], 0)
    widths = (A_WIDTH, A_KV_WIDTH, A_KV_WIDTH, B_QK_WIDTH, B_QK_WIDTH, B_WIDTH, C_WIDTH, ngate)
    grid_spec = pltpu.PrefetchScalarGridSpec(
        num_scalar_prefetch=2,
        grid=(nt,),
        in_specs=[pl.BlockSpec((tm, d), row),
                  pl.BlockSpec((1, 8, d), lambda i, tmod, trope: (tmod[i], 0, 0)),
                  pl.BlockSpec((1, d), const),
                  pl.BlockSpec(w1.shape, const),
                  pl.BlockSpec((1, Q_LORA), const),
                  pl.BlockSpec((1, KV_LORA), const),
                  pl.BlockSpec(lw["wuq"].shape, const),
                  pl.BlockSpec(lw["wuk"].shape, const),
                  pl.BlockSpec(lw["wuv"].shape, const),
                  pl.BlockSpec((tm, LANES), rope),
                  pl.BlockSpec((tm, LANES), rope),
                  pl.BlockSpec((tm, LANES), rope),
                  pl.BlockSpec((tm, LANES), rope)],
        out_specs=[pl.BlockSpec((tm, w), row) for w in widths],
    )
    return pl.pallas_call(
        _inproj_kernel,
        grid_spec=grid_spec,
        out_shape=[jax.ShapeDtypeStruct((t_all, w), BF16) for w in widths],
        compiler_params=_cparams(("arbitrary",)),
        name="inproj",
    )(tile_mod, tile_rope, h_all, mod, norm_g, w1, lw["qng"], lw["kvng"], lw["wuq"], lw["wuk"],
      lw["wuv"], tables["ca"], tables["sa"], tables["cb"], tables["sb"])


def _gqa_kernel(sink_ref, q_ref, *refs, local, ntile):
    if local:
        kp_ref, kc_ref, kn_ref, vp_ref, vc_ref, vn_ref, kx_ref, vx_ref, o_ref = refs
    else:
        kx_ref, vx_ref, _, o_ref = refs
    q = q_ref[...]
    tq = q.shape[0]
    if local:
        j = pl.program_id(1)
        nloc = tq + 2 * WINDOW
        r = lax.broadcasted_iota(jnp.int32, (tq, nloc), 0)
        c = lax.broadcasted_iota(jnp.int32, (tq, nloc), 1)
        ok = jnp.abs(c - WINDOW - r) <= WINDOW
        ok = jnp.logical_and(ok, jnp.logical_or(c >= WINDOW, j > 0))
        ok = jnp.logical_and(ok, jnp.logical_or(c < WINDOW + tq, j < ntile - 1))
        bias = jnp.where(ok, 0.0, NEG_INF)
        kcat = jnp.concatenate([kp_ref[...], kc_ref[...], kn_ref[...], kx_ref[...]], axis=0)
        vcat = jnp.concatenate([vp_ref[...], vc_ref[...], vn_ref[...], vx_ref[...]], axis=0)
    else:
        kcat, vcat = kx_ref[...], vx_ref[...]
    for h in range(A_HEADS):
        g = h // A_GROUP
        qh = q[:, h * A_HEAD_DIM:(h + 1) * A_HEAD_DIM]
        kg = kcat[:, g * A_HEAD_DIM:(g + 1) * A_HEAD_DIM]
        vg = vcat[:, g * A_HEAD_DIM:(g + 1) * A_HEAD_DIM]
        s = _dot_nt(qh, kg)
        if local:
            s = jnp.concatenate([s[:, :nloc] + bias, s[:, nloc:]], axis=1)
        sk = sink_ref[h]
        mx = jnp.maximum(jnp.max(s, axis=-1, keepdims=True), sk)
        e = jnp.exp(s - mx)
        denom = jnp.sum(e, axis=-1, keepdims=True) + jnp.exp(sk - mx)
        o = _dot(e.astype(BF16), vg) / denom
        o_ref[:, h * A_HEAD_DIM:(h + 1) * A_HEAD_DIM] = o.astype(BF16)


def _window_attn(qa, ka, va, sink, batch, n, lc, out_rows):
    t = batch * n
    tq = min(WIN_Q_TILE, n)
    ntile = n // tq
    sub = tq // WINDOW
    nblk = n // WINDOW
    cblk = t // lc
    qmap = lambda b, j: (b * ntile + j, 0)
    prev = lambda b, j: (b * nblk + jnp.maximum(j * sub - 1, 0), 0)
    nxt = lambda b, j: (b * nblk + jnp.minimum((j + 1) * sub, nblk - 1), 0)
    ctx = lambda b, j: (cblk + b, 0)
    kvw = A_KV_WIDTH
    return pl.pallas_call(
        functools.partial(_gqa_kernel, local=True, ntile=ntile),
        grid=(batch, ntile),
        in_specs=[pl.BlockSpec(memory_space=pltpu.SMEM),
                  pl.BlockSpec((tq, A_WIDTH), qmap),
                  pl.BlockSpec((WINDOW, kvw), prev), pl.BlockSpec((tq, kvw), qmap),
                  pl.BlockSpec((WINDOW, kvw), nxt),
                  pl.BlockSpec((WINDOW, kvw), prev), pl.BlockSpec((tq, kvw), qmap),
                  pl.BlockSpec((WINDOW, kvw), nxt),
                  pl.BlockSpec((lc, kvw), ctx), pl.BlockSpec((lc, kvw), ctx)],
        out_specs=pl.BlockSpec((tq, A_WIDTH), qmap),
        out_shape=jax.ShapeDtypeStruct((out_rows, A_WIDTH), BF16),
        compiler_params=_cparams(("arbitrary", "arbitrary")),
        name="window_attn",
    )(sink, qa, ka, ka, ka, va, va, va, ka, va)


def _ctx_gqa_attn(qa, ka, va, sink, oa, batch, n, lc):
    cblk = batch * n // lc
    cmap = lambda b: (cblk + b, 0)
    return pl.pallas_call(
        functools.partial(_gqa_kernel, local=False, ntile=0),
        grid=(batch,),
        in_specs=[pl.BlockSpec(memory_space=pltpu.SMEM),
                  pl.BlockSpec((lc, A_WIDTH), cmap),
                  pl.BlockSpec((lc, A_KV_WIDTH), cmap), pl.BlockSpec((lc, A_KV_WIDTH), cmap),
                  pl.BlockSpec(memory_space=pl.ANY)],
        out_specs=pl.BlockSpec((lc, A_WIDTH), cmap),
        out_shape=jax.ShapeDtypeStruct(oa.shape, BF16),
        input_output_aliases={4: 0},
        compiler_params=_cparams(("arbitrary",)),
        name="ctx_gqa_attn",
    )(sink, qa, ka, va, oa)


def _mla_kernel(q_ref, kc_ref, vc_ref, *refs, with_latent):
    if with_latent:
        kx_ref, vx_ref, o_ref, s_ref = refs
    else:
        _, o_ref, s_ref = refs
    lc = kc_ref.shape[0]
    chunks = [(kc_ref, vc_ref, 0, lc, 0)]
    if with_latent:
        n = kx_ref.shape[0]
        for r0 in range(0, n, MLA_KEY_CHUNK):
            chunks.append((kx_ref, vx_ref, r0, min(MLA_KEY_CHUNK, n - r0), lc + r0))

    def lane_tiles(x):
        return [x[:, t * LANES:(t + 1) * LANES] for t in range(x.shape[1] // LANES)]

    for h in range(B_HEADS):
        buf = h % 2
        ql = slice(h * B_HEAD_PAD, (h + 1) * B_HEAD_PAD)
        vl = slice(h * B_VDIM, (h + 1) * B_VDIM)
        qh = q_ref[:, ql]
        m_lanes = None
        for k_ref, _, r0, nr, c0 in chunks:
            s = _dot_nt(qh, k_ref[r0:r0 + nr, ql])
            s_ref[buf, :, c0:c0 + nr] = s
            for part in lane_tiles(s):
                m_lanes = part if m_lanes is None else jnp.maximum(m_lanes, part)
        mx = jnp.max(m_lanes, axis=-1, keepdims=True)
        l_lanes = None
        acc = None
        for _, v_ref, r0, nr, c0 in chunks:
            e = jnp.exp2(s_ref[buf, :, c0:c0 + nr] - mx)
            for part in lane_tiles(e):
                l_lanes = part if l_lanes is None else l_lanes + part
            pv = _dot(e.astype(BF16), v_ref[r0:r0 + nr, vl])
            acc = pv if acc is None else acc + pv
        denom = jnp.sum(l_lanes, axis=-1, keepdims=True)
        o_ref[:, vl] = (acc / denom).astype(BF16)


def _mla_attn(qb, kb, vb, batch, n, lc, out_rows):
    t = batch * n
    tq = min(MLA_Q_TILE, n)
    nq = n // tq
    cblk = t // lc
    qmap = lambda b, j: (b * nq + j, 0)
    ctx = lambda b, j: (cblk + b, 0)
    lat = lambda b, j: (b, 0)
    return pl.pallas_call(
        functools.partial(_mla_kernel, with_latent=True),
        grid=(batch, nq),
        in_specs=[pl.BlockSpec((tq, B_QK_WIDTH), qmap),
                  pl.BlockSpec((lc, B_QK_WIDTH), ctx), pl.BlockSpec((lc, B_WIDTH), ctx),
                  pl.BlockSpec((n, B_QK_WIDTH), lat), pl.BlockSpec((n, B_WIDTH), lat)],
        out_specs=pl.BlockSpec((tq, B_WIDTH), qmap),
        out_shape=jax.ShapeDtypeStruct((out_rows, B_WIDTH), BF16),
        scratch_shapes=[pltpu.VMEM((2, tq, lc + n), F32)],
        compiler_params=_cparams(("arbitrary", "arbitrary")),
        name="mla_attn",
    )(qb, kb, vb, kb, vb)


def _mla_ctx_attn(qb, kb, vb, ob, batch, n, lc):
    cblk = batch * n // lc
    cmap = lambda b: (cblk + b, 0)
    return pl.pallas_call(
        functools.partial(_mla_kernel, with_latent=False),
        grid=(batch,),
        in_specs=[pl.BlockSpec((lc, B_QK_WIDTH), cmap),
                  pl.BlockSpec((lc, B_QK_WIDTH), cmap), pl.BlockSpec((lc, B_WIDTH), cmap),
                  pl.BlockSpec(memory_space=pl.ANY)],
        out_specs=pl.BlockSpec((lc, B_WIDTH), cmap),
        out_shape=jax.ShapeDtypeStruct(ob.shape, BF16),
        input_output_aliases={3: 0},
        scratch_shapes=[pltpu.VMEM((2, lc, lc), F32)],
        compiler_params=_cparams(("arbitrary",)),
        name="mla_ctx_attn",
    )(qb, kb, vb, ob)


def _route_tile(logits, run):
    tm = logits.shape[0]
    lane_i = lax.broadcasted_iota(jnp.int32, logits.shape, 1)
    lane = lane_i.astype(F32)
    nolane = float(LANES)

    def first_argmax(v):
        mx = jnp.max(v, axis=-1, keepdims=True)
        return mx, jnp.min(jnp.where(v == mx, lane, nolane), axis=-1, keepdims=True)

    gl = jnp.where(lane_i < N_GROUPS, logits, NEG_INF)
    gmax, g_sel = first_argmax(gl)
    g_p = 1.0 / jnp.sum(jnp.exp(gl - gmax), axis=-1, keepdims=True)
    lo = N_GROUPS + EXPERTS_PER_GROUP * g_sel
    el = jnp.where(jnp.logical_and(lane >= lo, lane < lo + EXPERTS_PER_GROUP), logits, NEG_INF)
    v1, i1 = first_argmax(el)
    sel1 = lane == i1
    v2, i2 = first_argmax(jnp.where(sel1, NEG_INF, el))
    sel2 = lane == i2
    t2 = jnp.exp(v2 - v1)
    gate1 = g_p / (1.0 + t2)
    gate2 = gate1 * t2

    onehot = jnp.where(jnp.logical_or(sel1, sel2), 1.0, 0.0)
    r_i = lax.broadcasted_iota(jnp.int32, (tm, tm), 0)
    c_i = lax.broadcasted_iota(jnp.int32, (tm, tm), 1)
    lower = jnp.where(c_i < r_i, 1.0, 0.0).astype(BF16)
    before = _dot(lower, onehot.astype(BF16)) + run
    rank1 = jnp.sum(jnp.where(sel1, before, 0.0), axis=-1, keepdims=True)
    rank2 = jnp.sum(jnp.where(sel2, before, 0.0), axis=-1, keepdims=True)
    info = jnp.zeros_like(logits)
    for k, val in enumerate((i1 - N_GROUPS, i2 - N_GROUPS, rank1, rank2, gate1, gate2)):
        info = jnp.where(lane_i == k, val, info)
    return info, run + jnp.sum(onehot, axis=0, keepdims=True)


def _merge_kernel(tmod_ref, tpos_ref, tlen_ref, h_ref, mod_ref, g_ref, oa_ref, ob_ref, u_ref, up_ref,
                  un_ref, gs_ref, wpool_ref, pscale_ref, wa_ref, wb_ref, wc_ref, wo_ref, wr_ref, br_ref,
                  hn_o, fx_o, info_o, cnt_o, run_ref):
    del tmod_ref
    i = pl.program_id(0)

    @pl.when(i == 0)
    def _():
        run_ref[...] = jnp.zeros_like(run_ref)

    pos0 = tpos_ref[i]
    seq_len = tlen_ref[i]
    tm = u_ref.shape[0]
    d = h_ref.shape[1]

    u = u_ref[...]
    kdim = tm + LANES
    zpad = jnp.zeros((LANES - 2 * POOL_HALO, u.shape[1]), BF16)
    ucat = jnp.concatenate([up_ref[...], u, un_ref[...], zpad], axis=0)
    uf = u.astype(F32)
    r_i = lax.broadcasted_iota(jnp.int32, (tm, kdim), 0)
    c_i = lax.broadcasted_iota(jnp.int32, (tm, kdim), 1)
    rel = c_i - POOL_HALO - r_i
    kpos = pos0 + c_i - POOL_HALO
    valid = jnp.logical_and(kpos >= 0, kpos < seq_len)
    tpos = pos0 + lax.broadcasted_iota(jnp.int32, (tm, 1), 0)
    oc_parts = []
    for gi, w in enumerate(POOL_WINDOWS):
        rad = w // 2
        sl = slice(gi * C_GROUP_DIM, (gi + 1) * C_GROUP_DIM)
        band = jnp.where(jnp.logical_and(jnp.abs(rel) <= rad, valid), 1.0, 0.0).astype(BF16)
        win_sum = _dot(band, ucat[:, sl])
        cnt = (jnp.minimum(tpos + rad + 1, seq_len) - jnp.maximum(tpos - rad, 0)).astype(F32)
        pooled = (win_sum / cnt - uf[:, sl]).astype(BF16)
        oc_parts.append(_dot(pooled, wpool_ref[gi]))
    oc = (jnp.concatenate(oc_parts, axis=1) * pscale_ref[...]).astype(BF16)

    y = gs_ref[:, 0:d].astype(F32) * _dot(oa_ref[...], wa_ref[...])
    y = y + gs_ref[:, d:2 * d].astype(F32) * _dot(ob_ref[...], wb_ref[...])
    y = y + gs_ref[:, 2 * d:3 * d].astype(F32) * _dot(oc, wc_ref[...])
    mix = _dot(y.astype(BF16), wo_ref[...])

    m = mod_ref[0]
    hn = h_ref[...] + m[2:3] * mix
    hn_o[...] = hn
    fx = _rms(hn, g_ref[...]) * (1.0 + m[4:5]) + m[3:4]
    fx_hi = fx.astype(BF16)
    fx_lo = (fx - fx_hi.astype(F32)).astype(BF16)
    fx_o[...] = fx_hi
    both = _dot(fx_hi, wr_ref[...])
    logits = (both[:, :ROUTER_LANES] + _dot(fx_lo, wr_ref[:, :ROUTER_LANES])
              + both[:, ROUTER_LANES:]) + br_ref[...]
    info, run = _route_tile(logits, run_ref[...])
    info_o[...] = info
    run_ref[...] = run
    cnt_o[...] = run


def _merge(h_all, mod, norm_g, oa, ob, u, gs, lw, rows, tile_mod, tile_pos, tile_len):
    t_all, d = h_all.shape
    tm = ROW_TILE
    nt = rows // tm
    hb = tm // POOL_HALO
    nhalo = u.shape[0] // POOL_HALO
    row = lambda i, *_: (i, 0)
    const = lambda i, *_: (0, 0)
    const3 = lambda i, *_: (0, 0, 0)
    grid_spec = pltpu.PrefetchScalarGridSpec(
        num_scalar_prefetch=3,
        grid=(nt,),
        in_specs=[pl.BlockSpec((tm, d), row),
                  pl.BlockSpec((1, 8, d), lambda i, tmod, *_: (tmod[i], 0, 0)),
                  pl.BlockSpec((1, d), const),
                  pl.BlockSpec((tm, A_WIDTH), row),
                  pl.BlockSpec((tm, B_WIDTH), row),
                  pl.BlockSpec((tm, C_WIDTH), row),
                  pl.BlockSpec((POOL_HALO, C_WIDTH), lambda i, *_: (jnp.maximum(i * hb - 1, 0), 0)),
                  pl.BlockSpec((POOL_HALO, C_WIDTH), lambda i, *_: (jnp.minimum((i + 1) * hb, nhalo - 1), 0)),
                  pl.BlockSpec((tm, N_BRANCH * d), row),
                  pl.BlockSpec(lw["wpool"].shape, const3),
                  pl.BlockSpec((1, C_WIDTH), const),
                  pl.BlockSpec(lw["wa"].shape, const),
                  pl.BlockSpec(lw["wb"].shape, const),
                  pl.BlockSpec(lw["wc"].shape, const),
                  pl.BlockSpec(lw["wo"].shape, const),
                  pl.BlockSpec(lw["wr"].shape, const),
                  pl.BlockSpec((1, ROUTER_LANES), const)],
        out_specs=[pl.BlockSpec((tm, d), row), pl.BlockSpec((tm, d), row),
                   pl.BlockSpec((tm, ROUTER_LANES), row), pl.BlockSpec((1, ROUTER_LANES), const)],
        scratch_shapes=[pltpu.VMEM((1, ROUTER_LANES), F32)],
    )
    return pl.pallas_call(
        _merge_kernel,
        grid_spec=grid_spec,
        out_shape=[jax.ShapeDtypeStruct((rows, d), F32), jax.ShapeDtypeStruct((rows, d), BF16),
                   jax.ShapeDtypeStruct((rows, ROUTER_LANES), F32),
                   jax.ShapeDtypeStruct((1, ROUTER_LANES), F32)],
        compiler_params=_cparams(("arbitrary",)),
        name="merge",
    )(tile_mod, tile_pos, tile_len, h_all, mod, norm_g, oa, ob, u, u, u, gs, lw["wpool"],
      lw["pscale"], lw["wa"], lw["wb"], lw["wc"], lw["wo"], lw["wr"], lw["br"])


def _moe_kernel(blk_e_ref, nused_ref, x_ref, wgu_ref, wdn_ref, o_ref):
    del blk_e_ref
    i = pl.program_id(0)

    @pl.when(i < nused_ref[0])
    def _():
        gu = _dot(x_ref[...], wgu_ref[0, 0].astype(BF16))
        gt, up = gu[:, :EXPERT_FF], gu[:, EXPERT_FF:]
        act = (gt * jax.nn.sigmoid(gt) * up).astype(BF16)
        o_ref[...] = _dot(act, wdn_ref[0, 0].astype(BF16)).astype(o_ref.dtype)

    @pl.when(i >= nused_ref[0])
    def _():
        o_ref[...] = jnp.zeros_like(o_ref)


def _moe(xg, blk_e, n_used, w_gu, w_dn, layer):
    p, d = xg.shape
    nb = p // MOE_BLOCK
    grid_spec = pltpu.PrefetchScalarGridSpec(
        num_scalar_prefetch=2,
        grid=(nb,),
        in_specs=[pl.BlockSpec((MOE_BLOCK, d), lambda i, *_: (i, 0)),
                  pl.BlockSpec((1, 1, d, 2 * EXPERT_FF), lambda i, be, nu: (layer, be[i], 0, 0)),
                  pl.BlockSpec((1, 1, EXPERT_FF, d), lambda i, be, nu: (layer, be[i], 0, 0))],
        out_specs=pl.BlockSpec((MOE_BLOCK, d), lambda i, *_: (i, 0)),
    )
    return pl.pallas_call(
        _moe_kernel,
        grid_spec=grid_spec,
        out_shape=jax.ShapeDtypeStruct((p, d), BF16),
        compiler_params=_cparams(("arbitrary",)),
        name="moe_experts",
    )(blk_e, n_used, xg, w_gu, w_dn)


def _resid_kernel(tmod_ref, h_ref, mod_ref, y0_ref, y1_ref, info_ref, fg_ref, o_ref, *, final):
    del tmod_ref
    m = mod_ref[0]
    info = info_ref[...]
    ffn = y0_ref[...].astype(F32) * info[:, 4:5] + y1_ref[...].astype(F32) * info[:, 5:6]
    hn = h_ref[...] + m[5:6] * ffn
    if final:
        hn = _rms(hn, fg_ref[...])
    o_ref[...] = hn


def _resid(h, mod, y0, y1, info, final_g, tile_mod, final):
    rows, d = h.shape
    tm = ROW_TILE
    row = lambda i, *_: (i, 0)
    grid_spec = pltpu.PrefetchScalarGridSpec(
        num_scalar_prefetch=1,
        grid=(rows // tm,),
        in_specs=[pl.BlockSpec((tm, d), row),
                  pl.BlockSpec((1, 8, d), lambda i, tmod: (tmod[i], 0, 0)),
                  pl.BlockSpec((tm, d), row), pl.BlockSpec((tm, d), row),
                  pl.BlockSpec((tm, ROUTER_LANES), row),
                  pl.BlockSpec((1, d), lambda i, *_: (0, 0))],
        out_specs=pl.BlockSpec((tm, d), row),
    )
    return pl.pallas_call(
        functools.partial(_resid_kernel, final=final),
        grid_spec=grid_spec,
        out_shape=jax.ShapeDtypeStruct((rows, d), F32),
        compiler_params=_cparams(("arbitrary",)),
        name="ffn_residual",
    )(tile_mod, h, mod, y0, y1, info, final_g)


def _deinterleave(n):
    return np.concatenate([np.arange(0, n, 2), np.arange(1, n, 2)])


def _rope_tables(n, tm):
    rows = n // GRID_W
    row = np.repeat(np.arange(rows), GRID_W).astype(np.float32)
    col = np.tile(np.arange(GRID_W), rows).astype(np.float32)

    def cs(rot_dim):
        axis_dim = rot_dim // 2
        inv = jnp.asarray(ROPE_BASE, F32) ** (-jnp.arange(0, axis_dim, 2, dtype=F32) / axis_dim)
        ang = jnp.concatenate([jnp.asarray(row)[:, None] * inv, jnp.asarray(col)[:, None] * inv], axis=-1)
        return jnp.cos(ang), jnp.sin(ang)

    cos_a, sin_a = cs(A_HEAD_DIM)
    ca = jnp.tile(jnp.concatenate([cos_a, cos_a], axis=1), (1, LANES // A_HEAD_DIM))
    sa = jnp.tile(jnp.concatenate([-sin_a, sin_a], axis=1), (1, LANES // A_HEAD_DIM))
    cos_b, sin_b = cs(B_ROPE)
    ones = jnp.ones((n, B_NOPE), F32)
    tail = LANES - B_NOPE - B_ROPE
    cb = jnp.concatenate([ones, cos_b, cos_b, jnp.ones((n, tail), F32)], axis=1)
    sb = jnp.concatenate([0 * ones, -sin_b, sin_b, jnp.zeros((n, tail), F32)], axis=1)
    ident_c = jnp.ones((tm, LANES), F32)
    ident_s = jnp.zeros((tm, LANES), F32)
    return {"ca": jnp.concatenate([ca, ident_c]), "sa": jnp.concatenate([sa, ident_s]),
            "cb": jnp.concatenate([cb, ident_c]), "sb": jnp.concatenate([sb, ident_s])}


def _layer_weights(l, w_in, q_norm_g, w_uq, kv_norm_g, w_ukv, w_pool, pool_scale, w_br_a, w_br_b,
                   w_br_c, w_out, w_rg, b_rg, w_re, b_re):
    d = w_in.shape[1]
    wi = w_in[l]
    splits = np.cumsum([A_WIDTH, A_KV_WIDTH, A_KV_WIDTH, Q_LORA, KV_LORA, B_ROPE, C_WIDTH])
    qa, ka, va, cq, ckv, kr, u, gates = jnp.split(wi, splits, axis=1)
    pa = _deinterleave(A_HEAD_DIM)
    qa = qa.reshape(d, A_HEADS, A_HEAD_DIM)[:, :, pa].reshape(d, A_WIDTH) * (A_HEAD_DIM ** -0.5)
    ka = ka.reshape(d, A_KV_HEADS, A_HEAD_DIM)[:, :, pa].reshape(d, A_KV_WIDTH)
    pb = _deinterleave(B_ROPE)
    tail = LANES - B_NOPE - B_ROPE
    kr128 = jnp.concatenate([jnp.zeros((d, B_NOPE), F32), kr[:, pb], jnp.zeros((d, tail), F32)], axis=1)
    w1 = jnp.concatenate([qa, ka, va, cq, ckv, kr128, u, gates], axis=1).astype(BF16)

    uq = w_uq[l].reshape(Q_LORA, B_HEADS, B_NOPE + B_ROPE)
    uq = jnp.concatenate([uq[:, :, :B_NOPE], uq[:, :, B_NOPE:][:, :, pb],
                          jnp.zeros((Q_LORA, B_HEADS, tail), F32)], axis=2)
    ukv = w_ukv[l].reshape(KV_LORA, B_HEADS, B_NOPE + B_VDIM)
    uk = jnp.concatenate([ukv[:, :, :B_NOPE], jnp.zeros((KV_LORA, B_HEADS, LANES - B_NOPE), F32)], axis=2)
    uv = ukv[:, :, B_NOPE:]

    wr = jnp.concatenate([w_rg[l], w_re[l], jnp.zeros((d, ROUTER_LANES - N_GROUPS - N_EXPERTS), F32)], axis=1)
    wr_hi = wr.astype(BF16)
    wr_lo = (wr - wr_hi.astype(F32)).astype(BF16)
    br = jnp.concatenate([b_rg[l], b_re[l], jnp.zeros((ROUTER_LANES - N_GROUPS - N_EXPERTS,), F32)])
    return {
        "w1": w1,
        "qng": q_norm_g[l].reshape(1, Q_LORA), "kvng": kv_norm_g[l].reshape(1, KV_LORA),
        "wuq": uq.reshape(Q_LORA, B_QK_WIDTH).astype(BF16),
        "wuk": uk.reshape(KV_LORA, B_QK_WIDTH).astype(BF16),
        "wuv": uv.reshape(KV_LORA, B_WIDTH).astype(BF16),
        "wpool": w_pool[l].astype(BF16), "pscale": pool_scale[l].reshape(1, C_WIDTH),
        "wa": w_br_a[l].astype(BF16), "wb": w_br_b[l].astype(BF16), "wc": w_br_c[l].astype(BF16),
        "wo": w_out[l].astype(BF16),
        "wr": jnp.concatenate([wr_hi, wr_lo], axis=1), "br": br.reshape(1, ROUTER_LANES),
    }


def _block_layout(info, cnt):
    t = info.shape[0]
    a = t * TOP_K
    expert = info[:, 0:TOP_K].astype(jnp.int32)
    rank = info[:, TOP_K:2 * TOP_K].astype(jnp.int32)
    counts = cnt[0, N_GROUPS:N_GROUPS + N_EXPERTS].astype(jnp.int32)
    padded = (counts + MOE_BLOCK - 1) // MOE_BLOCK * MOE_BLOCK
    pad_end = jnp.cumsum(padded)
    pad_start = pad_end - padded
    eids = jnp.arange(N_EXPERTS, dtype=jnp.int32)
    base = jnp.sum(jnp.where(expert[:, :, None] == eids, pad_start, 0), axis=-1)
    slot = base + rank
    n_blocks = -(-(a + N_EXPERTS * (MOE_BLOCK - 1)) // MOE_BLOCK)
    blk_start = jnp.arange(n_blocks, dtype=jnp.int32) * MOE_BLOCK
    blk_e = jnp.minimum(jnp.sum((pad_end[None, :] <= blk_start[:, None]).astype(jnp.int32), axis=1),
                        N_EXPERTS - 1)
    n_used = (pad_end[-1] // MOE_BLOCK).reshape(1)

    by_slot = jnp.argsort(slot.reshape(a)).astype(jnp.int32) // TOP_K
    first = jnp.cumsum(counts) - counts
    blk_sel = blk_e[:, None] == eids[None, :]
    blk_off = blk_start - jnp.sum(jnp.where(blk_sel, pad_start, 0), axis=1)
    blk_src = jnp.sum(jnp.where(blk_sel, first, 0), axis=1) + blk_off
    blk_cnt = jnp.sum(jnp.where(blk_sel, counts, 0), axis=1) - blk_off
    within = jnp.arange(MOE_BLOCK, dtype=jnp.int32)[None, :]
    src = jnp.minimum(blk_src[:, None] + within, a - 1)
    tok_of_slot = jnp.where(within < blk_cnt[:, None], by_slot.at[src].get(mode="promise_in_bounds"), 0)
    return slot, tok_of_slot.reshape(n_blocks * MOE_BLOCK), blk_e, n_used


def kernel(x, c, ctx, c_ctx, w_mod, b_mod, norm_mix_g, norm_ffn_g, w_in, sink, q_norm_g, w_uq, kv_norm_g,
           w_ukv, w_pool, pool_scale, w_br_a, w_br_b, w_br_c, w_out, w_rg, b_rg, w_re, b_re, w_gu, w_dn,
           final_g):
    batch, n, d = x.shape
    lc = ctx.shape[1]
    depth = w_mod.shape[0]
    tm = ROW_TILE
    assert n % tm == 0 and lc % tm == 0 and n % WINDOW == 0 and (batch * n) % lc == 0
    t = batch * n
    tc = batch * lc
    nt_lat, nt_ctx = t // tm, tc // tm
    per_b, per_c = n // tm, lc // tm

    lat_i = np.arange(nt_lat)
    ctx_i = np.arange(nt_ctx)
    tile_mod = jnp.asarray(np.concatenate([lat_i // per_b, np.full(nt_ctx, batch)]), jnp.int32)
    tile_rope = jnp.asarray(np.concatenate([lat_i % per_b, np.full(nt_ctx, per_b)]), jnp.int32)
    tile_pos = jnp.asarray(np.concatenate([(lat_i % per_b) * tm, (ctx_i % per_c) * tm]), jnp.int32)
    tile_len = jnp.asarray(np.concatenate([np.full(nt_lat, n), np.full(nt_ctx, lc)]), jnp.int32)
    tables = _rope_tables(n, tm)

    mod_rows = 16
    cvec = jnp.concatenate([c, c_ctx[None, :], jnp.zeros((mod_rows - batch - 1, d), F32)], axis=0)
    h_all = jnp.concatenate([x.reshape(t, d), ctx.reshape(tc, d)], axis=0)

    mod_all = _modvec(cvec, w_mod, b_mod).reshape(depth, mod_rows, 6, d)
    mod_all = jnp.concatenate([mod_all, jnp.zeros((depth, mod_rows, 2, d), F32)], axis=2)

    out = None
    for l in range(depth):
        last = l == depth - 1
        lw = _layer_weights(l, w_in, q_norm_g, w_uq, kv_norm_g, w_ukv, w_pool, pool_scale, w_br_a,
                            w_br_b, w_br_c, w_out, w_rg, b_rg, w_re, b_re)
        mod = mod_all[l]

        qa, ka, va, qb, kb, vb, u, gs = _inproj(h_all, mod, norm_mix_g[l].reshape(1, d), lw, tables,
                                                tile_mod, tile_rope)
        rows = t if last else t + tc
        oa = _window_attn(qa, ka, va, sink[l], batch, n, lc, rows)
        ob = _mla_attn(qb, kb, vb, batch, n, lc, rows)
        if not last:
            oa = _ctx_gqa_attn(qa, ka, va, sink[l], oa, batch, n, lc)
            ob = _mla_ctx_attn(qb, kb, vb, ob, batch, n, lc)
        hn, fx, info, cnt = _merge(h_all, mod, norm_ffn_g[l].reshape(1, d), oa, ob, u, gs, lw, rows,
                                   tile_mod, tile_pos, tile_len)

        slot, tok_of_slot, blk_e, n_used = _block_layout(info, cnt)
        xg = fx.at[tok_of_slot].get(mode="promise_in_bounds")
        yb = _moe(xg, blk_e, n_used, w_gu, w_dn, l)
        y0 = yb.at[slot[:, 0]].get(mode="promise_in_bounds")
        y1 = yb.at[slot[:, 1]].get(mode="promise_in_bounds")
        res = _resid(hn, mod, y0, y1, info, final_g.reshape(1, d), tile_mod, last)
        if last:
            out = res
        else:
            h_all = res
    return out.reshape(batch, n, d)
```

```python
import functools
import math

import jax
import jax.numpy as jnp
import numpy as np
from jax import lax
from jax.experimental import pallas as pl
from jax.experimental.pallas import tpu as pltpu

GRID_W = 64
ROPE_BASE = 10000.0
EPS = 1e-6
NEG_INF = -1e30

A_HEADS = 8
A_KV_HEADS = 2
A_GROUP = A_HEADS // A_KV_HEADS
A_HEAD_DIM = 64
A_WIDTH = A_HEADS * A_HEAD_DIM
A_KV_WIDTH = A_KV_HEADS * A_HEAD_DIM
WINDOW = 128

B_HEADS = 8
B_NOPE = 64
B_ROPE = 32
B_VDIM = 64
B_WIDTH = B_HEADS * B_VDIM
Q_LORA = 256
KV_LORA = 256

POOL_WINDOWS = (2, 4, 8, 16)
C_GROUPS = 4
C_GROUP_DIM = 128
C_WIDTH = C_GROUPS * C_GROUP_DIM

N_BRANCH = 3
N_GROUPS = 4
EXPERTS_PER_GROUP = 8
N_EXPERTS = N_GROUPS * EXPERTS_PER_GROUP
TOP_K = 2
EXPERT_FF = 256

LANES = 128
BF16_SUBLANES = 16
VMEM_LIMIT_BYTES = 56 * 1024 * 1024

B_HEAD_PAD = LANES
B_QK_WIDTH = B_HEADS * B_HEAD_PAD
POOL_HALO = BF16_SUBLANES
ROW_TILE = 256
MLA_Q_TILE = 256
MLA_KEY_CHUNK = 512
WIN_Q_TILE = 512
MOE_BLOCK = 256
ROUTER_LANES = LANES

_SEG = {}
_off = 0
for _name, _w in (("qa", A_WIDTH), ("ka", A_KV_WIDTH), ("va", A_KV_WIDTH), ("cq", Q_LORA),
                  ("ckv", KV_LORA), ("kr", LANES), ("u", C_WIDTH), ("gates", None)):
    _SEG[_name] = _off
    if _w is not None:
        _off += _w
W1_GATES_OFF = _SEG["gates"]

F32 = jnp.float32
BF16 = jnp.bfloat16


def _dot(a, b):
    return jnp.dot(a, b, preferred_element_type=F32)


def _dot_nt(a, b):
    return lax.dot_general(a, b, (((1,), (1,)), ((), ())), preferred_element_type=F32)


def _cparams(sem):
    return pltpu.CompilerParams(dimension_semantics=sem, vmem_limit_bytes=VMEM_LIMIT_BYTES)


def _rms(x, g):
    return x * lax.rsqrt(jnp.mean(x * x, axis=-1, keepdims=True) + EPS) * g


def _modvec_kernel(c_ref, w_ref, b_ref, o_ref):
    c = c_ref[...]
    a = (c * jax.nn.sigmoid(c)).astype(BF16)
    o_ref[0] = _dot(a, w_ref[0].astype(BF16)) + b_ref[0]


def _modvec(cvec, w_mod, b_mod):
    rows, d = cvec.shape
    depth, _, n = w_mod.shape
    tn = d
    return pl.pallas_call(
        _modvec_kernel,
        grid=(depth, n // tn),
        in_specs=[pl.BlockSpec((rows, d), lambda l, j: (0, 0)),
                  pl.BlockSpec((1, d, tn), lambda l, j: (l, 0, j)),
                  pl.BlockSpec((1, 1, tn), lambda l, j: (l, 0, j))],
        out_specs=pl.BlockSpec((1, rows, tn), lambda l, j: (l, 0, j)),
        out_shape=jax.ShapeDtypeStruct((depth, rows, n), F32),
        compiler_params=_cparams(("arbitrary", "arbitrary")),
        name="modvec",
    )(cvec, w_mod, b_mod.reshape(depth, 1, n))


def _rope_tile(x, cos, sin, half):
    lane = lax.broadcasted_iota(jnp.int32, x.shape, 1)
    fwd = pltpu.roll(x, LANES - half, 1)
    bwd = pltpu.roll(x, half, 1)
    partner = jnp.where((lane % (2 * half)) < half, fwd, bwd)
    return x * cos + partner * sin


def _inproj_kernel(tmod_ref, trope_ref, h_ref, mod_ref, g_ref, w1_ref, qng_ref, kvng_ref, wuq_ref,
                   wuk_ref, wuv_ref, ca_ref, sa_ref, cb_ref, sb_ref,
                   qa_o, ka_o, va_o, qb_o, kb_o, vb_o, u_o, gs_o):
    del tmod_ref, trope_ref
    x = h_ref[...]
    m = mod_ref[0]
    hx = _rms(x, g_ref[...]) * (1.0 + m[1:2]) + m[0:1]
    hb = hx.astype(BF16)

    def seg(name, width):
        off = _SEG[name]
        return _dot(hb, w1_ref[:, off:off + width])

    ca, sa = ca_ref[...], sa_ref[...]
    cb, sb = cb_ref[...], sb_ref[...]

    qa = seg("qa", A_WIDTH)
    for t in range(A_WIDTH // LANES):
        sl = slice(t * LANES, (t + 1) * LANES)
        qa_o[:, sl] = _rope_tile(qa[:, sl], ca, sa, A_HEAD_DIM // 2).astype(BF16)
    ka_o[...] = _rope_tile(seg("ka", A_KV_WIDTH), ca, sa, A_HEAD_DIM // 2).astype(BF16)
    va_o[...] = seg("va", A_KV_WIDTH).astype(BF16)

    cqn = _rms(seg("cq", Q_LORA), qng_ref[...]).astype(BF16)
    ckvn = _rms(seg("ckv", KV_LORA), kvng_ref[...]).astype(BF16)
    qb = _dot(cqn, wuq_ref[...])
    kb = _dot(ckvn, wuk_ref[...])
    kr = _rope_tile(seg("kr", LANES), cb, sb, B_ROPE // 2)
    qscale = (B_NOPE + B_ROPE) ** -0.5 * math.log2(math.e)
    for t in range(B_HEADS):
        sl = slice(t * LANES, (t + 1) * LANES)
        qb_o[:, sl] = (_rope_tile(qb[:, sl], cb, sb, B_ROPE // 2) * qscale).astype(BF16)
        kb_o[:, sl] = (kb[:, sl] + kr).astype(BF16)
    vb_o[...] = _dot(ckvn, wuv_ref[...]).astype(BF16)

    u_o[...] = seg("u", C_WIDTH).astype(BF16)
    ngate = gs_o.shape[1]
    for t in range(ngate // 512):
        off = W1_GATES_OFF + t * 512
        gs_o[:, t * 512:(t + 1) * 512] = jax.nn.sigmoid(_dot(hb, w1_ref[:, off:off + 512])).astype(BF16)


def _inproj(h_all, mod, norm_g, lw, tables, tile_mod, tile_rope):
    t_all, d = h_all.shape
    tm = ROW_TILE
    nt = t_all // tm
    w1 = lw["w1"]
    ngate = N_BRANCH * d
    row = lambda i, *_: (i, 0)
    const = lambda i, *_: (0, 0)
    rope = lambda i, tmod, trope: (trope[i], 0)
    widths = (A_WIDTH, A_KV_WIDTH, A_KV_WIDTH, B_QK_WIDTH, B_QK_WIDTH, B_WIDTH, C_WIDTH, ngate)
    grid_spec = pltpu.PrefetchScalarGridSpec(
        num_scalar_prefetch=2,
        grid=(nt,),
        in_specs=[pl.BlockSpec((tm, d), row),
                  pl.BlockSpec((1, 8, d), lambda i, tmod, trope: (tmod[i], 0, 0)),
                  pl.BlockSpec((1, d), const),
                  pl.BlockSpec(w1.shape, const),
                  pl.BlockSpec((1, Q_LORA), const),
                  pl.BlockSpec((1, KV_LORA), const),
                  pl.BlockSpec(lw["wuq"].shape, const),
                  pl.BlockSpec(lw["wuk"].shape, const),
                  pl.BlockSpec(lw["wuv"].shape, const),
                  pl.BlockSpec((tm, LANES), rope),
                  pl.BlockSpec((tm, LANES), rope),
                  pl.BlockSpec((tm, LANES), rope),
                  pl.BlockSpec((tm, LANES), rope)],
        out_specs=[pl.BlockSpec((tm, w), row) for w in widths],
    )
    return pl.pallas_call(
        _inproj_kernel,
        grid_spec=grid_spec,
        out_shape=[jax.ShapeDtypeStruct((t_all, w), BF16) for w in widths],
        compiler_params=_cparams(("arbitrary",)),
        name="inproj",
    )(tile_mod, tile_rope, h_all, mod, norm_g, w1, lw["qng"], lw["kvng"], lw["wuq"], lw["wuk"],
      lw["wuv"], tables["ca"], tables["sa"], tables["cb"], tables["sb"])


def _gqa_kernel(sink_ref, q_ref, *refs, local, ntile):
    if local:
        kp_ref, kc_ref, kn_ref, vp_ref, vc_ref, vn_ref, kx_ref, vx_ref, o_ref = refs
    else:
        kx_ref, vx_ref, _, o_ref = refs
    q = q_ref[...]
    tq = q.shape[0]
    if local:
        j = pl.program_id(1)
        nloc = tq + 2 * WINDOW
        r = lax.broadcasted_iota(jnp.int32, (tq, nloc), 0)
        c = lax.broadcasted_iota(jnp.int32, (tq, nloc), 1)
        ok = jnp.abs(c - WINDOW - r) <= WINDOW
        ok = jnp.logical_and(ok, jnp.logical_or(c >= WINDOW, j > 0))
        ok = jnp.logical_and(ok, jnp.logical_or(c < WINDOW + tq, j < ntile - 1))
        bias = jnp.where(ok, 0.0, NEG_INF)
        kcat = jnp.concatenate([kp_ref[...], kc_ref[...], kn_ref[...], kx_ref[...]], axis=0)
        vcat = jnp.concatenate([vp_ref[...], vc_ref[...], vn_ref[...], vx_ref[...]], axis=0)
    else:
        kcat, vcat = kx_ref[...], vx_ref[...]
    for h in range(A_HEADS):
        g = h // A_GROUP
        qh = q[:, h * A_HEAD_DIM:(h + 1) * A_HEAD_DIM]
        kg = kcat[:, g * A_HEAD_DIM:(g + 1) * A_HEAD_DIM]
        vg = vcat[:, g * A_HEAD_DIM:(g + 1) * A_HEAD_DIM]
        s = _dot_nt(qh, kg)
        if local:
            s = jnp.concatenate([s[:, :nloc] + bias, s[:, nloc:]], axis=1)
        sk = sink_ref[h]
        mx = jnp.maximum(jnp.max(s, axis=-1, keepdims=True), sk)
        e = jnp.exp(s - mx)
        denom = jnp.sum(e, axis=-1, keepdims=True) + jnp.exp(sk - mx)
        o = _dot(e.astype(BF16), vg) / denom
        o_ref[:, h * A_HEAD_DIM:(h + 1) * A_HEAD_DIM] = o.astype(BF16)


def _window_attn(qa, ka, va, sink, batch, n, lc, out_rows):
    t = batch * n
    tq = min(WIN_Q_TILE, n)
    ntile = n // tq
    sub = tq // WINDOW
    nblk = n // WINDOW
    cblk = t // lc
    qmap = lambda b, j: (b * ntile + j, 0)
    prev = lambda b, j: (b * nblk + jnp.maximum(j * sub - 1, 0), 0)
    nxt = lambda b, j: (b * nblk + jnp.minimum((j + 1) * sub, nblk - 1), 0)
    ctx = lambda b, j: (cblk + b, 0)
    kvw = A_KV_WIDTH
    return pl.pallas_call(
        functools.partial(_gqa_kernel, local=True, ntile=ntile),
        grid=(batch, ntile),
        in_specs=[pl.BlockSpec(memory_space=pltpu.SMEM),
                  pl.BlockSpec((tq, A_WIDTH), qmap),
                  pl.BlockSpec((WINDOW, kvw), prev), pl.BlockSpec((tq, kvw), qmap),
                  pl.BlockSpec((WINDOW, kvw), nxt),
                  pl.BlockSpec((WINDOW, kvw), prev), pl.BlockSpec((tq, kvw), qmap),
                  pl.BlockSpec((WINDOW, kvw), nxt),
                  pl.BlockSpec((lc, kvw), ctx), pl.BlockSpec((lc, kvw), ctx)],
        out_specs=pl.BlockSpec((tq, A_WIDTH), qmap),
        out_shape=jax.ShapeDtypeStruct((out_rows, A_WIDTH), BF16),
        compiler_params=_cparams(("arbitrary", "arbitrary")),
        name="window_attn",
    )(sink, qa, ka, ka, ka, va, va, va, ka, va)


def _ctx_gqa_attn(qa, ka, va, sink, oa, batch, n, lc):
    cblk = batch * n // lc
    cmap = lambda b: (cblk + b, 0)
    return pl.pallas_call(
        functools.partial(_gqa_kernel, local=False, ntile=0),
        grid=(batch,),
        in_specs=[pl.BlockSpec(memory_space=pltpu.SMEM),
                  pl.BlockSpec((lc, A_WIDTH), cmap),
                  pl.BlockSpec((lc, A_KV_WIDTH), cmap), pl.BlockSpec((lc, A_KV_WIDTH), cmap),
                  pl.BlockSpec(memory_space=pl.ANY)],
        out_specs=pl.BlockSpec((lc, A_WIDTH), cmap),
        out_shape=jax.ShapeDtypeStruct(oa.shape, BF16),
        input_output_aliases={4: 0},
        compiler_params=_cparams(("arbitrary",)),
        name="ctx_gqa_attn",
    )(sink, qa, ka, va, oa)


def _mla_kernel(q_ref, kc_ref, vc_ref, *refs, with_latent):
    if with_latent:
        kx_ref, vx_ref, o_ref, s_ref = refs
    else:
        _, o_ref, s_ref = refs
    lc = kc_ref.shape[0]
    chunks = [(kc_ref, vc_ref, 0, lc, 0)]
    if with_latent:
        n = kx_ref.shape[0]
        for r0 in range(0, n, MLA_KEY_CHUNK):
            chunks.append((kx_ref, vx_ref, r0, min(MLA_KEY_CHUNK, n - r0), lc + r0))

    def lane_tiles(x):
        return [x[:, t * LANES:(t + 1) * LANES] for t in range(x.shape[1] // LANES)]

    for h in range(B_HEADS):
        buf = h % 2
        ql = slice(h * B_HEAD_PAD, (h + 1) * B_HEAD_PAD)
        vl = slice(h * B_VDIM, (h + 1) * B_VDIM)
        qh = q_ref[:, ql]
        m_lanes = None
        for k_ref, _, r0, nr, c0 in chunks:
            s = _dot_nt(qh, k_ref[r0:r0 + nr, ql])
            s_ref[buf, :, c0:c0 + nr] = s
            for part in lane_tiles(s):
                m_lanes = part if m_lanes is None else jnp.maximum(m_lanes, part)
        mx = jnp.max(m_lanes, axis=-1, keepdims=True)
        l_lanes = None
        acc = None
        for _, v_ref, r0, nr, c0 in chunks:
            e = jnp.exp2(s_ref[buf, :, c0:c0 + nr] - mx)
            for part in lane_tiles(e):
                l_lanes = part if l_lanes is None else l_lanes + part
            pv = _dot(e.astype(BF16), v_ref[r0:r0 + nr, vl])
            acc = pv if acc is None else acc + pv
        denom = jnp.sum(l_lanes, axis=-1, keepdims=True)
        o_ref[:, vl] = (acc / denom).astype(BF16)


def _mla_attn(qb, kb, vb, batch, n, lc, out_rows):
    t = batch * n
    tq = min(MLA_Q_TILE, n)
    nq = n // tq
    cblk = t // lc
    qmap = lambda b, j: (b * nq + j, 0)
    ctx = lambda b, j: (cblk + b, 0)
    lat = lambda b, j: (b, 0)
    return pl.pallas_call(
        functools.partial(_mla_kernel, with_latent=True),
        grid=(batch, nq),
        in_specs=[pl.BlockSpec((tq, B_QK_WIDTH), qmap),
                  pl.BlockSpec((lc, B_QK_WIDTH), ctx), pl.BlockSpec((lc, B_WIDTH), ctx),
                  pl.BlockSpec((n, B_QK_WIDTH), lat), pl.BlockSpec((n, B_WIDTH), lat)],
        out_specs=pl.BlockSpec((tq, B_WIDTH), qmap),
        out_shape=jax.ShapeDtypeStruct((out_rows, B_WIDTH), BF16),
        scratch_shapes=[pltpu.VMEM((2, tq, lc + n), F32)],
        compiler_params=_cparams(("arbitrary", "arbitrary")),
        name="mla_attn",
    )(qb, kb, vb, kb, vb)


def _mla_ctx_attn(qb, kb, vb, ob, batch, n, lc):
    cblk = batch * n // lc
    cmap = lambda b: (cblk + b, 0)
    return pl.pallas_call(
        functools.partial(_mla_kernel, with_latent=False),
        grid=(batch,),
        in_specs=[pl.BlockSpec((lc, B_QK_WIDTH), cmap),
                  pl.BlockSpec((lc, B_QK_WIDTH), cmap), pl.BlockSpec((lc, B_WIDTH), cmap),
                  pl.BlockSpec(memory_space=pl.ANY)],
        out_specs=pl.BlockSpec((lc, B_WIDTH), cmap),
        out_shape=jax.ShapeDtypeStruct(ob.shape, BF16),
        input_output_aliases={3: 0},
        scratch_shapes=[pltpu.VMEM((2, lc, lc), F32)],
        compiler_params=_cparams(("arbitrary",)),
        name="mla_ctx_attn",
    )(qb, kb, vb, ob)


def _route_tile(logits, run):
    tm = logits.shape[0]
    lane_i = lax.broadcasted_iota(jnp.int32, logits.shape, 1)
    lane = lane_i.astype(F32)
    nolane = float(LANES)

    def first_argmax(v):
        mx = jnp.max(v, axis=-1, keepdims=True)
        return mx, jnp.min(jnp.where(v == mx, lane, nolane), axis=-1, keepdims=True)

    gl = jnp.where(lane_i < N_GROUPS, logits, NEG_INF)
    gmax, g_sel = first_argmax(gl)
    g_p = 1.0 / jnp.sum(jnp.exp(gl - gmax), axis=-1, keepdims=True)
    lo = N_GROUPS + EXPERTS_PER_GROUP * g_sel
    el = jnp.where(jnp.logical_and(lane >= lo, lane < lo + EXPERTS_PER_GROUP), logits, NEG_INF)
    v1, i1 = first_argmax(el)
    sel1 = lane == i1
    v2, i2 = first_argmax(jnp.where(sel1, NEG_INF, el))
    sel2 = lane == i2
    t2 = jnp.exp(v2 - v1)
    gate1 = g_p / (1.0 + t2)
    gate2 = gate1 * t2

    onehot = jnp.where(jnp.logical_or(sel1, sel2), 1.0, 0.0)
    r_i = lax.broadcasted_iota(jnp.int32, (tm, tm), 0)
    c_i = lax.broadcasted_iota(jnp.int32, (tm, tm), 1)
    lower = jnp.where(c_i < r_i, 1.0, 0.0).astype(BF16)
    before = _dot(lower, onehot.astype(BF16)) + run
    rank1 = jnp.sum(jnp.where(sel1, before, 0.0), axis=-1, keepdims=True)
    rank2 = jnp.sum(jnp.where(sel2, before, 0.0), axis=-1, keepdims=True)
    info = jnp.zeros_like(logits)
    for k, val in enumerate((i1 - N_GROUPS, i2 - N_GROUPS, rank1, rank2, gate1, gate2)):
        info = jnp.where(lane_i == k, val, info)
    return info, run + jnp.sum(onehot, axis=0, keepdims=True)


def _merge_kernel(tmod_ref, tpos_ref, tlen_ref, h_ref, mod_ref, g_ref, oa_ref, ob_ref, u_ref, up_ref,
                  un_ref, gs_ref, wpool_ref, pscale_ref, wa_ref, wb_ref, wc_ref, wo_ref, wr_ref, br_ref,
                  hn_o, fx_o, info_o, cnt_o, run_ref):
    del tmod_ref
    i = pl.program_id(0)

    @pl.when(i == 0)
    def _():
        run_ref[...] = jnp.zeros_like(run_ref)

    pos0 = tpos_ref[i]
    seq_len = tlen_ref[i]
    tm = u_ref.shape[0]
    d = h_ref.shape[1]

    u = u_ref[...]
    kdim = tm + LANES
    zpad = jnp.zeros((LANES - 2 * POOL_HALO, u.shape[1]), BF16)
    ucat = jnp.concatenate([up_ref[...], u, un_ref[...], zpad], axis=0)
    uf = u.astype(F32)
    r_i = lax.broadcasted_iota(jnp.int32, (tm, kdim), 0)
    c_i = lax.broadcasted_iota(jnp.int32, (tm, kdim), 1)
    rel = c_i - POOL_HALO - r_i
    kpos = pos0 + c_i - POOL_HALO
    valid = jnp.logical_and(kpos >= 0, kpos < seq_len)
    tpos = pos0 + lax.broadcasted_iota(jnp.int32, (tm, 1), 0)
    oc_parts = []
    for gi, w in enumerate(POOL_WINDOWS):
        rad = w // 2
        sl = slice(gi * C_GROUP_DIM, (gi + 1) * C_GROUP_DIM)
        band = jnp.where(jnp.logical_and(jnp.abs(rel) <= rad, valid), 1.0, 0.0).astype(BF16)
        win_sum = _dot(band, ucat[:, sl])
        cnt = (jnp.minimum(tpos + rad + 1, seq_len) - jnp.maximum(tpos - rad, 0)).astype(F32)
        pooled = (win_sum / cnt - uf[:, sl]).astype(BF16)
        oc_parts.append(_dot(pooled, wpool_ref[gi]))
    oc = (jnp.concatenate(oc_parts, axis=1) * pscale_ref[...]).astype(BF16)

    y = gs_ref[:, 0:d].astype(F32) * _dot(oa_ref[...], wa_ref[...])
    y = y + gs_ref[:, d:2 * d].astype(F32) * _dot(ob_ref[...], wb_ref[...])
    y = y + gs_ref[:, 2 * d:3 * d].astype(F32) * _dot(oc, wc_ref[...])
    mix = _dot(y.astype(BF16), wo_ref[...])

    m = mod_ref[0]
    hn = h_ref[...] + m[2:3] * mix
    hn_o[...] = hn
    fx = _rms(hn, g_ref[...]) * (1.0 + m[4:5]) + m[3:4]
    fx_hi = fx.astype(BF16)
    fx_lo = (fx - fx_hi.astype(F32)).astype(BF16)
    fx_o[...] = fx_hi
    both = _dot(fx_hi, wr_ref[...])
    logits = (both[:, :ROUTER_LANES] + _dot(fx_lo, wr_ref[:, :ROUTER_LANES])
              + both[:, ROUTER_LANES:]) + br_ref[...]
    info, run = _route_tile(logits, run_ref[...])
    info_o[...] = info
    run_ref[...] = run
    cnt_o[...] = run


def _merge(h_all, mod, norm_g, oa, ob, u, gs, lw, rows, tile_mod, tile_pos, tile_len):
    t_all, d = h_all.shape
    tm = ROW_TILE
    nt = rows // tm
    hb = tm // POOL_HALO
    nhalo = u.shape[0] // POOL_HALO
    row = lambda i, *_: (i, 0)
    const = lambda i, *_: (0, 0)
    const3 = lambda i, *_: (0, 0, 0)
    grid_spec = pltpu.PrefetchScalarGridSpec(
        num_scalar_prefetch=3,
        grid=(nt,),
        in_specs=[pl.BlockSpec((tm, d), row),
                  pl.BlockSpec((1, 8, d), lambda i, tmod, *_: (tmod[i], 0, 0)),
                  pl.BlockSpec((1, d), const),
                  pl.BlockSpec((tm, A_WIDTH), row),
                  pl.BlockSpec((tm, B_WIDTH), row),
                  pl.BlockSpec((tm, C_WIDTH), row),
                  pl.BlockSpec((POOL_HALO, C_WIDTH), lambda i, *_: (jnp.maximum(i * hb - 1, 0), 0)),
                  pl.BlockSpec((POOL_HALO, C_WIDTH), lambda i, *_: (jnp.minimum((i + 1) * hb, nhalo - 1), 0)),
                  pl.BlockSpec((tm, N_BRANCH * d), row),
                  pl.BlockSpec(lw["wpool"].shape, const3),
                  pl.BlockSpec((1, C_WIDTH), const),
                  pl.BlockSpec(lw["wa"].shape, const),
                  pl.BlockSpec(lw["wb"].shape, const),
                  pl.BlockSpec(lw["wc"].shape, const),
                  pl.BlockSpec(lw["wo"].shape, const),
                  pl.BlockSpec(lw["wr"].shape, const),
                  pl.BlockSpec((1, ROUTER_LANES), const)],
        out_specs=[pl.BlockSpec((tm, d), row), pl.BlockSpec((tm, d), row),
                   pl.BlockSpec((tm, ROUTER_LANES), row), pl.BlockSpec((1, ROUTER_LANES), const)],
        scratch_shapes=[pltpu.VMEM((1, ROUTER_LANES), F32)],
    )
    return pl.pallas_call(
        _merge_kernel,
        grid_spec=grid_spec,
        out_shape=[jax.ShapeDtypeStruct((rows, d), F32), jax.ShapeDtypeStruct((rows, d), BF16),
                   jax.ShapeDtypeStruct((rows, ROUTER_LANES), F32),
                   jax.ShapeDtypeStruct((1, ROUTER_LANES), F32)],
        compiler_params=_cparams(("arbitrary",)),
        name="merge",
    )(tile_mod, tile_pos, tile_len, h_all, mod, norm_g, oa, ob, u, u, u, gs, lw["wpool"],
      lw["pscale"], lw["wa"], lw["wb"], lw["wc"], lw["wo"], lw["wr"], lw["br"])


def _moe_kernel(blk_e_ref, nused_ref, x_ref, wgu_ref, wdn_ref, o_ref):
    del blk_e_ref
    i = pl.program_id(0)

    @pl.when(i < nused_ref[0])
    def _():
        gu = _dot(x_ref[...], wgu_ref[0, 0].astype(BF16))
        gt, up = gu[:, :EXPERT_FF], gu[:, EXPERT_FF:]
        act = (gt * jax.nn.sigmoid(gt) * up).astype(BF16)
        o_ref[...] = _dot(act, wdn_ref[0, 0].astype(BF16)).astype(o_ref.dtype)

    @pl.when(i >= nused_ref[0])
    def _():
        o_ref[...] = jnp.zeros_like(o_ref)


def _moe(xg, blk_e, n_used, w_gu, w_dn, layer):
    p, d = xg.shape
    nb = p // MOE_BLOCK
    grid_spec = pltpu.PrefetchScalarGridSpec(
        num_scalar_prefetch=2,
        grid=(nb,),
        in_specs=[pl.BlockSpec((MOE_BLOCK, d), lambda i, *_: (i, 0)),
                  pl.BlockSpec((1, 1, d, 2 * EXPERT_FF), lambda i, be, nu: (layer, be[i], 0, 0)),
                  pl.BlockSpec((1, 1, EXPERT_FF, d), lambda i, be, nu: (layer, be[i], 0, 0))],
        out_specs=pl.BlockSpec((MOE_BLOCK, d), lambda i, *_: (i, 0)),
    )
    return pl.pallas_call(
        _moe_kernel,
        grid_spec=grid_spec,
        out_shape=jax.ShapeDtypeStruct((p, d), BF16),
        compiler_params=_cparams(("arbitrary",)),
        name="moe_experts",
    )(blk_e, n_used, xg, w_gu, w_dn)


def _resid_kernel(tmod_ref, h_ref, mod_ref, y0_ref, y1_ref, info_ref, fg_ref, o_ref, *, final):
    del tmod_ref
    m = mod_ref[0]
    info = info_ref[...]
    ffn = y0_ref[...].astype(F32) * info[:, 4:5] + y1_ref[...].astype(F32) * info[:, 5:6]
    hn = h_ref[...] + m[5:6] * ffn
    if final:
        hn = _rms(hn, fg_ref[...])
    o_ref[...] = hn


def _resid(h, mod, y0, y1, info, final_g, tile_mod, final):
    rows, d = h.shape
    tm = ROW_TILE
    row = lambda i, *_: (i, 0)
    grid_spec = pltpu.PrefetchScalarGridSpec(
        num_scalar_prefetch=1,
        grid=(rows // tm,),
        in_specs=[pl.BlockSpec((tm, d), row),
                  pl.BlockSpec((1, 8, d), lambda i, tmod: (tmod[i], 0, 0)),
                  pl.BlockSpec((tm, d), row), pl.BlockSpec((tm, d), row),
                  pl.BlockSpec((tm, ROUTER_LANES), row),
                  pl.BlockSpec((1, d), lambda i, *_: (0, 0))],
        out_specs=pl.BlockSpec((tm, d), row),
    )
    return pl.pallas_call(
        functools.partial(_resid_kernel, final=final),
        grid_spec=grid_spec,
        out_shape=jax.ShapeDtypeStruct((rows, d), F32),
        compiler_params=_cparams(("arbitrary",)),
        name="ffn_residual",
    )(tile_mod, h, mod, y0, y1, info, final_g)


def _deinterleave(n):
    return np.concatenate([np.arange(0, n, 2), np.arange(1, n, 2)])


def _rope_tables(n, tm):
    rows = n // GRID_W
    row = np.repeat(np.arange(rows), GRID_W).astype(np.float32)
    col = np.tile(np.arange(GRID_W), rows).astype(np.float32)

    def cs(rot_dim):
        axis_dim = rot_dim // 2
        inv = jnp.asarray(ROPE_BASE, F32) ** (-jnp.arange(0, axis_dim, 2, dtype=F32) / axis_dim)
        ang = jnp.concatenate([jnp.asarray(row)[:, None] * inv, jnp.asarray(col)[:, None] * inv], axis=-1)
        return jnp.cos(ang), jnp.sin(ang)

    cos_a, sin_a = cs(A_HEAD_DIM)
    ca = jnp.tile(jnp.concatenate([cos_a, cos_a], axis=1), (1, LANES // A_HEAD_DIM))
    sa = jnp.tile(jnp.concatenate([-sin_a, sin_a], axis=1), (1, LANES // A_HEAD_DIM))
    cos_b, sin_b = cs(B_ROPE)
    ones = jnp.ones((n, B_NOPE), F32)
    tail = LANES - B_NOPE - B_ROPE
    cb = jnp.concatenate([ones, cos_b, cos_b, jnp.ones((n, tail), F32)], axis=1)
    sb = jnp.concatenate([0 * ones, -sin_b, sin_b, jnp.zeros((n, tail), F32)], axis=1)
    ident_c = jnp.ones((tm, LANES), F32)
    ident_s = jnp.zeros((tm, LANES), F32)
    return {"ca": jnp.concatenate([ca, ident_c]), "sa": jnp.concatenate([sa, ident_s]),
            "cb": jnp.concatenate([cb, ident_c]), "sb": jnp.concatenate([sb, ident_s])}


def _layer_weights(l, w_in, q_norm_g, w_uq, kv_norm_g, w_ukv, w_pool, pool_scale, w_br_a, w_br_b,
                   w_br_c, w_out, w_rg, b_rg, w_re, b_re):
    d = w_in.shape[1]
    wi = w_in[l]
    splits = np.cumsum([A_WIDTH, A_KV_WIDTH, A_KV_WIDTH, Q_LORA, KV_LORA, B_ROPE, C_WIDTH])
    qa, ka, va, cq, ckv, kr, u, gates = jnp.split(wi, splits, axis=1)
    pa = _deinterleave(A_HEAD_DIM)
    qa = qa.reshape(d, A_HEADS, A_HEAD_DIM)[:, :, pa].reshape(d, A_WIDTH) * (A_HEAD_DIM ** -0.5)
    ka = ka.reshape(d, A_KV_HEADS, A_HEAD_DIM)[:, :, pa].reshape(d, A_KV_WIDTH)
    pb = _deinterleave(B_ROPE)
    tail = LANES - B_NOPE - B_ROPE
    kr128 = jnp.concatenate([jnp.zeros((d, B_NOPE), F32), kr[:, pb], jnp.zeros((d, tail), F32)], axis=1)
    w1 = jnp.concatenate([qa, ka, va, cq, ckv, kr128, u, gates], axis=1).astype(BF16)

    uq = w_uq[l].reshape(Q_LORA, B_HEADS, B_NOPE + B_ROPE)
    uq = jnp.concatenate([uq[:, :, :B_NOPE], uq[:, :, B_NOPE:][:, :, pb],
                          jnp.zeros((Q_LORA, B_HEADS, tail), F32)], axis=2)
    ukv = w_ukv[l].reshape(KV_LORA, B_HEADS, B_NOPE + B_VDIM)
    uk = jnp.concatenate([ukv[:, :, :B_NOPE], jnp.zeros((KV_LORA, B_HEADS, LANES - B_NOPE), F32)], axis=2)
    uv = ukv[:, :, B_NOPE:]

    wr = jnp.concatenate([w_rg[l], w_re[l], jnp.zeros((d, ROUTER_LANES - N_GROUPS - N_EXPERTS), F32)], axis=1)
    wr_hi = wr.astype(BF16)
    wr_lo = (wr - wr_hi.astype(F32)).astype(BF16)
    br = jnp.concatenate([b_rg[l], b_re[l], jnp.zeros((ROUTER_LANES - N_GROUPS - N_EXPERTS,), F32)])
    return {
        "w1": w1,
        "qng": q_norm_g[l].reshape(1, Q_LORA), "kvng": kv_norm_g[l].reshape(1, KV_LORA),
        "wuq": uq.reshape(Q_LORA, B_QK_WIDTH).astype(BF16),
        "wuk": uk.reshape(KV_LORA, B_QK_WIDTH).astype(BF16),
        "wuv": uv.reshape(KV_LORA, B_WIDTH).astype(BF16),
        "wpool": w_pool[l].astype(BF16), "pscale": pool_scale[l].reshape(1, C_WIDTH),
        "wa": w_br_a[l].astype(BF16), "wb": w_br_b[l].astype(BF16), "wc": w_br_c[l].astype(BF16),
        "wo": w_out[l].astype(BF16),
        "wr": jnp.concatenate([wr_hi, wr_lo], axis=1), "br": br.reshape(1, ROUTER_LANES),
    }


def _block_layout(info, cnt):
    t = info.shape[0]
    a = t * TOP_K
    expert = info[:, 0:TOP_K].astype(jnp.int32)
    rank = info[:, TOP_K:2 * TOP_K].astype(jnp.int32)
    counts = cnt[0, N_GROUPS:N_GROUPS + N_EXPERTS].astype(jnp.int32)
    padded = (counts + MOE_BLOCK - 1) // MOE_BLOCK * MOE_BLOCK
    pad_end = jnp.cumsum(padded)
    pad_start = pad_end - padded
    eids = jnp.arange(N_EXPERTS, dtype=jnp.int32)
    base = jnp.sum(jnp.where(expert[:, :, None] == eids, pad_start, 0), axis=-1)
    slot = base + rank
    n_blocks = -(-(a + N_EXPERTS * (MOE_BLOCK - 1)) // MOE_BLOCK)
    blk_start = jnp.arange(n_blocks, dtype=jnp.int32) * MOE_BLOCK
    blk_e = jnp.minimum(jnp.sum((pad_end[None, :] <= blk_start[:, None]).astype(jnp.int32), axis=1),
                        N_EXPERTS - 1)
    n_used = (pad_end[-1] // MOE_BLOCK).reshape(1)

    by_slot = jnp.argsort(slot.reshape(a)).astype(jnp.int32) // TOP_K
    first = jnp.cumsum(counts) - counts
    blk_sel = blk_e[:, None] == eids[None, :]
    blk_off = blk_start - jnp.sum(jnp.where(blk_sel, pad_start, 0), axis=1)
    blk_src = jnp.sum(jnp.where(blk_sel, first, 0), axis=1) + blk_off
    blk_cnt = jnp.sum(jnp.where(blk_sel, counts, 0), axis=1) - blk_off
    within = jnp.arange(MOE_BLOCK, dtype=jnp.int32)[None, :]
    src = jnp.minimum(blk_src[:, None] + within, a - 1)
    filler = (blk_start[:, None] + within) % t
    tok_of_slot = jnp.where(within < blk_cnt[:, None], by_slot.at[src].get(mode="promise_in_bounds"),
                            filler)
    return slot, tok_of_slot.reshape(n_blocks * MOE_BLOCK), blk_e, n_used


def kernel(x, c, ctx, c_ctx, w_mod, b_mod, norm_mix_g, norm_ffn_g, w_in, sink, q_norm_g, w_uq, kv_norm_g,
           w_ukv, w_pool, pool_scale, w_br_a, w_br_b, w_br_c, w_out, w_rg, b_rg, w_re, b_re, w_gu, w_dn,
           final_g):
    batch, n, d = x.shape
    lc = ctx.shape[1]
    depth = w_mod.shape[0]
    tm = ROW_TILE
    assert n % tm == 0 and lc % tm == 0 and n % WINDOW == 0 and (batch * n) % lc == 0
    t = batch * n
    tc = batch * lc
    nt_lat, nt_ctx = t // tm, tc // tm
    per_b, per_c = n // tm, lc // tm

    lat_i = np.arange(nt_lat)
    ctx_i = np.arange(nt_ctx)
    tile_mod = jnp.asarray(np.concatenate([lat_i // per_b, np.full(nt_ctx, batch)]), jnp.int32)
    tile_rope = jnp.asarray(np.concatenate([lat_i % per_b, np.full(nt_ctx, per_b)]), jnp.int32)
    tile_pos = jnp.asarray(np.concatenate([(lat_i % per_b) * tm, (ctx_i % per_c) * tm]), jnp.int32)
    tile_len = jnp.asarray(np.concatenate([np.full(nt_lat, n), np.full(nt_ctx, lc)]), jnp.int32)
    tables = _rope_tables(n, tm)

    mod_rows = 16
    cvec = jnp.concatenate([c, c_ctx[None, :], jnp.zeros((mod_rows - batch - 1, d), F32)], axis=0)
    h_all = jnp.concatenate([x.reshape(t, d), ctx.reshape(tc, d)], axis=0)

    mod_all = _modvec(cvec, w_mod, b_mod).reshape(depth, mod_rows, 6, d)
    mod_all = jnp.concatenate([mod_all, jnp.zeros((depth, mod_rows, 2, d), F32)], axis=2)

    out = None
    for l in range(depth):
        last = l == depth - 1
        lw = _layer_weights(l, w_in, q_norm_g, w_uq, kv_norm_g, w_ukv, w_pool, pool_scale, w_br_a,
                            w_br_b, w_br_c, w_out, w_rg, b_rg, w_re, b_re)
        mod = mod_all[l]

        qa, ka, va, qb, kb, vb, u, gs = _inproj(h_all, mod, norm_mix_g[l].reshape(1, d), lw, tables,
                                                tile_mod, tile_rope)
        rows = t if last else t + tc
        oa = _window_attn(qa, ka, va, sink[l], batch, n, lc, rows)
        ob = _mla_attn(qb, kb, vb, batch, n, lc, rows)
        if not last:
            oa = _ctx_gqa_attn(qa, ka, va, sink[l], oa, batch, n, lc)
            ob = _mla_ctx_attn(qb, kb, vb, ob, batch, n, lc)
        hn, fx, info, cnt = _merge(h_all, mod, norm_ffn_g[l].reshape(1, d), oa, ob, u, gs, lw, rows,
                                   tile_mod, tile_pos, tile_len)

        slot, tok_of_slot, blk_e, n_used = _block_layout(info, cnt)
        xg = fx.at[tok_of_slot].get(mode="promise_in_bounds")
        yb = _moe(xg, blk_e, n_used, w_gu, w_dn, l)
        y0 = yb.at[slot[:, 0]].get(mode="promise_in_bounds")
        y1 = yb.at[slot[:, 1]].get(mode="promise_in_bounds")
        res = _resid(hn, mod, y0, y1, info, final_g.reshape(1, d), tile_mod, last)
        if last:
            out = res
        else:
            h_all = res
    return out.reshape(batch, n, d)
```

```python
import functools
import math

import jax
import jax.numpy as jnp
import numpy as np
from jax import lax
from jax.experimental import pallas as pl
from jax.experimental.pallas import tpu as pltpu

GRID_W = 64
ROPE_BASE = 10000.0
EPS = 1e-6
NEG_INF = -1e30

A_HEADS = 8
A_KV_HEADS = 2
A_GROUP = A_HEADS // A_KV_HEADS
A_HEAD_DIM = 64
A_WIDTH = A_HEADS * A_HEAD_DIM
A_KV_WIDTH = A_KV_HEADS * A_HEAD_DIM
WINDOW = 128

B_HEADS = 8
B_NOPE = 64
B_ROPE = 32
B_VDIM = 64
B_WIDTH = B_HEADS * B_VDIM
Q_LORA = 256
KV_LORA = 256

POOL_WINDOWS = (2, 4, 8, 16)
C_GROUPS = 4
C_GROUP_DIM = 128
C_WIDTH = C_GROUPS * C_GROUP_DIM

N_BRANCH = 3
N_GROUPS = 4
EXPERTS_PER_GROUP = 8
N_EXPERTS = N_GROUPS * EXPERTS_PER_GROUP
TOP_K = 2
EXPERT_FF = 256

LANES = 128
BF16_SUBLANES = 16
VMEM_LIMIT_BYTES = 56 * 1024 * 1024

B_HEAD_PAD = LANES
B_QK_WIDTH = B_HEADS * B_HEAD_PAD
POOL_HALO = BF16_SUBLANES
ROW_TILE = 256
MLA_Q_TILE = 256
MLA_KEY_CHUNK = 256
WIN_Q_TILE = 512
MOE_BLOCK = 256
ROUTER_LANES = LANES

_SEG = {}
_off = 0
for _name, _w in (("qa", A_WIDTH), ("ka", A_KV_WIDTH), ("va", A_KV_WIDTH), ("cq", Q_LORA),
                  ("ckv", KV_LORA), ("kr", LANES), ("u", C_WIDTH), ("gates", None)):
    _SEG[_name] = _off
    if _w is not None:
        _off += _w
W1_GATES_OFF = _SEG["gates"]
VB_OUT = 5

F32 = jnp.float32
BF16 = jnp.bfloat16


def _dot(a, b):
    return jnp.dot(a, b, preferred_element_type=F32)


def _dot_nt(a, b):
    return lax.dot_general(a, b, (((1,), (1,)), ((), ())), preferred_element_type=F32)


def _cparams(sem):
    return pltpu.CompilerParams(dimension_semantics=sem, vmem_limit_bytes=VMEM_LIMIT_BYTES)


def _rms(x, g):
    return x * lax.rsqrt(jnp.mean(x * x, axis=-1, keepdims=True) + EPS) * g


def _modvec_kernel(c_ref, w_ref, b_ref, o_ref):
    c = c_ref[...]
    a = (c * jax.nn.sigmoid(c)).astype(BF16)
    o_ref[0] = _dot(a, w_ref[0].astype(BF16)) + b_ref[0]


def _modvec(cvec, w_mod, b_mod):
    rows, d = cvec.shape
    depth, _, n = w_mod.shape
    tn = d
    return pl.pallas_call(
        _modvec_kernel,
        grid=(depth, n // tn),
        in_specs=[pl.BlockSpec((rows, d), lambda l, j: (0, 0)),
                  pl.BlockSpec((1, d, tn), lambda l, j: (l, 0, j)),
                  pl.BlockSpec((1, 1, tn), lambda l, j: (l, 0, j))],
        out_specs=pl.BlockSpec((1, rows, tn), lambda l, j: (l, 0, j)),
        out_shape=jax.ShapeDtypeStruct((depth, rows, n), F32),
        compiler_params=_cparams(("arbitrary", "arbitrary")),
        name="modvec",
    )(cvec, w_mod, b_mod.reshape(depth, 1, n))


def _rope_tile(x, cos, sin, half):
    lane = lax.broadcasted_iota(jnp.int32, x.shape, 1)
    fwd = pltpu.roll(x, LANES - half, 1)
    bwd = pltpu.roll(x, half, 1)
    partner = jnp.where((lane % (2 * half)) < half, fwd, bwd)
    return x * cos + partner * sin


def _inproj_kernel(tmod_ref, trope_ref, h_ref, mod_ref, g_ref, w1_ref, qng_ref, kvng_ref, wuq_ref,
                   wuk_ref, wuv_ref, ca_ref, sa_ref, cb_ref, sb_ref,
                   qa_o, ka_o, va_o, qb_o, kb_o, vb_o, u_o, gs_o):
    del tmod_ref, trope_ref
    x = h_ref[...]
    m = mod_ref[0]
    hx = _rms(x, g_ref[...]) * (1.0 + m[1:2]) + m[0:1]
    hb = hx.astype(BF16)

    def seg(name, width):
        off = _SEG[name]
        return _dot(hb, w1_ref[:, off:off + width])

    ca, sa = ca_ref[...], sa_ref[...]
    cb, sb = cb_ref[...], sb_ref[...]

    qa = seg("qa", A_WIDTH)
    for t in range(A_WIDTH // LANES):
        sl = slice(t * LANES, (t + 1) * LANES)
        qa_o[:, sl] = _rope_tile(qa[:, sl], ca, sa, A_HEAD_DIM // 2).astype(BF16)
    ka_o[...] = _rope_tile(seg("ka", A_KV_WIDTH), ca, sa, A_HEAD_DIM // 2).astype(BF16)
    va_o[...] = seg("va", A_KV_WIDTH).astype(BF16)

    cqn = _rms(seg("cq", Q_LORA), qng_ref[...]).astype(BF16)
    ckvn = _rms(seg("ckv", KV_LORA), kvng_ref[...]).astype(BF16)
    qb = _dot(cqn, wuq_ref[...])
    kb = _dot(ckvn, wuk_ref[...])
    kr = _rope_tile(seg("kr", LANES), cb, sb, B_ROPE // 2)
    qscale = (B_NOPE + B_ROPE) ** -0.5 * math.log2(math.e)
    for t in range(B_HEADS):
        sl = slice(t * LANES, (t + 1) * LANES)
        qb_o[:, sl] = (_rope_tile(qb[:, sl], cb, sb, B_ROPE // 2) * qscale).astype(BF16)
        kb_o[:, sl] = (kb[:, sl] + kr).astype(BF16)
    vb_o[...] = _dot(ckvn, wuv_ref[...]).T.astype(BF16)

    u_o[...] = seg("u", C_WIDTH).astype(BF16)
    ngate = gs_o.shape[1]
    for t in range(ngate // 512):
        off = W1_GATES_OFF + t * 512
        gs_o[:, t * 512:(t + 1) * 512] = jax.nn.sigmoid(_dot(hb, w1_ref[:, off:off + 512])).astype(BF16)


def _inproj(h_all, mod, norm_g, lw, tables, tile_mod, tile_rope):
    t_all, d = h_all.shape
    tm = ROW_TILE
    nt = t_all // tm
    w1 = lw["w1"]
    ngate = N_BRANCH * d
    row = lambda i, *_: (i, 0)
    const = lambda i, *_: (0, 0)
    rope = lambda i, tmod, trope: (trope[i], 0)
    widths = (A_WIDTH, A_KV_WIDTH, A_KV_WIDTH, B_QK_WIDTH, B_QK_WIDTH, B_WIDTH, C_WIDTH, ngate)
    grid_spec = pltpu.PrefetchScalarGridSpec(
        num_scalar_prefetch=2,
        grid=(nt,),
        in_specs=[pl.BlockSpec((tm, d), row),
                  pl.BlockSpec((1, 8, d), lambda i, tmod, trope: (tmod[i], 0, 0)),
                  pl.BlockSpec((1, d), const),
                  pl.BlockSpec(w1.shape, const),
                  pl.BlockSpec((1, Q_LORA), const),
                  pl.BlockSpec((1, KV_LORA), const),
                  pl.BlockSpec(lw["wuq"].shape, const),
                  pl.BlockSpec(lw["wuk"].shape, const),
                  pl.BlockSpec(lw["wuv"].shape, const),
                  pl.BlockSpec((tm, LANES), rope),
                  pl.BlockSpec((tm, LANES), rope),
                  pl.BlockSpec((tm, LANES), rope),
                  pl.BlockSpec((tm, LANES), rope)],
        out_specs=[pl.BlockSpec((B_WIDTH, tm), lambda i, *_: (0, i)) if k == VB_OUT
                   else pl.BlockSpec((tm, w), row) for k, w in enumerate(widths)],
    )
    return pl.pallas_call(
        _inproj_kernel,
        grid_spec=grid_spec,
        out_shape=[jax.ShapeDtypeStruct((B_WIDTH, t_all) if k == VB_OUT else (t_all, w), BF16)
                   for k, w in enumerate(widths)],
        compiler_params=_cparams(("arbitrary",)),
        name="inproj",
    )(tile_mod, tile_rope, h_all, mod, norm_g, w1, lw["qng"], lw["kvng"], lw["wuq"], lw["wuk"],
      lw["wuv"], tables["ca"], tables["sa"], tables["cb"], tables["sb"])


def _gqa_kernel(sink_ref, q_ref, *refs, local, ntile):
    if local:
        kp_ref, kc_ref, kn_ref, vp_ref, vc_ref, vn_ref, kx_ref, vx_ref, o_ref = refs
    else:
        kx_ref, vx_ref, _, o_ref = refs
    q = q_ref[...]
    tq = q.shape[0]
    if local:
        j = pl.program_id(1)
        nloc = tq + 2 * WINDOW
        r = lax.broadcasted_iota(jnp.int32, (tq, nloc), 0)
        c = lax.broadcasted_iota(jnp.int32, (tq, nloc), 1)
        ok = jnp.abs(c - WINDOW - r) <= WINDOW
        ok = jnp.logical_and(ok, jnp.logical_or(c >= WINDOW, j > 0))
        ok = jnp.logical_and(ok, jnp.logical_or(c < WINDOW + tq, j < ntile - 1))
        bias = jnp.where(ok, 0.0, NEG_INF)
        kcat = jnp.concatenate([kp_ref[...], kc_ref[...], kn_ref[...], kx_ref[...]], axis=0)
        vcat = jnp.concatenate([vp_ref[...], vc_ref[...], vn_ref[...], vx_ref[...]], axis=0)
    else:
        kcat, vcat = kx_ref[...], vx_ref[...]
    for h in range(A_HEADS):
        g = h // A_GROUP
        qh = q[:, h * A_HEAD_DIM:(h + 1) * A_HEAD_DIM]
        kg = kcat[:, g * A_HEAD_DIM:(g + 1) * A_HEAD_DIM]
        vg = vcat[:, g * A_HEAD_DIM:(g + 1) * A_HEAD_DIM]
        s = _dot_nt(qh, kg)
        if local:
            s = jnp.concatenate([s[:, :nloc] + bias, s[:, nloc:]], axis=1)
        sk = sink_ref[h]
        mx = jnp.maximum(jnp.max(s, axis=-1, keepdims=True), sk)
        e = jnp.exp(s - mx)
        denom = jnp.sum(e, axis=-1, keepdims=True) + jnp.exp(sk - mx)
        o = _dot(e.astype(BF16), vg) / denom
        o_ref[:, h * A_HEAD_DIM:(h + 1) * A_HEAD_DIM] = o.astype(BF16)


def _window_attn(qa, ka, va, sink, batch, n, lc, out_rows):
    t = batch * n
    tq = min(WIN_Q_TILE, n)
    ntile = n // tq
    sub = tq // WINDOW
    nblk = n // WINDOW
    cblk = t // lc
    qmap = lambda b, j: (b * ntile + j, 0)
    prev = lambda b, j: (b * nblk + jnp.maximum(j * sub - 1, 0), 0)
    nxt = lambda b, j: (b * nblk + jnp.minimum((j + 1) * sub, nblk - 1), 0)
    ctx = lambda b, j: (cblk + b, 0)
    kvw = A_KV_WIDTH
    return pl.pallas_call(
        functools.partial(_gqa_kernel, local=True, ntile=ntile),
        grid=(batch, ntile),
        in_specs=[pl.BlockSpec(memory_space=pltpu.SMEM),
                  pl.BlockSpec((tq, A_WIDTH), qmap),
                  pl.BlockSpec((WINDOW, kvw), prev), pl.BlockSpec((tq, kvw), qmap),
                  pl.BlockSpec((WINDOW, kvw), nxt),
                  pl.BlockSpec((WINDOW, kvw), prev), pl.BlockSpec((tq, kvw), qmap),
                  pl.BlockSpec((WINDOW, kvw), nxt),
                  pl.BlockSpec((lc, kvw), ctx), pl.BlockSpec((lc, kvw), ctx)],
        out_specs=pl.BlockSpec((tq, A_WIDTH), qmap),
        out_shape=jax.ShapeDtypeStruct((out_rows, A_WIDTH), BF16),
        compiler_params=_cparams(("arbitrary", "arbitrary")),
        name="window_attn",
    )(sink, qa, ka, ka, ka, va, va, va, ka, va)


def _ctx_gqa_attn(qa, ka, va, sink, oa, batch, n, lc):
    cblk = batch * n // lc
    cmap = lambda b: (cblk + b, 0)
    return pl.pallas_call(
        functools.partial(_gqa_kernel, local=False, ntile=0),
        grid=(batch,),
        in_specs=[pl.BlockSpec(memory_space=pltpu.SMEM),
                  pl.BlockSpec((lc, A_WIDTH), cmap),
                  pl.BlockSpec((lc, A_KV_WIDTH), cmap), pl.BlockSpec((lc, A_KV_WIDTH), cmap),
                  pl.BlockSpec(memory_space=pl.ANY)],
        out_specs=pl.BlockSpec((lc, A_WIDTH), cmap),
        out_shape=jax.ShapeDtypeStruct(oa.shape, BF16),
        input_output_aliases={4: 0},
        compiler_params=_cparams(("arbitrary",)),
        name="ctx_gqa_attn",
    )(sink, qa, ka, va, oa)


def _mla_kernel(q_ref, kc_ref, vc_ref, *refs, with_latent):
    if with_latent:
        kx_ref, vx_ref, o_ref, s_ref = refs
    else:
        _, o_ref, s_ref = refs
    tq = q_ref.shape[0]
    lc = kc_ref.shape[0]
    chunks = [(kc_ref, vc_ref, 0, lc, 0)]
    if with_latent:
        n = kx_ref.shape[0]
        for r0 in range(0, n, MLA_KEY_CHUNK):
            chunks.append((kx_ref, vx_ref, r0, min(MLA_KEY_CHUNK, n - r0), lc + r0))

    def fold(x, op):
        while x.shape[0] > 8:
            half = x.shape[0] // 2
            x = op(x[:half], x[half:])
        return x

    outs = []
    for h in range(B_HEADS):
        buf = h % 2
        ql = slice(h * B_HEAD_PAD, (h + 1) * B_HEAD_PAD)
        vl = slice(h * B_VDIM, (h + 1) * B_VDIM)
        qh = q_ref[:, ql]
        m8 = None
        for k_ref, _, r0, nr, c0 in chunks:
            st = _dot_nt(k_ref[r0:r0 + nr, ql], qh)
            s_ref[buf, c0:c0 + nr, :] = st
            part = fold(st, jnp.maximum)
            m8 = part if m8 is None else jnp.maximum(m8, part)
        mx = jnp.max(m8, axis=0, keepdims=True)
        l8 = None
        acc = None
        for _, vt_ref, r0, nr, c0 in chunks:
            et = jnp.exp2(s_ref[buf, c0:c0 + nr, :] - mx)
            part = fold(et, jnp.add)
            l8 = part if l8 is None else l8 + part
            pv = _dot(vt_ref[vl, r0:r0 + nr], et.astype(BF16))
            acc = pv if acc is None else acc + pv
        outs.append(acc / jnp.sum(l8, axis=0, keepdims=True))
    o_ref[...] = jnp.concatenate(outs, axis=0).T.astype(BF16)


def _mla_attn(qb, kb, vb, batch, n, lc, out_rows):
    t = batch * n
    tq = min(MLA_Q_TILE, n)
    nq = n // tq
    cblk = t // lc
    qmap = lambda b, j: (b * nq + j, 0)
    ctx = lambda b, j: (cblk + b, 0)
    lat = lambda b, j: (b, 0)
    ctx_t = lambda b, j: (0, cblk + b)
    lat_t = lambda b, j: (0, b)
    return pl.pallas_call(
        functools.partial(_mla_kernel, with_latent=True),
        grid=(batch, nq),
        in_specs=[pl.BlockSpec((tq, B_QK_WIDTH), qmap),
                  pl.BlockSpec((lc, B_QK_WIDTH), ctx), pl.BlockSpec((B_WIDTH, lc), ctx_t),
                  pl.BlockSpec((n, B_QK_WIDTH), lat), pl.BlockSpec((B_WIDTH, n), lat_t)],
        out_specs=pl.BlockSpec((tq, B_WIDTH), qmap),
        out_shape=jax.ShapeDtypeStruct((out_rows, B_WIDTH), BF16),
        scratch_shapes=[pltpu.VMEM((2, lc + n, tq), F32)],
        compiler_params=_cparams(("arbitrary", "arbitrary")),
        name="mla_attn",
    )(qb, kb, vb, kb, vb)


def _mla_ctx_attn(qb, kb, vb, ob, batch, n, lc):
    cblk = batch * n // lc
    cmap = lambda b: (cblk + b, 0)
    return pl.pallas_call(
        functools.partial(_mla_kernel, with_latent=False),
        grid=(batch,),
        in_specs=[pl.BlockSpec((lc, B_QK_WIDTH), cmap),
                  pl.BlockSpec((lc, B_QK_WIDTH), cmap),
                  pl.BlockSpec((B_WIDTH, lc), lambda b: (0, cblk + b)),
                  pl.BlockSpec(memory_space=pl.ANY)],
        out_specs=pl.BlockSpec((lc, B_WIDTH), cmap),
        out_shape=jax.ShapeDtypeStruct(ob.shape, BF16),
        input_output_aliases={3: 0},
        scratch_shapes=[pltpu.VMEM((2, lc, lc), F32)],
        compiler_params=_cparams(("arbitrary",)),
        name="mla_ctx_attn",
    )(qb, kb, vb, ob)


def _route_tile(logits, run):
    tm = logits.shape[0]
    lane_i = lax.broadcasted_iota(jnp.int32, logits.shape, 1)
    lane = lane_i.astype(F32)
    nolane = float(LANES)

    def first_argmax(v):
        mx = jnp.max(v, axis=-1, keepdims=True)
        return mx, jnp.min(jnp.where(v == mx, lane, nolane), axis=-1, keepdims=True)

    gl = jnp.where(lane_i < N_GROUPS, logits, NEG_INF)
    gmax, g_sel = first_argmax(gl)
    g_p = 1.0 / jnp.sum(jnp.exp(gl - gmax), axis=-1, keepdims=True)
    lo = N_GROUPS + EXPERTS_PER_GROUP * g_sel
    el = jnp.where(jnp.logical_and(lane >= lo, lane < lo + EXPERTS_PER_GROUP), logits, NEG_INF)
    v1, i1 = first_argmax(el)
    sel1 = lane == i1
    v2, i2 = first_argmax(jnp.where(sel1, NEG_INF, el))
    sel2 = lane == i2
    t2 = jnp.exp(v2 - v1)
    gate1 = g_p / (1.0 + t2)
    gate2 = gate1 * t2

    onehot = jnp.where(jnp.logical_or(sel1, sel2), 1.0, 0.0)
    r_i = lax.broadcasted_iota(jnp.int32, (tm, tm), 0)
    c_i = lax.broadcasted_iota(jnp.int32, (tm, tm), 1)
    lower = jnp.where(c_i < r_i, 1.0, 0.0).astype(BF16)
    before = _dot(lower, onehot.astype(BF16)) + run
    rank1 = jnp.sum(jnp.where(sel1, before, 0.0), axis=-1, keepdims=True)
    rank2 = jnp.sum(jnp.where(sel2, before, 0.0), axis=-1, keepdims=True)
    info = jnp.zeros_like(logits)
    for k, val in enumerate((i1 - N_GROUPS, i2 - N_GROUPS, rank1, rank2, gate1, gate2)):
        info = jnp.where(lane_i == k, val, info)
    return info, run + jnp.sum(onehot, axis=0, keepdims=True)


def _merge_kernel(tmod_ref, tpos_ref, tlen_ref, h_ref, mod_ref, g_ref, oa_ref, ob_ref, u_ref, up_ref,
                  un_ref, gs_ref, wpool_ref, pscale_ref, wa_ref, wb_ref, wc_ref, wo_ref, wr_ref, br_ref,
                  hn_o, fx_o, info_o, cnt_o, run_ref):
    del tmod_ref
    i = pl.program_id(0)

    @pl.when(i == 0)
    def _():
        run_ref[...] = jnp.zeros_like(run_ref)

    pos0 = tpos_ref[i]
    seq_len = tlen_ref[i]
    tm = u_ref.shape[0]
    d = h_ref.shape[1]

    u = u_ref[...]
    kdim = tm + LANES
    zpad = jnp.zeros((LANES - 2 * POOL_HALO, u.shape[1]), BF16)
    ucat = jnp.concatenate([up_ref[...], u, un_ref[...], zpad], axis=0)
    uf = u.astype(F32)
    r_i = lax.broadcasted_iota(jnp.int32, (tm, kdim), 0)
    c_i = lax.broadcasted_iota(jnp.int32, (tm, kdim), 1)
    rel = c_i - POOL_HALO - r_i
    kpos = pos0 + c_i - POOL_HALO
    valid = jnp.logical_and(kpos >= 0, kpos < seq_len)
    tpos = pos0 + lax.broadcasted_iota(jnp.int32, (tm, 1), 0)
    oc_parts = []
    for gi, w in enumerate(POOL_WINDOWS):
        rad = w // 2
        sl = slice(gi * C_GROUP_DIM, (gi + 1) * C_GROUP_DIM)
        band = jnp.where(jnp.logical_and(jnp.abs(rel) <= rad, valid), 1.0, 0.0).astype(BF16)
        win_sum = _dot(band, ucat[:, sl])
        cnt = (jnp.minimum(tpos + rad + 1, seq_len) - jnp.maximum(tpos - rad, 0)).astype(F32)
        pooled = (win_sum / cnt - uf[:, sl]).astype(BF16)
        oc_parts.append(_dot(pooled, wpool_ref[gi]))
    oc = (jnp.concatenate(oc_parts, axis=1) * pscale_ref[...]).astype(BF16)

    y = gs_ref[:, 0:d].astype(F32) * _dot(oa_ref[...], wa_ref[...])
    y = y + gs_ref[:, d:2 * d].astype(F32) * _dot(ob_ref[...], wb_ref[...])
    y = y + gs_ref[:, 2 * d:3 * d].astype(F32) * _dot(oc, wc_ref[...])
    mix = _dot(y.astype(BF16), wo_ref[...])

    m = mod_ref[0]
    hn = h_ref[...] + m[2:3] * mix
    hn_o[...] = hn
    fx = _rms(hn, g_ref[...]) * (1.0 + m[4:5]) + m[3:4]
    fx_hi = fx.astype(BF16)
    fx_lo = (fx - fx_hi.astype(F32)).astype(BF16)
    fx_o[...] = fx_hi
    both = _dot(fx_hi, wr_ref[...])
    logits = (both[:, :ROUTER_LANES] + _dot(fx_lo, wr_ref[:, :ROUTER_LANES])
              + both[:, ROUTER_LANES:]) + br_ref[...]
    info, run = _route_tile(logits, run_ref[...])
    info_o[...] = info
    run_ref[...] = run
    cnt_o[...] = run


def _merge(h_all, mod, norm_g, oa, ob, u, gs, lw, rows, tile_mod, tile_pos, tile_len):
    t_all, d = h_all.shape
    tm = ROW_TILE
    nt = rows // tm
    hb = tm // POOL_HALO
    nhalo = u.shape[0] // POOL_HALO
    row = lambda i, *_: (i, 0)
    const = lambda i, *_: (0, 0)
    const3 = lambda i, *_: (0, 0, 0)
    grid_spec = pltpu.PrefetchScalarGridSpec(
        num_scalar_prefetch=3,
        grid=(nt,),
        in_specs=[pl.BlockSpec((tm, d), row),
                  pl.BlockSpec((1, 8, d), lambda i, tmod, *_: (tmod[i], 0, 0)),
                  pl.BlockSpec((1, d), const),
                  pl.BlockSpec((tm, A_WIDTH), row),
                  pl.BlockSpec((tm, B_WIDTH), row),
                  pl.BlockSpec((tm, C_WIDTH), row),
                  pl.BlockSpec((POOL_HALO, C_WIDTH), lambda i, *_: (jnp.maximum(i * hb - 1, 0), 0)),
                  pl.BlockSpec((POOL_HALO, C_WIDTH), lambda i, *_: (jnp.minimum((i + 1) * hb, nhalo - 1), 0)),
                  pl.BlockSpec((tm, N_BRANCH * d), row),
                  pl.BlockSpec(lw["wpool"].shape, const3),
                  pl.BlockSpec((1, C_WIDTH), const),
                  pl.BlockSpec(lw["wa"].shape, const),
                  pl.BlockSpec(lw["wb"].shape, const),
                  pl.BlockSpec(lw["wc"].shape, const),
                  pl.BlockSpec(lw["wo"].shape, const),
                  pl.BlockSpec(lw["wr"].shape, const),
                  pl.BlockSpec((1, ROUTER_LANES), const)],
        out_specs=[pl.BlockSpec((tm, d), row), pl.BlockSpec((tm, d), row),
                   pl.BlockSpec((tm, ROUTER_LANES), row), pl.BlockSpec((1, ROUTER_LANES), const)],
        scratch_shapes=[pltpu.VMEM((1, ROUTER_LANES), F32)],
    )
    return pl.pallas_call(
        _merge_kernel,
        grid_spec=grid_spec,
        out_shape=[jax.ShapeDtypeStruct((rows, d), F32), jax.ShapeDtypeStruct((rows, d), BF16),
                   jax.ShapeDtypeStruct((rows, ROUTER_LANES), F32),
                   jax.ShapeDtypeStruct((1, ROUTER_LANES), F32)],
        compiler_params=_cparams(("arbitrary",)),
        name="merge",
    )(tile_mod, tile_pos, tile_len, h_all, mod, norm_g, oa, ob, u, u, u, gs, lw["wpool"],
      lw["pscale"], lw["wa"], lw["wb"], lw["wc"], lw["wo"], lw["wr"], lw["br"])


def _moe_kernel(blk_e_ref, nused_ref, x_ref, wgu_ref, wdn_ref, o_ref):
    del blk_e_ref
    i = pl.program_id(0)

    @pl.when(i < nused_ref[0])
    def _():
        gu = _dot(x_ref[...], wgu_ref[0, 0].astype(BF16))
        gt, up = gu[:, :EXPERT_FF], gu[:, EXPERT_FF:]
        act = (gt * jax.nn.sigmoid(gt) * up).astype(BF16)
        o_ref[...] = _dot(act, wdn_ref[0, 0].astype(BF16)).astype(o_ref.dtype)

    @pl.when(i >= nused_ref[0])
    def _():
        o_ref[...] = jnp.zeros_like(o_ref)


def _moe(xg, blk_e, n_used, w_gu, w_dn, layer):
    p, d = xg.shape
    nb = p // MOE_BLOCK
    grid_spec = pltpu.PrefetchScalarGridSpec(
        num_scalar_prefetch=2,
        grid=(nb,),
        in_specs=[pl.BlockSpec((MOE_BLOCK, d), lambda i, *_: (i, 0)),
                  pl.BlockSpec((1, 1, d, 2 * EXPERT_FF), lambda i, be, nu: (layer, be[i], 0, 0)),
                  pl.BlockSpec((1, 1, EXPERT_FF, d), lambda i, be, nu: (layer, be[i], 0, 0))],
        out_specs=pl.BlockSpec((MOE_BLOCK, d), lambda i, *_: (i, 0)),
    )
    return pl.pallas_call(
        _moe_kernel,
        grid_spec=grid_spec,
        out_shape=jax.ShapeDtypeStruct((p, d), BF16),
        compiler_params=_cparams(("arbitrary",)),
        name="moe_experts",
    )(blk_e, n_used, xg, w_gu, w_dn)


def _resid_kernel(tmod_ref, h_ref, mod_ref, y0_ref, y1_ref, info_ref, fg_ref, o_ref, *, final):
    del tmod_ref
    m = mod_ref[0]
    info = info_ref[...]
    ffn = y0_ref[...].astype(F32) * info[:, 4:5] + y1_ref[...].astype(F32) * info[:, 5:6]
    hn = h_ref[...] + m[5:6] * ffn
    if final:
        hn = _rms(hn, fg_ref[...])
    o_ref[...] = hn


def _resid(h, mod, y0, y1, info, final_g, tile_mod, final):
    rows, d = h.shape
    tm = ROW_TILE
    row = lambda i, *_: (i, 0)
    grid_spec = pltpu.PrefetchScalarGridSpec(
        num_scalar_prefetch=1,
        grid=(rows // tm,),
        in_specs=[pl.BlockSpec((tm, d), row),
                  pl.BlockSpec((1, 8, d), lambda i, tmod: (tmod[i], 0, 0)),
                  pl.BlockSpec((tm, d), row), pl.BlockSpec((tm, d), row),
                  pl.BlockSpec((tm, ROUTER_LANES), row),
                  pl.BlockSpec((1, d), lambda i, *_: (0, 0))],
        out_specs=pl.BlockSpec((tm, d), row),
    )
    return pl.pallas_call(
        functools.partial(_resid_kernel, final=final),
        grid_spec=grid_spec,
        out_shape=jax.ShapeDtypeStruct((rows, d), F32),
        compiler_params=_cparams(("arbitrary",)),
        name="ffn_residual",
    )(tile_mod, h, mod, y0, y1, info, final_g)


def _deinterleave(n):
    return np.concatenate([np.arange(0, n, 2), np.arange(1, n, 2)])


def _rope_tables(n, tm):
    rows = n // GRID_W
    row = np.repeat(np.arange(rows), GRID_W).astype(np.float32)
    col = np.tile(np.arange(GRID_W), rows).astype(np.float32)

    def cs(rot_dim):
        axis_dim = rot_dim // 2
        inv = jnp.asarray(ROPE_BASE, F32) ** (-jnp.arange(0, axis_dim, 2, dtype=F32) / axis_dim)
        ang = jnp.concatenate([jnp.asarray(row)[:, None] * inv, jnp.asarray(col)[:, None] * inv], axis=-1)
        return jnp.cos(ang), jnp.sin(ang)

    cos_a, sin_a = cs(A_HEAD_DIM)
    ca = jnp.tile(jnp.concatenate([cos_a, cos_a], axis=1), (1, LANES // A_HEAD_DIM))
    sa = jnp.tile(jnp.concatenate([-sin_a, sin_a], axis=1), (1, LANES // A_HEAD_DIM))
    cos_b, sin_b = cs(B_ROPE)
    ones = jnp.ones((n, B_NOPE), F32)
    tail = LANES - B_NOPE - B_ROPE
    cb = jnp.concatenate([ones, cos_b, cos_b, jnp.ones((n, tail), F32)], axis=1)
    sb = jnp.concatenate([0 * ones, -sin_b, sin_b, jnp.zeros((n, tail), F32)], axis=1)
    ident_c = jnp.ones((tm, LANES), F32)
    ident_s = jnp.zeros((tm, LANES), F32)
    return {"ca": jnp.concatenate([ca, ident_c]), "sa": jnp.concatenate([sa, ident_s]),
            "cb": jnp.concatenate([cb, ident_c]), "sb": jnp.concatenate([sb, ident_s])}


def _layer_weights(l, w_in, q_norm_g, w_uq, kv_norm_g, w_ukv, w_pool, pool_scale, w_br_a, w_br_b,
                   w_br_c, w_out, w_rg, b_rg, w_re, b_re):
    d = w_in.shape[1]
    wi = w_in[l]
    splits = np.cumsum([A_WIDTH, A_KV_WIDTH, A_KV_WIDTH, Q_LORA, KV_LORA, B_ROPE, C_WIDTH])
    qa, ka, va, cq, ckv, kr, u, gates = jnp.split(wi, splits, axis=1)
    pa = _deinterleave(A_HEAD_DIM)
    qa = qa.reshape(d, A_HEADS, A_HEAD_DIM)[:, :, pa].reshape(d, A_WIDTH) * (A_HEAD_DIM ** -0.5)
    ka = ka.reshape(d, A_KV_HEADS, A_HEAD_DIM)[:, :, pa].reshape(d, A_KV_WIDTH)
    pb = _deinterleave(B_ROPE)
    tail = LANES - B_NOPE - B_ROPE
    kr128 = jnp.concatenate([jnp.zeros((d, B_NOPE), F32), kr[:, pb], jnp.zeros((d, tail), F32)], axis=1)
    w1 = jnp.concatenate([qa, ka, va, cq, ckv, kr128, u, gates], axis=1).astype(BF16)

    uq = w_uq[l].reshape(Q_LORA, B_HEADS, B_NOPE + B_ROPE)
    uq = jnp.concatenate([uq[:, :, :B_NOPE], uq[:, :, B_NOPE:][:, :, pb],
                          jnp.zeros((Q_LORA, B_HEADS, tail), F32)], axis=2)
    ukv = w_ukv[l].reshape(KV_LORA, B_HEADS, B_NOPE + B_VDIM)
    uk = jnp.concatenate([ukv[:, :, :B_NOPE], jnp.zeros((KV_LORA, B_HEADS, LANES - B_NOPE), F32)], axis=2)
    uv = ukv[:, :, B_NOPE:]

    wr = jnp.concatenate([w_rg[l], w_re[l], jnp.zeros((d, ROUTER_LANES - N_GROUPS - N_EXPERTS), F32)], axis=1)
    wr_hi = wr.astype(BF16)
    wr_lo = (wr - wr_hi.astype(F32)).astype(BF16)
    br = jnp.concatenate([b_rg[l], b_re[l], jnp.zeros((ROUTER_LANES - N_GROUPS - N_EXPERTS,), F32)])
    return {
        "w1": w1,
        "qng": q_norm_g[l].reshape(1, Q_LORA), "kvng": kv_norm_g[l].reshape(1, KV_LORA),
        "wuq": uq.reshape(Q_LORA, B_QK_WIDTH).astype(BF16),
        "wuk": uk.reshape(KV_LORA, B_QK_WIDTH).astype(BF16),
        "wuv": uv.reshape(KV_LORA, B_WIDTH).astype(BF16),
        "wpool": w_pool[l].astype(BF16), "pscale": pool_scale[l].reshape(1, C_WIDTH),
        "wa": w_br_a[l].astype(BF16), "wb": w_br_b[l].astype(BF16), "wc": w_br_c[l].astype(BF16),
        "wo": w_out[l].astype(BF16),
        "wr": jnp.concatenate([wr_hi, wr_lo], axis=1), "br": br.reshape(1, ROUTER_LANES),
    }


def _block_layout(info, cnt):
    t = info.shape[0]
    a = t * TOP_K
    expert = info[:, 0:TOP_K].astype(jnp.int32)
    rank = info[:, TOP_K:2 * TOP_K].astype(jnp.int32)
    counts = cnt[0, N_GROUPS:N_GROUPS + N_EXPERTS].astype(jnp.int32)
    padded = (counts + MOE_BLOCK - 1) // MOE_BLOCK * MOE_BLOCK
    pad_end = jnp.cumsum(padded)
    pad_start = pad_end - padded
    eids = jnp.arange(N_EXPERTS, dtype=jnp.int32)
    base = jnp.sum(jnp.where(expert[:, :, None] == eids, pad_start, 0), axis=-1)
    slot = base + rank
    n_blocks = -(-(a + N_EXPERTS * (MOE_BLOCK - 1)) // MOE_BLOCK)
    blk_start = jnp.arange(n_blocks, dtype=jnp.int32) * MOE_BLOCK
    blk_e = jnp.minimum(jnp.sum((pad_end[None, :] <= blk_start[:, None]).astype(jnp.int32), axis=1),
                        N_EXPERTS - 1)
    n_used = (pad_end[-1] // MOE_BLOCK).reshape(1)

    by_slot = jnp.argsort(slot.reshape(a)).astype(jnp.int32) // TOP_K
    first = jnp.cumsum(counts) - counts
    blk_sel = blk_e[:, None] == eids[None, :]
    blk_off = blk_start - jnp.sum(jnp.where(blk_sel, pad_start, 0), axis=1)
    blk_src = jnp.sum(jnp.where(blk_sel, first, 0), axis=1) + blk_off
    blk_cnt = jnp.sum(jnp.where(blk_sel, counts, 0), axis=1) - blk_off
    within = jnp.arange(MOE_BLOCK, dtype=jnp.int32)[None, :]
    src = jnp.minimum(blk_src[:, None] + within, a - 1)
    filler = (blk_start[:, None] + within) % t
    tok_of_slot = jnp.where(within < blk_cnt[:, None], by_slot.at[src].get(mode="promise_in_bounds"),
                            filler)
    return slot, tok_of_slot.reshape(n_blocks * MOE_BLOCK), blk_e, n_used


def kernel(x, c, ctx, c_ctx, w_mod, b_mod, norm_mix_g, norm_ffn_g, w_in, sink, q_norm_g, w_uq, kv_norm_g,
           w_ukv, w_pool, pool_scale, w_br_a, w_br_b, w_br_c, w_out, w_rg, b_rg, w_re, b_re, w_gu, w_dn,
           final_g):
    batch, n, d = x.shape
    lc = ctx.shape[1]
    depth = w_mod.shape[0]
    tm = ROW_TILE
    assert n % tm == 0 and lc % tm == 0 and n % WINDOW == 0 and (batch * n) % lc == 0
    t = batch * n
    tc = batch * lc
    nt_lat, nt_ctx = t // tm, tc // tm
    per_b, per_c = n // tm, lc // tm

    lat_i = np.arange(nt_lat)
    ctx_i = np.arange(nt_ctx)
    tile_mod = jnp.asarray(np.concatenate([lat_i // per_b, np.full(nt_ctx, batch)]), jnp.int32)
    tile_rope = jnp.asarray(np.concatenate([lat_i % per_b, np.full(nt_ctx, per_b)]), jnp.int32)
    tile_pos = jnp.asarray(np.concatenate([(lat_i % per_b) * tm, (ctx_i % per_c) * tm]), jnp.int32)
    tile_len = jnp.asarray(np.concatenate([np.full(nt_lat, n), np.full(nt_ctx, lc)]), jnp.int32)
    tables = _rope_tables(n, tm)

    mod_rows = 16
    cvec = jnp.concatenate([c, c_ctx[None, :], jnp.zeros((mod_rows - batch - 1, d), F32)], axis=0)
    h_all = jnp.concatenate([x.reshape(t, d), ctx.reshape(tc, d)], axis=0)

    mod_all = _modvec(cvec, w_mod, b_mod).reshape(depth, mod_rows, 6, d)
    mod_all = jnp.concatenate([mod_all, jnp.zeros((depth, mod_rows, 2, d), F32)], axis=2)

    out = None
    for l in range(depth):
        last = l == depth - 1
        lw = _layer_weights(l, w_in, q_norm_g, w_uq, kv_norm_g, w_ukv, w_pool, pool_scale, w_br_a,
                            w_br_b, w_br_c, w_out, w_rg, b_rg, w_re, b_re)
        mod = mod_all[l]

        qa, ka, va, qb, kb, vb, u, gs = _inproj(h_all, mod, norm_mix_g[l].reshape(1, d), lw, tables,
                                                tile_mod, tile_rope)
        rows = t if last else t + tc
        oa = _window_attn(qa, ka, va, sink[l], batch, n, lc, rows)
        ob = _mla_attn(qb, kb, vb, batch, n, lc, rows)
        if not last:
            oa = _ctx_gqa_attn(qa, ka, va, sink[l], oa, batch, n, lc)
            ob = _mla_ctx_attn(qb, kb, vb, ob, batch, n, lc)
        hn, fx, info, cnt = _merge(h_all, mod, norm_ffn_g[l].reshape(1, d), oa, ob, u, gs, lw, rows,
                                   tile_mod, tile_pos, tile_len)

        slot, tok_of_slot, blk_e, n_used = _block_layout(info, cnt)
        xg = fx.at[tok_of_slot].get(mode="promise_in_bounds")
        yb = _moe(xg, blk_e, n_used, w_gu, w_dn, l)
        y0 = yb.at[slot[:, 0]].get(mode="promise_in_bounds")
        y1 = yb.at[slot[:, 1]].get(mode="promise_in_bounds")
        res = _resid(hn, mod, y0, y1, info, final_g.reshape(1, d), tile_mod, last)
        if last:
            out = res
        else:
            h_all = res
    return out.reshape(batch, n, d)
```

```python
import functools
import math

import jax
import jax.numpy as jnp
import numpy as np
from jax import lax
from jax.experimental import pallas as pl
from jax.experimental.pallas import tpu as pltpu

GRID_W = 64
ROPE_BASE = 10000.0
EPS = 1e-6
NEG_INF = -1e30

A_HEADS = 8
A_KV_HEADS = 2
A_GROUP = A_HEADS // A_KV_HEADS
A_HEAD_DIM = 64
A_WIDTH = A_HEADS * A_HEAD_DIM
A_KV_WIDTH = A_KV_HEADS * A_HEAD_DIM
WINDOW = 128

B_HEADS = 8
B_NOPE = 64
B_ROPE = 32
B_VDIM = 64
B_WIDTH = B_HEADS * B_VDIM
Q_LORA = 256
KV_LORA = 256

POOL_WINDOWS = (2, 4, 8, 16)
C_GROUPS = 4
C_GROUP_DIM = 128
C_WIDTH = C_GROUPS * C_GROUP_DIM

N_BRANCH = 3
N_GROUPS = 4
EXPERTS_PER_GROUP = 8
N_EXPERTS = N_GROUPS * EXPERTS_PER_GROUP
TOP_K = 2
EXPERT_FF = 256

LANES = 128
BF16_SUBLANES = 16
VMEM_LIMIT_BYTES = 56 * 1024 * 1024

B_HEAD_PAD = LANES
B_QK_WIDTH = B_HEADS * B_HEAD_PAD
POOL_HALO = BF16_SUBLANES
ROW_TILE = 256
MLA_Q_TILE = 256
MLA_KEY_CHUNK = 256
WIN_Q_TILE = 512
MOE_BLOCK = 512
ROUTER_LANES = LANES

_SEG = {}
_off = 0
for _name, _w in (("qa", A_WIDTH), ("ka", A_KV_WIDTH), ("va", A_KV_WIDTH), ("cq", Q_LORA),
                  ("ckv", KV_LORA), ("kr", LANES), ("u", C_WIDTH), ("gates", None)):
    _SEG[_name] = _off
    if _w is not None:
        _off += _w
W1_GATES_OFF = _SEG["gates"]
assert A_KV_HEADS * A_HEAD_DIM == LANES
VB_OUT = 5

F32 = jnp.float32
BF16 = jnp.bfloat16


def _dot(a, b):
    return jnp.dot(a, b, preferred_element_type=F32)


def _dot_nt(a, b):
    return lax.dot_general(a, b, (((1,), (1,)), ((), ())), preferred_element_type=F32)


def _cparams(sem):
    return pltpu.CompilerParams(dimension_semantics=sem, vmem_limit_bytes=VMEM_LIMIT_BYTES)


def _rms(x, g):
    return x * lax.rsqrt(jnp.mean(x * x, axis=-1, keepdims=True) + EPS) * g


def _modvec_kernel(c_ref, w_ref, b_ref, o_ref):
    c = c_ref[...]
    a = (c * jax.nn.sigmoid(c)).astype(BF16)
    o_ref[0] = _dot(a, w_ref[0].astype(BF16)) + b_ref[0]


def _modvec(cvec, w_mod, b_mod):
    rows, d = cvec.shape
    depth, _, n = w_mod.shape
    tn = d
    return pl.pallas_call(
        _modvec_kernel,
        grid=(depth, n // tn),
        in_specs=[pl.BlockSpec((rows, d), lambda l, j: (0, 0)),
                  pl.BlockSpec((1, d, tn), lambda l, j: (l, 0, j)),
                  pl.BlockSpec((1, 1, tn), lambda l, j: (l, 0, j))],
        out_specs=pl.BlockSpec((1, rows, tn), lambda l, j: (l, 0, j)),
        out_shape=jax.ShapeDtypeStruct((depth, rows, n), F32),
        compiler_params=_cparams(("arbitrary", "arbitrary")),
        name="modvec",
    )(cvec, w_mod, b_mod.reshape(depth, 1, n))


def _rope_tile(x, cos, sin, half):
    lane = lax.broadcasted_iota(jnp.int32, x.shape, 1)
    fwd = pltpu.roll(x, LANES - half, 1)
    bwd = pltpu.roll(x, half, 1)
    partner = jnp.where((lane % (2 * half)) < half, fwd, bwd)
    return x * cos + partner * sin


def _ffn_residual(h, mod_rows, y0, y1, info):
    ffn = y0.astype(F32) * info[:, 4:5] + y1.astype(F32) * info[:, 5:6]
    return h + mod_rows[5:6] * ffn


def _inproj_kernel(tmod_ref, trope_ref, *refs, pending):
    del tmod_ref, trope_ref
    if pending:
        h_ref, y0_ref, y1_ref, info_ref, modp_ref = refs[:5]
        refs = refs[5:]
    else:
        h_ref = refs[0]
        refs = refs[1:]
    (mod_ref, g_ref, w1_ref, qng_ref, kvng_ref, wuq_ref, wuk_ref, wuv_ref, ca_ref, sa_ref, cb_ref,
     sb_ref, qa_o, ka_o, va_o, qb_o, kb_o, vb_o, u_o, gs_o) = refs[:20]
    x = h_ref[...]
    if pending:
        x = _ffn_residual(x, modp_ref[0], y0_ref[...], y1_ref[...], info_ref[...])
        refs[20][...] = x
    m = mod_ref[0]
    hx = _rms(x, g_ref[...]) * (1.0 + m[1:2]) + m[0:1]
    hb = hx.astype(BF16)

    def seg(name, width):
        off = _SEG[name]
        return _dot(hb, w1_ref[:, off:off + width])

    ca, sa = ca_ref[...], sa_ref[...]
    cb, sb = cb_ref[...], sb_ref[...]

    qa = seg("qa", A_WIDTH)
    for t in range(A_WIDTH // LANES):
        sl = slice(t * LANES, (t + 1) * LANES)
        qa_o[:, sl] = _rope_tile(qa[:, sl], ca, sa, A_HEAD_DIM // 2).astype(BF16)
    ka_o[...] = _rope_tile(seg("ka", A_KV_WIDTH), ca, sa, A_HEAD_DIM // 2).astype(BF16)
    va_o[...] = seg("va", A_KV_WIDTH).astype(BF16)

    cqn = _rms(seg("cq", Q_LORA), qng_ref[...]).astype(BF16)
    ckvn = _rms(seg("ckv", KV_LORA), kvng_ref[...]).astype(BF16)
    qb = _dot(cqn, wuq_ref[...])
    kb = _dot(ckvn, wuk_ref[...])
    kr = _rope_tile(seg("kr", LANES), cb, sb, B_ROPE // 2)
    qscale = (B_NOPE + B_ROPE) ** -0.5 * math.log2(math.e)
    for t in range(B_HEADS):
        sl = slice(t * LANES, (t + 1) * LANES)
        qb_o[:, sl] = (_rope_tile(qb[:, sl], cb, sb, B_ROPE // 2) * qscale).astype(BF16)
        kb_o[:, sl] = (kb[:, sl] + kr).astype(BF16)
    vb_o[...] = _dot(ckvn, wuv_ref[...]).T.astype(BF16)

    u_o[...] = seg("u", C_WIDTH).astype(BF16)
    ngate = gs_o.shape[1]
    for t in range(ngate // 512):
        off = W1_GATES_OFF + t * 512
        gs_o[:, t * 512:(t + 1) * 512] = jax.nn.sigmoid(_dot(hb, w1_ref[:, off:off + 512])).astype(BF16)


def _inproj(h_all, mod, norm_g, lw, tables, tile_mod, tile_rope, pending=None):
    t_all, d = h_all.shape
    tm = ROW_TILE
    nt = t_all // tm
    w1 = lw["w1"]
    ngate = N_BRANCH * d
    row = lambda i, *_: (i, 0)
    const = lambda i, *_: (0, 0)
    rope = lambda i, tmod, trope: (trope[i], 0)
    modrow = lambda i, tmod, trope: (tmod[i], 0, 0)
    widths = (A_WIDTH, A_KV_WIDTH, A_KV_WIDTH, B_QK_WIDTH, B_QK_WIDTH, B_WIDTH, C_WIDTH, ngate)
    out_specs = [pl.BlockSpec((B_WIDTH, tm), lambda i, *_: (0, i)) if k == VB_OUT
                 else pl.BlockSpec((tm, w), row) for k, w in enumerate(widths)]
    out_shape = [jax.ShapeDtypeStruct((B_WIDTH, t_all) if k == VB_OUT else (t_all, w), BF16)
                 for k, w in enumerate(widths)]
    lead_specs = [pl.BlockSpec((tm, d), row)]
    lead_args = [h_all]
    if pending is not None:
        y0, y1, info, mod_prev = pending
        lead_specs += [pl.BlockSpec((tm, d), row), pl.BlockSpec((tm, d), row),
                       pl.BlockSpec((tm, ROUTER_LANES), row), pl.BlockSpec((1, 8, d), modrow)]
        lead_args += [y0, y1, info, mod_prev]
        out_specs.append(pl.BlockSpec((tm, d), row))
        out_shape.append(jax.ShapeDtypeStruct((t_all, d), F32))
    grid_spec = pltpu.PrefetchScalarGridSpec(
        num_scalar_prefetch=2,
        grid=(nt,),
        in_specs=lead_specs + [
                  pl.BlockSpec((1, 8, d), modrow),
                  pl.BlockSpec((1, d), const),
                  pl.BlockSpec(w1.shape, const),
                  pl.BlockSpec((1, Q_LORA), const),
                  pl.BlockSpec((1, KV_LORA), const),
                  pl.BlockSpec(lw["wuq"].shape, const),
                  pl.BlockSpec(lw["wuk"].shape, const),
                  pl.BlockSpec(lw["wuv"].shape, const),
                  pl.BlockSpec((tm, LANES), rope),
                  pl.BlockSpec((tm, LANES), rope),
                  pl.BlockSpec((tm, LANES), rope),
                  pl.BlockSpec((tm, LANES), rope)],
        out_specs=out_specs,
    )
    return pl.pallas_call(
        functools.partial(_inproj_kernel, pending=pending is not None),
        grid_spec=grid_spec,
        out_shape=out_shape,
        compiler_params=_cparams(("arbitrary",)),
        name="inproj",
    )(tile_mod, tile_rope, *lead_args, mod, norm_g, w1, lw["qng"], lw["kvng"], lw["wuq"], lw["wuk"],
      lw["wuv"], tables["ca"], tables["sa"], tables["cb"], tables["sb"])


def _gqa_kernel(sink_ref, q_ref, *refs, local, ntile):
    if local:
        kp_ref, kc_ref, kn_ref, vp_ref, vc_ref, vn_ref, kx_ref, vx_ref, o_ref = refs
    else:
        kx_ref, vx_ref, _, o_ref = refs
    tq = q_ref.shape[0]
    hd = A_HEAD_DIM
    sub = WINDOW if local else tq
    nsub = tq // sub
    rows = A_GROUP * sub
    lane = lax.broadcasted_iota(jnp.int32, (1, LANES), 1)
    head_row = lax.broadcasted_iota(jnp.int32, (rows, 1), 0) // sub
    sinks = []
    for g in range(A_KV_HEADS):
        col = jnp.zeros((rows, 1), F32)
        for j in range(A_GROUP):
            col = jnp.where(head_row == j, sink_ref[g * A_GROUP + j], col)
        sinks.append(col)
    kx, vx = kx_ref[...], vx_ref[...]
    if local:
        j_tile = pl.program_id(1)
        kloc = jnp.concatenate([kp_ref[...], kc_ref[...], kn_ref[...]], axis=0)
        vloc = jnp.concatenate([vp_ref[...], vc_ref[...], vn_ref[...]], axis=0)
        r = lax.broadcasted_iota(jnp.int32, (sub, 3 * WINDOW), 0)
        c = lax.broadcasted_iota(jnp.int32, (sub, 3 * WINDOW), 1)
        band = jnp.abs(c - WINDOW - r) <= WINDOW
    for s in range(nsub):
        q = jnp.concatenate([q_ref[s * sub:(s + 1) * sub, j * LANES:(j + 1) * LANES]
                             for j in range(A_GROUP)], axis=0)
        if local:
            kcat = jnp.concatenate([kloc[s * sub:s * sub + 3 * WINDOW], kx], axis=0)
            vcat = jnp.concatenate([vloc[s * sub:s * sub + 3 * WINDOW], vx], axis=0)
            ok = band
            if s == 0:
                ok = jnp.logical_and(ok, jnp.logical_or(c >= WINDOW, j_tile > 0))
            if s == nsub - 1:
                ok = jnp.logical_and(ok, jnp.logical_or(c < 2 * WINDOW, j_tile < ntile - 1))
            bias = jnp.where(ok, 0.0, NEG_INF)
            bias = jnp.concatenate([bias] * A_GROUP, axis=0)
        else:
            kcat, vcat = kx, vx
        outs = []
        for g in range(A_KV_HEADS):
            in_g = jnp.logical_and(lane >= g * hd, lane < (g + 1) * hd)
            kg = jnp.where(in_g, kcat, jnp.zeros_like(kcat))
            sc = _dot_nt(q, kg)
            if local:
                sc = jnp.concatenate([sc[:, :3 * WINDOW] + bias, sc[:, 3 * WINDOW:]], axis=1)
            mx = jnp.maximum(jnp.max(sc, axis=-1, keepdims=True), sinks[g])
            e = jnp.exp(sc - mx)
            denom = jnp.sum(e, axis=-1, keepdims=True) + jnp.exp(sinks[g] - mx)
            outs.append(_dot(e.astype(BF16), vcat) / denom)
        o = jnp.where(lane < hd, outs[0], outs[1]).astype(BF16)
        for j in range(A_GROUP):
            o_ref[s * sub:(s + 1) * sub, j * LANES:(j + 1) * LANES] = o[j * sub:(j + 1) * sub]


def _window_attn(qa, ka, va, sink, batch, n, lc, out_rows):
    t = batch * n
    tq = min(WIN_Q_TILE, n)
    ntile = n // tq
    sub = tq // WINDOW
    nblk = n // WINDOW
    cblk = t // lc
    qmap = lambda b, j: (b * ntile + j, 0)
    prev = lambda b, j: (b * nblk + jnp.maximum(j * sub - 1, 0), 0)
    nxt = lambda b, j: (b * nblk + jnp.minimum((j + 1) * sub, nblk - 1), 0)
    ctx = lambda b, j: (cblk + b, 0)
    kvw = A_KV_WIDTH
    return pl.pallas_call(
        functools.partial(_gqa_kernel, local=True, ntile=ntile),
        grid=(batch, ntile),
        in_specs=[pl.BlockSpec(memory_space=pltpu.SMEM),
                  pl.BlockSpec((tq, A_WIDTH), qmap),
                  pl.BlockSpec((WINDOW, kvw), prev), pl.BlockSpec((tq, kvw), qmap),
                  pl.BlockSpec((WINDOW, kvw), nxt),
                  pl.BlockSpec((WINDOW, kvw), prev), pl.BlockSpec((tq, kvw), qmap),
                  pl.BlockSpec((WINDOW, kvw), nxt),
                  pl.BlockSpec((lc, kvw), ctx), pl.BlockSpec((lc, kvw), ctx)],
        out_specs=pl.BlockSpec((tq, A_WIDTH), qmap),
        out_shape=jax.ShapeDtypeStruct((out_rows, A_WIDTH), BF16),
        compiler_params=_cparams(("arbitrary", "arbitrary")),
        name="window_attn",
    )(sink, qa, ka, ka, ka, va, va, va, ka, va)


def _ctx_gqa_attn(qa, ka, va, sink, oa, batch, n, lc):
    cblk = batch * n // lc
    cmap = lambda b: (cblk + b, 0)
    return pl.pallas_call(
        functools.partial(_gqa_kernel, local=False, ntile=0),
        grid=(batch,),
        in_specs=[pl.BlockSpec(memory_space=pltpu.SMEM),
                  pl.BlockSpec((lc, A_WIDTH), cmap),
                  pl.BlockSpec((lc, A_KV_WIDTH), cmap), pl.BlockSpec((lc, A_KV_WIDTH), cmap),
                  pl.BlockSpec(memory_space=pl.ANY)],
        out_specs=pl.BlockSpec((lc, A_WIDTH), cmap),
        out_shape=jax.ShapeDtypeStruct(oa.shape, BF16),
        input_output_aliases={4: 0},
        compiler_params=_cparams(("arbitrary",)),
        name="ctx_gqa_attn",
    )(sink, qa, ka, va, oa)


def _mla_kernel(q_ref, kc_ref, vc_ref, *refs, with_latent):
    if with_latent:
        kx_ref, vx_ref, o_ref, s_ref = refs
    else:
        _, o_ref, s_ref = refs
    tq = q_ref.shape[0]
    lc = kc_ref.shape[0]
    chunks = [(kc_ref, vc_ref, 0, lc, 0)]
    if with_latent:
        n = kx_ref.shape[0]
        for r0 in range(0, n, MLA_KEY_CHUNK):
            chunks.append((kx_ref, vx_ref, r0, min(MLA_KEY_CHUNK, n - r0), lc + r0))

    def fold(x, op):
        while x.shape[0] > 8:
            half = x.shape[0] // 2
            x = op(x[:half], x[half:])
        return x

    outs = []
    for h in range(B_HEADS):
        buf = h % 2
        ql = slice(h * B_HEAD_PAD, (h + 1) * B_HEAD_PAD)
        vl = slice(h * B_VDIM, (h + 1) * B_VDIM)
        qh = q_ref[:, ql]
        m8 = None
        for k_ref, _, r0, nr, c0 in chunks:
            st = _dot_nt(k_ref[r0:r0 + nr, ql], qh)
            s_ref[buf, c0:c0 + nr, :] = st
            part = fold(st, jnp.maximum)
            m8 = part if m8 is None else jnp.maximum(m8, part)
        mx = jnp.max(m8, axis=0, keepdims=True)
        l8 = None
        acc = None
        for _, vt_ref, r0, nr, c0 in chunks:
            et = jnp.exp2(s_ref[buf, c0:c0 + nr, :] - mx)
            part = fold(et, jnp.add)
            l8 = part if l8 is None else l8 + part
            pv = _dot(vt_ref[vl, r0:r0 + nr], et.astype(BF16))
            acc = pv if acc is None else acc + pv
        outs.append(acc / jnp.sum(l8, axis=0, keepdims=True))
    o_ref[...] = jnp.concatenate(outs, axis=0).T.astype(BF16)


def _mla_attn(qb, kb, vb, batch, n, lc, out_rows):
    t = batch * n
    tq = min(MLA_Q_TILE, n)
    nq = n // tq
    cblk = t // lc
    qmap = lambda b, j: (b * nq + j, 0)
    ctx = lambda b, j: (cblk + b, 0)
    lat = lambda b, j: (b, 0)
    ctx_t = lambda b, j: (0, cblk + b)
    lat_t = lambda b, j: (0, b)
    return pl.pallas_call(
        functools.partial(_mla_kernel, with_latent=True),
        grid=(batch, nq),
        in_specs=[pl.BlockSpec((tq, B_QK_WIDTH), qmap),
                  pl.BlockSpec((lc, B_QK_WIDTH), ctx), pl.BlockSpec((B_WIDTH, lc), ctx_t),
                  pl.BlockSpec((n, B_QK_WIDTH), lat), pl.BlockSpec((B_WIDTH, n), lat_t)],
        out_specs=pl.BlockSpec((tq, B_WIDTH), qmap),
        out_shape=jax.ShapeDtypeStruct((out_rows, B_WIDTH), BF16),
        scratch_shapes=[pltpu.VMEM((2, lc + n, tq), F32)],
        compiler_params=_cparams(("arbitrary", "arbitrary")),
        name="mla_attn",
    )(qb, kb, vb, kb, vb)


def _mla_ctx_attn(qb, kb, vb, ob, batch, n, lc):
    cblk = batch * n // lc
    cmap = lambda b: (cblk + b, 0)
    return pl.pallas_call(
        functools.partial(_mla_kernel, with_latent=False),
        grid=(batch,),
        in_specs=[pl.BlockSpec((lc, B_QK_WIDTH), cmap),
                  pl.BlockSpec((lc, B_QK_WIDTH), cmap),
                  pl.BlockSpec((B_WIDTH, lc), lambda b: (0, cblk + b)),
                  pl.BlockSpec(memory_space=pl.ANY)],
        out_specs=pl.BlockSpec((lc, B_WIDTH), cmap),
        out_shape=jax.ShapeDtypeStruct(ob.shape, BF16),
        input_output_aliases={3: 0},
        scratch_shapes=[pltpu.VMEM((2, lc, lc), F32)],
        compiler_params=_cparams(("arbitrary",)),
        name="mla_ctx_attn",
    )(qb, kb, vb, ob)


def _route_tile(logits, run):
    tm = logits.shape[0]
    lane_i = lax.broadcasted_iota(jnp.int32, logits.shape, 1)
    lane = lane_i.astype(F32)
    nolane = float(LANES)

    def first_argmax(v):
        mx = jnp.max(v, axis=-1, keepdims=True)
        return mx, jnp.min(jnp.where(v == mx, lane, nolane), axis=-1, keepdims=True)

    gl = jnp.where(lane_i < N_GROUPS, logits, NEG_INF)
    gmax, g_sel = first_argmax(gl)
    g_p = 1.0 / jnp.sum(jnp.exp(gl - gmax), axis=-1, keepdims=True)
    lo = N_GROUPS + EXPERTS_PER_GROUP * g_sel
    el = jnp.where(jnp.logical_and(lane >= lo, lane < lo + EXPERTS_PER_GROUP), logits, NEG_INF)
    v1, i1 = first_argmax(el)
    sel1 = lane == i1
    v2, i2 = first_argmax(jnp.where(sel1, NEG_INF, el))
    sel2 = lane == i2
    t2 = jnp.exp(v2 - v1)
    gate1 = g_p / (1.0 + t2)
    gate2 = gate1 * t2

    onehot = jnp.where(jnp.logical_or(sel1, sel2), 1.0, 0.0)
    r_i = lax.broadcasted_iota(jnp.int32, (tm, tm), 0)
    c_i = lax.broadcasted_iota(jnp.int32, (tm, tm), 1)
    lower = jnp.where(c_i < r_i, 1.0, 0.0).astype(BF16)
    before = _dot(lower, onehot.astype(BF16)) + run
    rank1 = jnp.sum(jnp.where(sel1, before, 0.0), axis=-1, keepdims=True)
    rank2 = jnp.sum(jnp.where(sel2, before, 0.0), axis=-1, keepdims=True)
    info = jnp.zeros_like(logits)
    for k, val in enumerate((i1 - N_GROUPS, i2 - N_GROUPS, rank1, rank2, gate1, gate2)):
        info = jnp.where(lane_i == k, val, info)
    return info, run + jnp.sum(onehot, axis=0, keepdims=True)


def _merge_kernel(tmod_ref, tpos_ref, tlen_ref, h_ref, mod_ref, g_ref, oa_ref, ob_ref, u_ref, up_ref,
                  un_ref, gs_ref, wpool_ref, pscale_ref, wa_ref, wb_ref, wc_ref, wo_ref, wr_ref, br_ref,
                  hn_o, fx_o, info_o, cnt_o, run_ref):
    del tmod_ref
    i = pl.program_id(0)

    @pl.when(i == 0)
    def _():
        run_ref[...] = jnp.zeros_like(run_ref)

    pos0 = tpos_ref[i]
    seq_len = tlen_ref[i]
    tm = u_ref.shape[0]
    d = h_ref.shape[1]

    u = u_ref[...]
    kdim = tm + LANES
    zpad = jnp.zeros((LANES - 2 * POOL_HALO, u.shape[1]), BF16)
    ucat = jnp.concatenate([up_ref[...], u, un_ref[...], zpad], axis=0)
    uf = u.astype(F32)
    r_i = lax.broadcasted_iota(jnp.int32, (tm, kdim), 0)
    c_i = lax.broadcasted_iota(jnp.int32, (tm, kdim), 1)
    rel = c_i - POOL_HALO - r_i
    kpos = pos0 + c_i - POOL_HALO
    valid = jnp.logical_and(kpos >= 0, kpos < seq_len)
    tpos = pos0 + lax.broadcasted_iota(jnp.int32, (tm, 1), 0)
    oc_parts = []
    for gi, w in enumerate(POOL_WINDOWS):
        rad = w // 2
        sl = slice(gi * C_GROUP_DIM, (gi + 1) * C_GROUP_DIM)
        band = jnp.where(jnp.logical_and(jnp.abs(rel) <= rad, valid), 1.0, 0.0).astype(BF16)
        win_sum = _dot(band, ucat[:, sl])
        cnt = (jnp.minimum(tpos + rad + 1, seq_len) - jnp.maximum(tpos - rad, 0)).astype(F32)
        pooled = (win_sum / cnt - uf[:, sl]).astype(BF16)
        oc_parts.append(_dot(pooled, wpool_ref[gi]))
    oc = (jnp.concatenate(oc_parts, axis=1) * pscale_ref[...]).astype(BF16)

    y = gs_ref[:, 0:d].astype(F32) * _dot(oa_ref[...], wa_ref[...])
    y = y + gs_ref[:, d:2 * d].astype(F32) * _dot(ob_ref[...], wb_ref[...])
    y = y + gs_ref[:, 2 * d:3 * d].astype(F32) * _dot(oc, wc_ref[...])
    mix = _dot(y.astype(BF16), wo_ref[...])

    m = mod_ref[0]
    hn = h_ref[...] + m[2:3] * mix
    hn_o[...] = hn
    fx = _rms(hn, g_ref[...]) * (1.0 + m[4:5]) + m[3:4]
    fx_hi = fx.astype(BF16)
    fx_lo = (fx - fx_hi.astype(F32)).astype(BF16)
    fx_o[...] = fx_hi
    both = _dot(fx_hi, wr_ref[...])
    logits = (both[:, :ROUTER_LANES] + _dot(fx_lo, wr_ref[:, :ROUTER_LANES])
              + both[:, ROUTER_LANES:]) + br_ref[...]
    info, run = _route_tile(logits, run_ref[...])
    info_o[...] = info
    run_ref[...] = run
    cnt_o[...] = run


def _merge(h_all, mod, norm_g, oa, ob, u, gs, lw, rows, tile_mod, tile_pos, tile_len):
    t_all, d = h_all.shape
    tm = ROW_TILE
    nt = rows // tm
    hb = tm // POOL_HALO
    nhalo = u.shape[0] // POOL_HALO
    row = lambda i, *_: (i, 0)
    const = lambda i, *_: (0, 0)
    const3 = lambda i, *_: (0, 0, 0)
    grid_spec = pltpu.PrefetchScalarGridSpec(
        num_scalar_prefetch=3,
        grid=(nt,),
        in_specs=[pl.BlockSpec((tm, d), row),
                  pl.BlockSpec((1, 8, d), lambda i, tmod, *_: (tmod[i], 0, 0)),
                  pl.BlockSpec((1, d), const),
                  pl.BlockSpec((tm, A_WIDTH), row),
                  pl.BlockSpec((tm, B_WIDTH), row),
                  pl.BlockSpec((tm, C_WIDTH), row),
                  pl.BlockSpec((POOL_HALO, C_WIDTH), lambda i, *_: (jnp.maximum(i * hb - 1, 0), 0)),
                  pl.BlockSpec((POOL_HALO, C_WIDTH), lambda i, *_: (jnp.minimum((i + 1) * hb, nhalo - 1), 0)),
                  pl.BlockSpec((tm, N_BRANCH * d), row),
                  pl.BlockSpec(lw["wpool"].shape, const3),
                  pl.BlockSpec((1, C_WIDTH), const),
                  pl.BlockSpec(lw["wa"].shape, const),
                  pl.BlockSpec(lw["wb"].shape, const),
                  pl.BlockSpec(lw["wc"].shape, const),
                  pl.BlockSpec(lw["wo"].shape, const),
                  pl.BlockSpec(lw["wr"].shape, const),
                  pl.BlockSpec((1, ROUTER_LANES), const)],
        out_specs=[pl.BlockSpec((tm, d), row), pl.BlockSpec((tm, d), row),
                   pl.BlockSpec((tm, ROUTER_LANES), row), pl.BlockSpec((1, ROUTER_LANES), const)],
        scratch_shapes=[pltpu.VMEM((1, ROUTER_LANES), F32)],
    )
    return pl.pallas_call(
        _merge_kernel,
        grid_spec=grid_spec,
        out_shape=[jax.ShapeDtypeStruct((rows, d), F32), jax.ShapeDtypeStruct((rows, d), BF16),
                   jax.ShapeDtypeStruct((rows, ROUTER_LANES), F32),
                   jax.ShapeDtypeStruct((1, ROUTER_LANES), F32)],
        compiler_params=_cparams(("arbitrary",)),
        name="merge",
    )(tile_mod, tile_pos, tile_len, h_all, mod, norm_g, oa, ob, u, u, u, gs, lw["wpool"],
      lw["pscale"], lw["wa"], lw["wb"], lw["wc"], lw["wo"], lw["wr"], lw["br"])


def _moe_kernel(blk_e_ref, nused_ref, x_ref, wgu_ref, wdn_ref, o_ref):
    del blk_e_ref
    i = pl.program_id(0)

    @pl.when(i < nused_ref[0])
    def _():
        gu = _dot(x_ref[...], wgu_ref[0, 0].astype(BF16))
        gt, up = gu[:, :EXPERT_FF], gu[:, EXPERT_FF:]
        act = (gt * jax.nn.sigmoid(gt) * up).astype(BF16)
        o_ref[...] = _dot(act, wdn_ref[0, 0].astype(BF16)).astype(o_ref.dtype)

    @pl.when(i >= nused_ref[0])
    def _():
        o_ref[...] = jnp.zeros_like(o_ref)


def _moe(xg, blk_e, n_used, w_gu, w_dn, layer):
    p, d = xg.shape
    nb = p // MOE_BLOCK
    grid_spec = pltpu.PrefetchScalarGridSpec(
        num_scalar_prefetch=2,
        grid=(nb,),
        in_specs=[pl.BlockSpec((MOE_BLOCK, d), lambda i, *_: (i, 0)),
                  pl.BlockSpec((1, 1, d, 2 * EXPERT_FF), lambda i, be, nu: (layer, be[i], 0, 0)),
                  pl.BlockSpec((1, 1, EXPERT_FF, d), lambda i, be, nu: (layer, be[i], 0, 0))],
        out_specs=pl.BlockSpec((MOE_BLOCK, d), lambda i, *_: (i, 0)),
    )
    return pl.pallas_call(
        _moe_kernel,
        grid_spec=grid_spec,
        out_shape=jax.ShapeDtypeStruct((p, d), BF16),
        compiler_params=_cparams(("arbitrary",)),
        name="moe_experts",
    )(blk_e, n_used, xg, w_gu, w_dn)


def _resid_kernel(tmod_ref, h_ref, mod_ref, y0_ref, y1_ref, info_ref, fg_ref, o_ref, *, final):
    del tmod_ref
    hn = _ffn_residual(h_ref[...], mod_ref[0], y0_ref[...], y1_ref[...], info_ref[...])
    if final:
        hn = _rms(hn, fg_ref[...])
    o_ref[...] = hn


def _resid(h, mod, y0, y1, info, final_g, tile_mod, final):
    rows, d = h.shape
    tm = ROW_TILE
    row = lambda i, *_: (i, 0)
    grid_spec = pltpu.PrefetchScalarGridSpec(
        num_scalar_prefetch=1,
        grid=(rows // tm,),
        in_specs=[pl.BlockSpec((tm, d), row),
                  pl.BlockSpec((1, 8, d), lambda i, tmod: (tmod[i], 0, 0)),
                  pl.BlockSpec((tm, d), row), pl.BlockSpec((tm, d), row),
                  pl.BlockSpec((tm, ROUTER_LANES), row),
                  pl.BlockSpec((1, d), lambda i, *_: (0, 0))],
        out_specs=pl.BlockSpec((tm, d), row),
    )
    return pl.pallas_call(
        functools.partial(_resid_kernel, final=final),
        grid_spec=grid_spec,
        out_shape=jax.ShapeDtypeStruct((rows, d), F32),
        compiler_params=_cparams(("arbitrary",)),
        name="ffn_residual",
    )(tile_mod, h, mod, y0, y1, info, final_g)


def _deinterleave(n):
    return np.concatenate([np.arange(0, n, 2), np.arange(1, n, 2)])


def _rope_tables(n, tm):
    rows = n // GRID_W
    row = np.repeat(np.arange(rows), GRID_W).astype(np.float32)
    col = np.tile(np.arange(GRID_W), rows).astype(np.float32)

    def cs(rot_dim):
        axis_dim = rot_dim // 2
        inv = jnp.asarray(ROPE_BASE, F32) ** (-jnp.arange(0, axis_dim, 2, dtype=F32) / axis_dim)
        ang = jnp.concatenate([jnp.asarray(row)[:, None] * inv, jnp.asarray(col)[:, None] * inv], axis=-1)
        return jnp.cos(ang), jnp.sin(ang)

    cos_a, sin_a = cs(A_HEAD_DIM)
    ca = jnp.tile(jnp.concatenate([cos_a, cos_a], axis=1), (1, LANES // A_HEAD_DIM))
    sa = jnp.tile(jnp.concatenate([-sin_a, sin_a], axis=1), (1, LANES // A_HEAD_DIM))
    cos_b, sin_b = cs(B_ROPE)
    ones = jnp.ones((n, B_NOPE), F32)
    tail = LANES - B_NOPE - B_ROPE
    cb = jnp.concatenate([ones, cos_b, cos_b, jnp.ones((n, tail), F32)], axis=1)
    sb = jnp.concatenate([0 * ones, -sin_b, sin_b, jnp.zeros((n, tail), F32)], axis=1)
    ident_c = jnp.ones((tm, LANES), F32)
    ident_s = jnp.zeros((tm, LANES), F32)
    return {"ca": jnp.concatenate([ca, ident_c]), "sa": jnp.concatenate([sa, ident_s]),
            "cb": jnp.concatenate([cb, ident_c]), "sb": jnp.concatenate([sb, ident_s])}


def _layer_weights(l, w_in, q_norm_g, w_uq, kv_norm_g, w_ukv, w_pool, pool_scale, w_br_a, w_br_b,
                   w_br_c, w_out, w_rg, b_rg, w_re, b_re):
    d = w_in.shape[1]
    wi = w_in[l]
    splits = np.cumsum([A_WIDTH, A_KV_WIDTH, A_KV_WIDTH, Q_LORA, KV_LORA, B_ROPE, C_WIDTH])
    qa, ka, va, cq, ckv, kr, u, gates = jnp.split(wi, splits, axis=1)
    pa = _deinterleave(A_HEAD_DIM)
    ho = np.arange(A_HEADS).reshape(A_KV_HEADS, A_GROUP).T.reshape(-1)
    qa = qa.reshape(d, A_HEADS, A_HEAD_DIM)[:, ho][:, :, pa].reshape(d, A_WIDTH) * (A_HEAD_DIM ** -0.5)
    wa = w_br_a[l].reshape(A_HEADS, A_HEAD_DIM, -1)[ho].reshape(A_WIDTH, -1)
    ka = ka.reshape(d, A_KV_HEADS, A_HEAD_DIM)[:, :, pa].reshape(d, A_KV_WIDTH)
    pb = _deinterleave(B_ROPE)
    tail = LANES - B_NOPE - B_ROPE
    kr128 = jnp.concatenate([jnp.zeros((d, B_NOPE), F32), kr[:, pb], jnp.zeros((d, tail), F32)], axis=1)
    w1 = jnp.concatenate([qa, ka, va, cq, ckv, kr128, u, gates], axis=1).astype(BF16)

    uq = w_uq[l].reshape(Q_LORA, B_HEADS, B_NOPE + B_ROPE)
    uq = jnp.concatenate([uq[:, :, :B_NOPE], uq[:, :, B_NOPE:][:, :, pb],
                          jnp.zeros((Q_LORA, B_HEADS, tail), F32)], axis=2)
    ukv = w_ukv[l].reshape(KV_LORA, B_HEADS, B_NOPE + B_VDIM)
    uk = jnp.concatenate([ukv[:, :, :B_NOPE], jnp.zeros((KV_LORA, B_HEADS, LANES - B_NOPE), F32)], axis=2)
    uv = ukv[:, :, B_NOPE:]

    wr = jnp.concatenate([w_rg[l], w_re[l], jnp.zeros((d, ROUTER_LANES - N_GROUPS - N_EXPERTS), F32)], axis=1)
    wr_hi = wr.astype(BF16)
    wr_lo = (wr - wr_hi.astype(F32)).astype(BF16)
    br = jnp.concatenate([b_rg[l], b_re[l], jnp.zeros((ROUTER_LANES - N_GROUPS - N_EXPERTS,), F32)])
    return {
        "w1": w1,
        "qng": q_norm_g[l].reshape(1, Q_LORA), "kvng": kv_norm_g[l].reshape(1, KV_LORA),
        "wuq": uq.reshape(Q_LORA, B_QK_WIDTH).astype(BF16),
        "wuk": uk.reshape(KV_LORA, B_QK_WIDTH).astype(BF16),
        "wuv": uv.reshape(KV_LORA, B_WIDTH).astype(BF16),
        "wpool": w_pool[l].astype(BF16), "pscale": pool_scale[l].reshape(1, C_WIDTH),
        "wa": wa.astype(BF16), "wb": w_br_b[l].astype(BF16), "wc": w_br_c[l].astype(BF16),
        "wo": w_out[l].astype(BF16),
        "wr": jnp.concatenate([wr_hi, wr_lo], axis=1), "br": br.reshape(1, ROUTER_LANES),
    }


def _block_layout(info, cnt):
    t = info.shape[0]
    a = t * TOP_K
    expert = info[:, 0:TOP_K].astype(jnp.int32)
    rank = info[:, TOP_K:2 * TOP_K].astype(jnp.int32)
    counts = cnt[0, N_GROUPS:N_GROUPS + N_EXPERTS].astype(jnp.int32)
    padded = (counts + MOE_BLOCK - 1) // MOE_BLOCK * MOE_BLOCK
    pad_end = jnp.cumsum(padded)
    pad_start = pad_end - padded
    eids = jnp.arange(N_EXPERTS, dtype=jnp.int32)
    base = jnp.sum(jnp.where(expert[:, :, None] == eids, pad_start, 0), axis=-1)
    slot = base + rank
    n_blocks = -(-(a + N_EXPERTS * (MOE_BLOCK - 1)) // MOE_BLOCK)
    blk_start = jnp.arange(n_blocks, dtype=jnp.int32) * MOE_BLOCK
    blk_e = jnp.minimum(jnp.sum((pad_end[None, :] <= blk_start[:, None]).astype(jnp.int32), axis=1),
                        N_EXPERTS - 1)
    n_used = (pad_end[-1] // MOE_BLOCK).reshape(1)

    by_slot = jnp.argsort(slot.reshape(a)).astype(jnp.int32) // TOP_K
    first = jnp.cumsum(counts) - counts
    blk_sel = blk_e[:, None] == eids[None, :]
    blk_off = blk_start - jnp.sum(jnp.where(blk_sel, pad_start, 0), axis=1)
    blk_src = jnp.sum(jnp.where(blk_sel, first, 0), axis=1) + blk_off
    blk_cnt = jnp.sum(jnp.where(blk_sel, counts, 0), axis=1) - blk_off
    within = jnp.arange(MOE_BLOCK, dtype=jnp.int32)[None, :]
    src = jnp.minimum(blk_src[:, None] + within, a - 1)
    filler = (blk_start[:, None] + within) % t
    tok_of_slot = jnp.where(within < blk_cnt[:, None], by_slot.at[src].get(mode="promise_in_bounds"),
                            filler)
    return slot, tok_of_slot.reshape(n_blocks * MOE_BLOCK), blk_e, n_used


def kernel(x, c, ctx, c_ctx, w_mod, b_mod, norm_mix_g, norm_ffn_g, w_in, sink, q_norm_g, w_uq, kv_norm_g,
           w_ukv, w_pool, pool_scale, w_br_a, w_br_b, w_br_c, w_out, w_rg, b_rg, w_re, b_re, w_gu, w_dn,
           final_g):
    batch, n, d = x.shape
    lc = ctx.shape[1]
    depth = w_mod.shape[0]
    tm = ROW_TILE
    assert n % tm == 0 and lc % tm == 0 and n % WINDOW == 0 and (batch * n) % lc == 0
    t = batch * n
    tc = batch * lc
    nt_lat, nt_ctx = t // tm, tc // tm
    per_b, per_c = n // tm, lc // tm

    lat_i = np.arange(nt_lat)
    ctx_i = np.arange(nt_ctx)
    tile_mod = jnp.asarray(np.concatenate([lat_i // per_b, np.full(nt_ctx, batch)]), jnp.int32)
    tile_rope = jnp.asarray(np.concatenate([lat_i % per_b, np.full(nt_ctx, per_b)]), jnp.int32)
    tile_pos = jnp.asarray(np.concatenate([(lat_i % per_b) * tm, (ctx_i % per_c) * tm]), jnp.int32)
    tile_len = jnp.asarray(np.concatenate([np.full(nt_lat, n), np.full(nt_ctx, lc)]), jnp.int32)
    tables = _rope_tables(n, tm)

    mod_rows = 16
    cvec = jnp.concatenate([c, c_ctx[None, :], jnp.zeros((mod_rows - batch - 1, d), F32)], axis=0)
    h_all = jnp.concatenate([x.reshape(t, d), ctx.reshape(tc, d)], axis=0)

    mod_all = _modvec(cvec, w_mod, b_mod).reshape(depth, mod_rows, 6, d)
    mod_all = jnp.concatenate([mod_all, jnp.zeros((depth, mod_rows, 2, d), F32)], axis=2)

    out = None
    pending = None
    for l in range(depth):
        last = l == depth - 1
        lw = _layer_weights(l, w_in, q_norm_g, w_uq, kv_norm_g, w_ukv, w_pool, pool_scale, w_br_a,
                            w_br_b, w_br_c, w_out, w_rg, b_rg, w_re, b_re)
        mod = mod_all[l]

        proj = _inproj(h_all, mod, norm_mix_g[l].reshape(1, d), lw, tables, tile_mod, tile_rope,
                       pending)
        qa, ka, va, qb, kb, vb, u, gs = proj[:8]
        if pending is not None:
            h_all = proj[8]
        rows = t if last else t + tc
        oa = _window_attn(qa, ka, va, sink[l], batch, n, lc, rows)
        ob = _mla_attn(qb, kb, vb, batch, n, lc, rows)
        if not last:
            oa = _ctx_gqa_attn(qa, ka, va, sink[l], oa, batch, n, lc)
            ob = _mla_ctx_attn(qb, kb, vb, ob, batch, n, lc)
        hn, fx, info, cnt = _merge(h_all, mod, norm_ffn_g[l].reshape(1, d), oa, ob, u, gs, lw, rows,
                                   tile_mod, tile_pos, tile_len)

        slot, tok_of_slot, blk_e, n_used = _block_layout(info, cnt)
        xg = fx.at[tok_of_slot].get(mode="promise_in_bounds")
        yb = _moe(xg, blk_e, n_used, w_gu, w_dn, l)
        y0 = yb.at[slot[:, 0]].get(mode="promise_in_bounds")
        y1 = yb.at[slot[:, 1]].get(mode="promise_in_bounds")
        if last:
            out = _resid(hn, mod, y0, y1, info, final_g.reshape(1, d), tile_mod, True)
        else:
            h_all, pending = hn, (y0, y1, info, mod)
    return out.reshape(batch, n, d)
```

```python
import functools
import math

import jax
import jax.numpy as jnp
import numpy as np
from jax import lax
from jax.experimental import pallas as pl
from jax.experimental.pallas import tpu as pltpu

GRID_W = 64
ROPE_BASE = 10000.0
EPS = 1e-6
NEG_INF = -1e30

A_HEADS = 8
A_KV_HEADS = 2
A_GROUP = A_HEADS // A_KV_HEADS
A_HEAD_DIM = 64
A_WIDTH = A_HEADS * A_HEAD_DIM
A_KV_WIDTH = A_KV_HEADS * A_HEAD_DIM
WINDOW = 128

B_HEADS = 8
B_NOPE = 64
B_ROPE = 32
B_VDIM = 64
B_WIDTH = B_HEADS * B_VDIM
Q_LORA = 256
KV_LORA = 256

POOL_WINDOWS = (2, 4, 8, 16)
C_GROUPS = 4
C_GROUP_DIM = 128
C_WIDTH = C_GROUPS * C_GROUP_DIM

N_BRANCH = 3
N_GROUPS = 4
EXPERTS_PER_GROUP = 8
N_EXPERTS = N_GROUPS * EXPERTS_PER_GROUP
TOP_K = 2
EXPERT_FF = 256

LANES = 128
BF16_SUBLANES = 16
VMEM_LIMIT_BYTES = 56 * 1024 * 1024

B_HEAD_PAD = LANES
B_QK_WIDTH = B_HEADS * B_HEAD_PAD
POOL_HALO = BF16_SUBLANES
ROW_TILE = 256
MLA_Q_TILE = 256
MLA_KEY_CHUNK = 256
WIN_Q_TILE = 512
MOE_BLOCK = 512
ROUTER_LANES = LANES

_SEG = {}
_off = 0
for _name, _w in (("qa", A_WIDTH), ("ka", A_KV_WIDTH), ("va", A_KV_WIDTH), ("cq", Q_LORA),
                  ("ckv", KV_LORA), ("kr", LANES), ("u", C_WIDTH), ("gates", None)):
    _SEG[_name] = _off
    if _w is not None:
        _off += _w
W1_GATES_OFF = _SEG["gates"]
assert A_KV_HEADS * A_HEAD_DIM == LANES
VB_OUT = 5

F32 = jnp.float32
BF16 = jnp.bfloat16


def _dot(a, b):
    return jnp.dot(a, b, preferred_element_type=F32)


def _dot_nt(a, b):
    return lax.dot_general(a, b, (((1,), (1,)), ((), ())), preferred_element_type=F32)


def _cparams(sem):
    return pltpu.CompilerParams(dimension_semantics=sem, vmem_limit_bytes=VMEM_LIMIT_BYTES)


def _rms(x, g):
    return x * lax.rsqrt(jnp.mean(x * x, axis=-1, keepdims=True) + EPS) * g


def _modvec_kernel(c_ref, w_ref, b_ref, o_ref):
    c = c_ref[...]
    a = (c * jax.nn.sigmoid(c)).astype(BF16)
    o_ref[0] = _dot(a, w_ref[0].astype(BF16)) + b_ref[0]


def _modvec(cvec, w_mod, b_mod):
    rows, d = cvec.shape
    depth, _, n = w_mod.shape
    tn = d
    return pl.pallas_call(
        _modvec_kernel,
        grid=(depth, n // tn),
        in_specs=[pl.BlockSpec((rows, d), lambda l, j: (0, 0)),
                  pl.BlockSpec((1, d, tn), lambda l, j: (l, 0, j)),
                  pl.BlockSpec((1, 1, tn), lambda l, j: (l, 0, j))],
        out_specs=pl.BlockSpec((1, rows, tn), lambda l, j: (l, 0, j)),
        out_shape=jax.ShapeDtypeStruct((depth, rows, n), F32),
        compiler_params=_cparams(("arbitrary", "arbitrary")),
        name="modvec",
    )(cvec, w_mod, b_mod.reshape(depth, 1, n))


def _rope_tile(x, cos, sin, half):
    lane = lax.broadcasted_iota(jnp.int32, x.shape, 1)
    fwd = pltpu.roll(x, LANES - half, 1)
    bwd = pltpu.roll(x, half, 1)
    partner = jnp.where((lane % (2 * half)) < half, fwd, bwd)
    return x * cos + partner * sin


def _ffn_residual(h, mod_rows, y0, y1, info):
    ffn = y0.astype(F32) * info[:, 4:5] + y1.astype(F32) * info[:, 5:6]
    return h + mod_rows[5:6] * ffn


def _inproj_kernel(tmod_ref, trope_ref, *refs, pending):
    del tmod_ref, trope_ref
    if pending:
        h_ref, y0_ref, y1_ref, info_ref, modp_ref = refs[:5]
        refs = refs[5:]
    else:
        h_ref = refs[0]
        refs = refs[1:]
    (mod_ref, g_ref, w1_ref, qng_ref, kvng_ref, wuq_ref, wuk_ref, wuv_ref, ca_ref, sa_ref, cb_ref,
     sb_ref, qa_o, ka_o, va_o, qb_o, kb_o, vb_o, u_o, gs_o) = refs[:20]
    x = h_ref[...]
    if pending:
        x = _ffn_residual(x, modp_ref[0], y0_ref[...], y1_ref[...], info_ref[...])
        refs[20][...] = x
    m = mod_ref[0]
    hx = _rms(x, g_ref[...]) * (1.0 + m[1:2]) + m[0:1]
    hb = hx.astype(BF16)

    def seg(name, width):
        off = _SEG[name]
        return _dot(hb, w1_ref[:, off:off + width])

    ca, sa = ca_ref[...], sa_ref[...]
    cb, sb = cb_ref[...], sb_ref[...]

    qa = seg("qa", A_WIDTH)
    for t in range(A_WIDTH // LANES):
        sl = slice(t * LANES, (t + 1) * LANES)
        qa_o[:, sl] = _rope_tile(qa[:, sl], ca, sa, A_HEAD_DIM // 2).astype(BF16)
    ka_o[...] = _rope_tile(seg("ka", A_KV_WIDTH), ca, sa, A_HEAD_DIM // 2).astype(BF16)
    va_o[...] = seg("va", A_KV_WIDTH).astype(BF16)

    cqn = _rms(seg("cq", Q_LORA), qng_ref[...]).astype(BF16)
    ckvn = _rms(seg("ckv", KV_LORA), kvng_ref[...]).astype(BF16)
    qb = _dot(cqn, wuq_ref[...])
    kb = _dot(ckvn, wuk_ref[...])
    kr = _rope_tile(seg("kr", LANES), cb, sb, B_ROPE // 2)
    qscale = (B_NOPE + B_ROPE) ** -0.5 * math.log2(math.e)
    for t in range(B_HEADS):
        sl = slice(t * LANES, (t + 1) * LANES)
        qb_o[:, sl] = (_rope_tile(qb[:, sl], cb, sb, B_ROPE // 2) * qscale).astype(BF16)
        kb_o[:, sl] = (kb[:, sl] + kr).astype(BF16)
    vb_o[...] = _dot(ckvn, wuv_ref[...]).T.astype(BF16)

    u_o[...] = seg("u", C_WIDTH).astype(BF16)
    ngate = gs_o.shape[1]
    for t in range(ngate // 512):
        off = W1_GATES_OFF + t * 512
        gs_o[:, t * 512:(t + 1) * 512] = jax.nn.sigmoid(_dot(hb, w1_ref[:, off:off + 512])).astype(BF16)


def _inproj(h_all, mod, norm_g, lw, tables, tile_mod, tile_rope, pending=None):
    t_all, d = h_all.shape
    tm = ROW_TILE
    nt = t_all // tm
    w1 = lw["w1"]
    ngate = N_BRANCH * d
    row = lambda i, *_: (i, 0)
    const = lambda i, *_: (0, 0)
    rope = lambda i, tmod, trope: (trope[i], 0)
    modrow = lambda i, tmod, trope: (tmod[i], 0, 0)
    widths = (A_WIDTH, A_KV_WIDTH, A_KV_WIDTH, B_QK_WIDTH, B_QK_WIDTH, B_WIDTH, C_WIDTH, ngate)
    out_specs = [pl.BlockSpec((B_WIDTH, tm), lambda i, *_: (0, i)) if k == VB_OUT
                 else pl.BlockSpec((tm, w), row) for k, w in enumerate(widths)]
    out_shape = [jax.ShapeDtypeStruct((B_WIDTH, t_all) if k == VB_OUT else (t_all, w), BF16)
                 for k, w in enumerate(widths)]
    lead_specs = [pl.BlockSpec((tm, d), row)]
    lead_args = [h_all]
    if pending is not None:
        y0, y1, info, mod_prev = pending
        lead_specs += [pl.BlockSpec((tm, d), row), pl.BlockSpec((tm, d), row),
                       pl.BlockSpec((tm, ROUTER_LANES), row), pl.BlockSpec((1, 8, d), modrow)]
        lead_args += [y0, y1, info, mod_prev]
        out_specs.append(pl.BlockSpec((tm, d), row))
        out_shape.append(jax.ShapeDtypeStruct((t_all, d), F32))
    grid_spec = pltpu.PrefetchScalarGridSpec(
        num_scalar_prefetch=2,
        grid=(nt,),
        in_specs=lead_specs + [
                  pl.BlockSpec((1, 8, d), modrow),
                  pl.BlockSpec((1, d), const),
                  pl.BlockSpec(w1.shape, const),
                  pl.BlockSpec((1, Q_LORA), const),
                  pl.BlockSpec((1, KV_LORA), const),
                  pl.BlockSpec(lw["wuq"].shape, const),
                  pl.BlockSpec(lw["wuk"].shape, const),
                  pl.BlockSpec(lw["wuv"].shape, const),
                  pl.BlockSpec((tm, LANES), rope),
                  pl.BlockSpec((tm, LANES), rope),
                  pl.BlockSpec((tm, LANES), rope),
                  pl.BlockSpec((tm, LANES), rope)],
        out_specs=out_specs,
    )
    return pl.pallas_call(
        functools.partial(_inproj_kernel, pending=pending is not None),
        grid_spec=grid_spec,
        out_shape=out_shape,
        compiler_params=_cparams(("arbitrary",)),
        name="inproj",
    )(tile_mod, tile_rope, *lead_args, mod, norm_g, w1, lw["qng"], lw["kvng"], lw["wuq"], lw["wuk"],
      lw["wuv"], tables["ca"], tables["sa"], tables["cb"], tables["sb"])


def _gqa_kernel(sink_ref, q_ref, *refs, local, ntile):
    if local:
        kp_ref, kc_ref, kn_ref, vp_ref, vc_ref, vn_ref, kx_ref, vx_ref, o_ref = refs
    else:
        kx_ref, vx_ref, _, o_ref = refs
    tq = q_ref.shape[0]
    hd = A_HEAD_DIM
    sub = WINDOW if local else tq
    nsub = tq // sub
    rows = A_GROUP * sub
    lane = lax.broadcasted_iota(jnp.int32, (1, LANES), 1)
    head_row = lax.broadcasted_iota(jnp.int32, (rows, 1), 0) // sub
    sinks = []
    for g in range(A_KV_HEADS):
        col = jnp.zeros((rows, 1), F32)
        for j in range(A_GROUP):
            col = jnp.where(head_row == j, sink_ref[g * A_GROUP + j], col)
        sinks.append(col)
    kx, vx = kx_ref[...], vx_ref[...]
    if local:
        j_tile = pl.program_id(1)
        kloc = jnp.concatenate([kp_ref[...], kc_ref[...], kn_ref[...]], axis=0)
        vloc = jnp.concatenate([vp_ref[...], vc_ref[...], vn_ref[...]], axis=0)
        r = lax.broadcasted_iota(jnp.int32, (sub, 3 * WINDOW), 0)
        c = lax.broadcasted_iota(jnp.int32, (sub, 3 * WINDOW), 1)
        band = jnp.abs(c - WINDOW - r) <= WINDOW
    def block_scores(s):
        q = jnp.concatenate([q_ref[s * sub:(s + 1) * sub, j * LANES:(j + 1) * LANES]
                             for j in range(A_GROUP)], axis=0)
        if local:
            kcat = jnp.concatenate([kloc[s * sub:s * sub + 3 * WINDOW], kx], axis=0)
            vcat = jnp.concatenate([vloc[s * sub:s * sub + 3 * WINDOW], vx], axis=0)
            ok = band
            if s == 0:
                ok = jnp.logical_and(ok, jnp.logical_or(c >= WINDOW, j_tile > 0))
            if s == nsub - 1:
                ok = jnp.logical_and(ok, jnp.logical_or(c < 2 * WINDOW, j_tile < ntile - 1))
            bias = jnp.where(ok, 0.0, NEG_INF)
            bias = jnp.concatenate([bias] * A_GROUP, axis=0)
        else:
            kcat, vcat = kx, vx
        scs = []
        for g in range(A_KV_HEADS):
            in_g = jnp.logical_and(lane >= g * hd, lane < (g + 1) * hd)
            kg = jnp.where(in_g, kcat, jnp.zeros_like(kcat))
            sc = _dot_nt(q, kg)
            if local:
                sc = jnp.concatenate([sc[:, :3 * WINDOW] + bias, sc[:, 3 * WINDOW:]], axis=1)
            scs.append(sc)
        return scs, vcat

    def block_output(s, scs, vcat):
        outs = []
        for g in range(A_KV_HEADS):
            mx = jnp.maximum(jnp.max(scs[g], axis=-1, keepdims=True), sinks[g])
            e = jnp.exp(scs[g] - mx)
            denom = jnp.sum(e, axis=-1, keepdims=True) + jnp.exp(sinks[g] - mx)
            outs.append(_dot(e.astype(BF16), vcat) / denom)
        o = jnp.where(lane < hd, outs[0], outs[1]).astype(BF16)
        for j in range(A_GROUP):
            o_ref[s * sub:(s + 1) * sub, j * LANES:(j + 1) * LANES] = o[j * sub:(j + 1) * sub]

    pending = block_scores(0)
    for s in range(1, nsub):
        nxt = block_scores(s)
        block_output(s - 1, *pending)
        pending = nxt
    block_output(nsub - 1, *pending)


def _window_attn(qa, ka, va, sink, batch, n, lc, out_rows):
    t = batch * n
    tq = min(WIN_Q_TILE, n)
    ntile = n // tq
    sub = tq // WINDOW
    nblk = n // WINDOW
    cblk = t // lc
    qmap = lambda b, j: (b * ntile + j, 0)
    prev = lambda b, j: (b * nblk + jnp.maximum(j * sub - 1, 0), 0)
    nxt = lambda b, j: (b * nblk + jnp.minimum((j + 1) * sub, nblk - 1), 0)
    ctx = lambda b, j: (cblk + b, 0)
    kvw = A_KV_WIDTH
    return pl.pallas_call(
        functools.partial(_gqa_kernel, local=True, ntile=ntile),
        grid=(batch, ntile),
        in_specs=[pl.BlockSpec(memory_space=pltpu.SMEM),
                  pl.BlockSpec((tq, A_WIDTH), qmap),
                  pl.BlockSpec((WINDOW, kvw), prev), pl.BlockSpec((tq, kvw), qmap),
                  pl.BlockSpec((WINDOW, kvw), nxt),
                  pl.BlockSpec((WINDOW, kvw), prev), pl.BlockSpec((tq, kvw), qmap),
                  pl.BlockSpec((WINDOW, kvw), nxt),
                  pl.BlockSpec((lc, kvw), ctx), pl.BlockSpec((lc, kvw), ctx)],
        out_specs=pl.BlockSpec((tq, A_WIDTH), qmap),
        out_shape=jax.ShapeDtypeStruct((out_rows, A_WIDTH), BF16),
        compiler_params=_cparams(("arbitrary", "arbitrary")),
        name="window_attn",
    )(sink, qa, ka, ka, ka, va, va, va, ka, va)


def _ctx_gqa_attn(qa, ka, va, sink, oa, batch, n, lc):
    cblk = batch * n // lc
    cmap = lambda b: (cblk + b, 0)
    return pl.pallas_call(
        functools.partial(_gqa_kernel, local=False, ntile=0),
        grid=(batch,),
        in_specs=[pl.BlockSpec(memory_space=pltpu.SMEM),
                  pl.BlockSpec((lc, A_WIDTH), cmap),
                  pl.BlockSpec((lc, A_KV_WIDTH), cmap), pl.BlockSpec((lc, A_KV_WIDTH), cmap),
                  pl.BlockSpec(memory_space=pl.ANY)],
        out_specs=pl.BlockSpec((lc, A_WIDTH), cmap),
        out_shape=jax.ShapeDtypeStruct(oa.shape, BF16),
        input_output_aliases={4: 0},
        compiler_params=_cparams(("arbitrary",)),
        name="ctx_gqa_attn",
    )(sink, qa, ka, va, oa)


def _mla_kernel(q_ref, kc_ref, vc_ref, *refs, with_latent):
    if with_latent:
        kx_ref, vx_ref, o_ref, s_ref = refs
    else:
        _, o_ref, s_ref = refs
    tq = q_ref.shape[0]
    lc = kc_ref.shape[0]
    chunks = [(kc_ref, vc_ref, 0, lc, 0)]
    if with_latent:
        n = kx_ref.shape[0]
        for r0 in range(0, n, MLA_KEY_CHUNK):
            chunks.append((kx_ref, vx_ref, r0, min(MLA_KEY_CHUNK, n - r0), lc + r0))

    def fold(x, op):
        while x.shape[0] > 8:
            half = x.shape[0] // 2
            x = op(x[:half], x[half:])
        return x

    def scores(h):
        ql = slice(h * B_HEAD_PAD, (h + 1) * B_HEAD_PAD)
        qh = q_ref[:, ql]
        m8 = None
        for k_ref, _, r0, nr, c0 in chunks:
            st = _dot_nt(k_ref[r0:r0 + nr, ql], qh)
            s_ref[h % 2, c0:c0 + nr, :] = st
            part = fold(st, jnp.maximum)
            m8 = part if m8 is None else jnp.maximum(m8, part)
        return jnp.max(m8, axis=0, keepdims=True)

    def values(h, mx):
        vl = slice(h * B_VDIM, (h + 1) * B_VDIM)
        l8 = None
        acc = None
        for _, vt_ref, r0, nr, c0 in chunks:
            et = jnp.exp2(s_ref[h % 2, c0:c0 + nr, :] - mx)
            part = fold(et, jnp.add)
            l8 = part if l8 is None else l8 + part
            pv = _dot(vt_ref[vl, r0:r0 + nr], et.astype(BF16))
            acc = pv if acc is None else acc + pv
        return acc / jnp.sum(l8, axis=0, keepdims=True)

    outs = []
    mx_prev = scores(0)
    for h in range(1, B_HEADS):
        mx = scores(h)
        outs.append(values(h - 1, mx_prev))
        mx_prev = mx
    outs.append(values(B_HEADS - 1, mx_prev))
    o_ref[...] = jnp.concatenate(outs, axis=0).T.astype(BF16)


def _mla_attn(qb, kb, vb, batch, n, lc, out_rows):
    t = batch * n
    tq = min(MLA_Q_TILE, n)
    nq = n // tq
    cblk = t // lc
    qmap = lambda b, j: (b * nq + j, 0)
    ctx = lambda b, j: (cblk + b, 0)
    lat = lambda b, j: (b, 0)
    ctx_t = lambda b, j: (0, cblk + b)
    lat_t = lambda b, j: (0, b)
    return pl.pallas_call(
        functools.partial(_mla_kernel, with_latent=True),
        grid=(batch, nq),
        in_specs=[pl.BlockSpec((tq, B_QK_WIDTH), qmap),
                  pl.BlockSpec((lc, B_QK_WIDTH), ctx), pl.BlockSpec((B_WIDTH, lc), ctx_t),
                  pl.BlockSpec((n, B_QK_WIDTH), lat), pl.BlockSpec((B_WIDTH, n), lat_t)],
        out_specs=pl.BlockSpec((tq, B_WIDTH), qmap),
        out_shape=jax.ShapeDtypeStruct((out_rows, B_WIDTH), BF16),
        scratch_shapes=[pltpu.VMEM((2, lc + n, tq), F32)],
        compiler_params=_cparams(("arbitrary", "arbitrary")),
        name="mla_attn",
    )(qb, kb, vb, kb, vb)


def _mla_ctx_attn(qb, kb, vb, ob, batch, n, lc):
    cblk = batch * n // lc
    cmap = lambda b: (cblk + b, 0)
    return pl.pallas_call(
        functools.partial(_mla_kernel, with_latent=False),
        grid=(batch,),
        in_specs=[pl.BlockSpec((lc, B_QK_WIDTH), cmap),
                  pl.BlockSpec((lc, B_QK_WIDTH), cmap),
                  pl.BlockSpec((B_WIDTH, lc), lambda b: (0, cblk + b)),
                  pl.BlockSpec(memory_space=pl.ANY)],
        out_specs=pl.BlockSpec((lc, B_WIDTH), cmap),
        out_shape=jax.ShapeDtypeStruct(ob.shape, BF16),
        input_output_aliases={3: 0},
        scratch_shapes=[pltpu.VMEM((2, lc, lc), F32)],
        compiler_params=_cparams(("arbitrary",)),
        name="mla_ctx_attn",
    )(qb, kb, vb, ob)


def _route_tile(logits, run):
    tm = logits.shape[0]
    lane_i = lax.broadcasted_iota(jnp.int32, logits.shape, 1)
    lane = lane_i.astype(F32)
    nolane = float(LANES)

    def first_argmax(v):
        mx = jnp.max(v, axis=-1, keepdims=True)
        return mx, jnp.min(jnp.where(v == mx, lane, nolane), axis=-1, keepdims=True)

    gl = jnp.where(lane_i < N_GROUPS, logits, NEG_INF)
    gmax, g_sel = first_argmax(gl)
    g_p = 1.0 / jnp.sum(jnp.exp(gl - gmax), axis=-1, keepdims=True)
    lo = N_GROUPS + EXPERTS_PER_GROUP * g_sel
    el = jnp.where(jnp.logical_and(lane >= lo, lane < lo + EXPERTS_PER_GROUP), logits, NEG_INF)
    v1, i1 = first_argmax(el)
    sel1 = lane == i1
    v2, i2 = first_argmax(jnp.where(sel1, NEG_INF, el))
    sel2 = lane == i2
    t2 = jnp.exp(v2 - v1)
    gate1 = g_p / (1.0 + t2)
    gate2 = gate1 * t2

    onehot = jnp.where(jnp.logical_or(sel1, sel2), 1.0, 0.0)
    r_i = lax.broadcasted_iota(jnp.int32, (tm, tm), 0)
    c_i = lax.broadcasted_iota(jnp.int32, (tm, tm), 1)
    lower = jnp.where(c_i < r_i, 1.0, 0.0).astype(BF16)
    before = _dot(lower, onehot.astype(BF16)) + run
    rank1 = jnp.sum(jnp.where(sel1, before, 0.0), axis=-1, keepdims=True)
    rank2 = jnp.sum(jnp.where(sel2, before, 0.0), axis=-1, keepdims=True)
    info = jnp.zeros_like(logits)
    for k, val in enumerate((i1 - N_GROUPS, i2 - N_GROUPS, rank1, rank2, gate1, gate2)):
        info = jnp.where(lane_i == k, val, info)
    return info, run + jnp.sum(onehot, axis=0, keepdims=True)


def _merge_kernel(tmod_ref, tpos_ref, tlen_ref, h_ref, mod_ref, g_ref, oa_ref, ob_ref, u_ref, up_ref,
                  un_ref, gs_ref, wpool_ref, pscale_ref, wa_ref, wb_ref, wc_ref, wo_ref, wr_ref, br_ref,
                  hn_o, fx_o, info_o, cnt_o, run_ref):
    del tmod_ref
    i = pl.program_id(0)

    @pl.when(i == 0)
    def _():
        run_ref[...] = jnp.zeros_like(run_ref)

    pos0 = tpos_ref[i]
    seq_len = tlen_ref[i]
    tm = u_ref.shape[0]
    d = h_ref.shape[1]

    u = u_ref[...]
    kdim = tm + LANES
    zpad = jnp.zeros((LANES - 2 * POOL_HALO, u.shape[1]), BF16)
    ucat = jnp.concatenate([up_ref[...], u, un_ref[...], zpad], axis=0)
    uf = u.astype(F32)
    r_i = lax.broadcasted_iota(jnp.int32, (tm, kdim), 0)
    c_i = lax.broadcasted_iota(jnp.int32, (tm, kdim), 1)
    rel = c_i - POOL_HALO - r_i
    kpos = pos0 + c_i - POOL_HALO
    valid = jnp.logical_and(kpos >= 0, kpos < seq_len)
    tpos = pos0 + lax.broadcasted_iota(jnp.int32, (tm, 1), 0)
    win_sums = []
    for gi, w in enumerate(POOL_WINDOWS):
        rad = w // 2
        sl = slice(gi * C_GROUP_DIM, (gi + 1) * C_GROUP_DIM)
        band = jnp.where(jnp.logical_and(jnp.abs(rel) <= rad, valid), 1.0, 0.0).astype(BF16)
        win_sums.append(_dot(band, ucat[:, sl]))
    ya = _dot(oa_ref[...], wa_ref[...])
    oc_parts = []
    for gi, w in enumerate(POOL_WINDOWS):
        rad = w // 2
        sl = slice(gi * C_GROUP_DIM, (gi + 1) * C_GROUP_DIM)
        cnt = (jnp.minimum(tpos + rad + 1, seq_len) - jnp.maximum(tpos - rad, 0)).astype(F32)
        pooled = (win_sums[gi] / cnt - uf[:, sl]).astype(BF16)
        oc_parts.append(_dot(pooled, wpool_ref[gi]))
    yb = _dot(ob_ref[...], wb_ref[...])
    oc = (jnp.concatenate(oc_parts, axis=1) * pscale_ref[...]).astype(BF16)

    y = gs_ref[:, 0:d].astype(F32) * ya
    y = y + gs_ref[:, d:2 * d].astype(F32) * yb
    y = y + gs_ref[:, 2 * d:3 * d].astype(F32) * _dot(oc, wc_ref[...])
    mix = _dot(y.astype(BF16), wo_ref[...])

    m = mod_ref[0]
    hn = h_ref[...] + m[2:3] * mix
    hn_o[...] = hn
    fx = _rms(hn, g_ref[...]) * (1.0 + m[4:5]) + m[3:4]
    fx_hi = fx.astype(BF16)
    fx_lo = (fx - fx_hi.astype(F32)).astype(BF16)
    fx_o[...] = fx_hi
    both = _dot(fx_hi, wr_ref[...])
    logits = (both[:, :ROUTER_LANES] + _dot(fx_lo, wr_ref[:, :ROUTER_LANES])
              + both[:, ROUTER_LANES:]) + br_ref[...]
    info, run = _route_tile(logits, run_ref[...])
    info_o[...] = info
    run_ref[...] = run
    cnt_o[...] = run


def _merge(h_all, mod, norm_g, oa, ob, u, gs, lw, rows, tile_mod, tile_pos, tile_len):
    t_all, d = h_all.shape
    tm = ROW_TILE
    nt = rows // tm
    hb = tm // POOL_HALO
    nhalo = u.shape[0] // POOL_HALO
    row = lambda i, *_: (i, 0)
    const = lambda i, *_: (0, 0)
    const3 = lambda i, *_: (0, 0, 0)
    grid_spec = pltpu.PrefetchScalarGridSpec(
        num_scalar_prefetch=3,
        grid=(nt,),
        in_specs=[pl.BlockSpec((tm, d), row),
                  pl.BlockSpec((1, 8, d), lambda i, tmod, *_: (tmod[i], 0, 0)),
                  pl.BlockSpec((1, d), const),
                  pl.BlockSpec((tm, A_WIDTH), row),
                  pl.BlockSpec((tm, B_WIDTH), row),
                  pl.BlockSpec((tm, C_WIDTH), row),
                  pl.BlockSpec((POOL_HALO, C_WIDTH), lambda i, *_: (jnp.maximum(i * hb - 1, 0), 0)),
                  pl.BlockSpec((POOL_HALO, C_WIDTH), lambda i, *_: (jnp.minimum((i + 1) * hb, nhalo - 1), 0)),
                  pl.BlockSpec((tm, N_BRANCH * d), row),
                  pl.BlockSpec(lw["wpool"].shape, const3),
                  pl.BlockSpec((1, C_WIDTH), const),
                  pl.BlockSpec(lw["wa"].shape, const),
                  pl.BlockSpec(lw["wb"].shape, const),
                  pl.BlockSpec(lw["wc"].shape, const),
                  pl.BlockSpec(lw["wo"].shape, const),
                  pl.BlockSpec(lw["wr"].shape, const),
                  pl.BlockSpec((1, ROUTER_LANES), const)],
        out_specs=[pl.BlockSpec((tm, d), row), pl.BlockSpec((tm, d), row),
                   pl.BlockSpec((tm, ROUTER_LANES), row), pl.BlockSpec((1, ROUTER_LANES), const)],
        scratch_shapes=[pltpu.VMEM((1, ROUTER_LANES), F32)],
    )
    return pl.pallas_call(
        _merge_kernel,
        grid_spec=grid_spec,
        out_shape=[jax.ShapeDtypeStruct((rows, d), F32), jax.ShapeDtypeStruct((rows, d), BF16),
                   jax.ShapeDtypeStruct((rows, ROUTER_LANES), F32),
                   jax.ShapeDtypeStruct((1, ROUTER_LANES), F32)],
        compiler_params=_cparams(("arbitrary",)),
        name="merge",
    )(tile_mod, tile_pos, tile_len, h_all, mod, norm_g, oa, ob, u, u, u, gs, lw["wpool"],
      lw["pscale"], lw["wa"], lw["wb"], lw["wc"], lw["wo"], lw["wr"], lw["br"])


def _moe_kernel(blk_e_ref, nused_ref, x_ref, wgu_ref, wdn_ref, o_ref):
    del blk_e_ref
    i = pl.program_id(0)

    @pl.when(i < nused_ref[0])
    def _():
        gu = _dot(x_ref[...], wgu_ref[0, 0].astype(BF16))
        gt, up = gu[:, :EXPERT_FF], gu[:, EXPERT_FF:]
        act = (gt * jax.nn.sigmoid(gt) * up).astype(BF16)
        o_ref[...] = _dot(act, wdn_ref[0, 0].astype(BF16)).astype(o_ref.dtype)

    @pl.when(i >= nused_ref[0])
    def _():
        o_ref[...] = jnp.zeros_like(o_ref)


def _moe(xg, blk_e, n_used, w_gu, w_dn, layer):
    p, d = xg.shape
    nb = p // MOE_BLOCK
    grid_spec = pltpu.PrefetchScalarGridSpec(
        num_scalar_prefetch=2,
        grid=(nb,),
        in_specs=[pl.BlockSpec((MOE_BLOCK, d), lambda i, *_: (i, 0)),
                  pl.BlockSpec((1, 1, d, 2 * EXPERT_FF), lambda i, be, nu: (layer, be[i], 0, 0)),
                  pl.BlockSpec((1, 1, EXPERT_FF, d), lambda i, be, nu: (layer, be[i], 0, 0))],
        out_specs=pl.BlockSpec((MOE_BLOCK, d), lambda i, *_: (i, 0)),
    )
    return pl.pallas_call(
        _moe_kernel,
        grid_spec=grid_spec,
        out_shape=jax.ShapeDtypeStruct((p, d), BF16),
        compiler_params=_cparams(("arbitrary",)),
        name="moe_experts",
    )(blk_e, n_used, xg, w_gu, w_dn)


def _resid_kernel(tmod_ref, h_ref, mod_ref, y0_ref, y1_ref, info_ref, fg_ref, o_ref, *, final):
    del tmod_ref
    hn = _ffn_residual(h_ref[...], mod_ref[0], y0_ref[...], y1_ref[...], info_ref[...])
    if final:
        hn = _rms(hn, fg_ref[...])
    o_ref[...] = hn


def _resid(h, mod, y0, y1, info, final_g, tile_mod, final):
    rows, d = h.shape
    tm = ROW_TILE
    row = lambda i, *_: (i, 0)
    grid_spec = pltpu.PrefetchScalarGridSpec(
        num_scalar_prefetch=1,
        grid=(rows // tm,),
        in_specs=[pl.BlockSpec((tm, d), row),
                  pl.BlockSpec((1, 8, d), lambda i, tmod: (tmod[i], 0, 0)),
                  pl.BlockSpec((tm, d), row), pl.BlockSpec((tm, d), row),
                  pl.BlockSpec((tm, ROUTER_LANES), row),
                  pl.BlockSpec((1, d), lambda i, *_: (0, 0))],
        out_specs=pl.BlockSpec((tm, d), row),
    )
    return pl.pallas_call(
        functools.partial(_resid_kernel, final=final),
        grid_spec=grid_spec,
        out_shape=jax.ShapeDtypeStruct((rows, d), F32),
        compiler_params=_cparams(("arbitrary",)),
        name="ffn_residual",
    )(tile_mod, h, mod, y0, y1, info, final_g)


def _deinterleave(n):
    return np.concatenate([np.arange(0, n, 2), np.arange(1, n, 2)])


def _rope_tables(n, tm):
    rows = n // GRID_W
    row = np.repeat(np.arange(rows), GRID_W).astype(np.float32)
    col = np.tile(np.arange(GRID_W), rows).astype(np.float32)

    def cs(rot_dim):
        axis_dim = rot_dim // 2
        inv = jnp.asarray(ROPE_BASE, F32) ** (-jnp.arange(0, axis_dim, 2, dtype=F32) / axis_dim)
        ang = jnp.concatenate([jnp.asarray(row)[:, None] * inv, jnp.asarray(col)[:, None] * inv], axis=-1)
        return jnp.cos(ang), jnp.sin(ang)

    cos_a, sin_a = cs(A_HEAD_DIM)
    ca = jnp.tile(jnp.concatenate([cos_a, cos_a], axis=1), (1, LANES // A_HEAD_DIM))
    sa = jnp.tile(jnp.concatenate([-sin_a, sin_a], axis=1), (1, LANES // A_HEAD_DIM))
    cos_b, sin_b = cs(B_ROPE)
    ones = jnp.ones((n, B_NOPE), F32)
    tail = LANES - B_NOPE - B_ROPE
    cb = jnp.concatenate([ones, cos_b, cos_b, jnp.ones((n, tail), F32)], axis=1)
    sb = jnp.concatenate([0 * ones, -sin_b, sin_b, jnp.zeros((n, tail), F32)], axis=1)
    ident_c = jnp.ones((tm, LANES), F32)
    ident_s = jnp.zeros((tm, LANES), F32)
    return {"ca": jnp.concatenate([ca, ident_c]), "sa": jnp.concatenate([sa, ident_s]),
            "cb": jnp.concatenate([cb, ident_c]), "sb": jnp.concatenate([sb, ident_s])}


def _layer_weights(l, w_in, q_norm_g, w_uq, kv_norm_g, w_ukv, w_pool, pool_scale, w_br_a, w_br_b,
                   w_br_c, w_out, w_rg, b_rg, w_re, b_re):
    d = w_in.shape[1]
    wi = w_in[l]
    splits = np.cumsum([A_WIDTH, A_KV_WIDTH, A_KV_WIDTH, Q_LORA, KV_LORA, B_ROPE, C_WIDTH])
    qa, ka, va, cq, ckv, kr, u, gates = jnp.split(wi, splits, axis=1)
    pa = _deinterleave(A_HEAD_DIM)
    ho = np.arange(A_HEADS).reshape(A_KV_HEADS, A_GROUP).T.reshape(-1)
    qa = qa.reshape(d, A_HEADS, A_HEAD_DIM)[:, ho][:, :, pa].reshape(d, A_WIDTH) * (A_HEAD_DIM ** -0.5)
    wa = w_br_a[l].reshape(A_HEADS, A_HEAD_DIM, -1)[ho].reshape(A_WIDTH, -1)
    ka = ka.reshape(d, A_KV_HEADS, A_HEAD_DIM)[:, :, pa].reshape(d, A_KV_WIDTH)
    pb = _deinterleave(B_ROPE)
    tail = LANES - B_NOPE - B_ROPE
    kr128 = jnp.concatenate([jnp.zeros((d, B_NOPE), F32), kr[:, pb], jnp.zeros((d, tail), F32)], axis=1)
    w1 = jnp.concatenate([qa, ka, va, cq, ckv, kr128, u, gates], axis=1).astype(BF16)

    uq = w_uq[l].reshape(Q_LORA, B_HEADS, B_NOPE + B_ROPE)
    uq = jnp.concatenate([uq[:, :, :B_NOPE], uq[:, :, B_NOPE:][:, :, pb],
                          jnp.zeros((Q_LORA, B_HEADS, tail), F32)], axis=2)
    ukv = w_ukv[l].reshape(KV_LORA, B_HEADS, B_NOPE + B_VDIM)
    uk = jnp.concatenate([ukv[:, :, :B_NOPE], jnp.zeros((KV_LORA, B_HEADS, LANES - B_NOPE), F32)], axis=2)
    uv = ukv[:, :, B_NOPE:]

    wr = jnp.concatenate([w_rg[l], w_re[l], jnp.zeros((d, ROUTER_LANES - N_GROUPS - N_EXPERTS), F32)], axis=1)
    wr_hi = wr.astype(BF16)
    wr_lo = (wr - wr_hi.astype(F32)).astype(BF16)
    br = jnp.concatenate([b_rg[l], b_re[l], jnp.zeros((ROUTER_LANES - N_GROUPS - N_EXPERTS,), F32)])
    return {
        "w1": w1,
        "qng": q_norm_g[l].reshape(1, Q_LORA), "kvng": kv_norm_g[l].reshape(1, KV_LORA),
        "wuq": uq.reshape(Q_LORA, B_QK_WIDTH).astype(BF16),
        "wuk": uk.reshape(KV_LORA, B_QK_WIDTH).astype(BF16),
        "wuv": uv.reshape(KV_LORA, B_WIDTH).astype(BF16),
        "wpool": w_pool[l].astype(BF16), "pscale": pool_scale[l].reshape(1, C_WIDTH),
        "wa": wa.astype(BF16), "wb": w_br_b[l].astype(BF16), "wc": w_br_c[l].astype(BF16),
        "wo": w_out[l].astype(BF16),
        "wr": jnp.concatenate([wr_hi, wr_lo], axis=1), "br": br.reshape(1, ROUTER_LANES),
    }


def _block_layout(info, cnt):
    t = info.shape[0]
    a = t * TOP_K
    expert = info[:, 0:TOP_K].astype(jnp.int32)
    rank = info[:, TOP_K:2 * TOP_K].astype(jnp.int32)
    counts = cnt[0, N_GROUPS:N_GROUPS + N_EXPERTS].astype(jnp.int32)
    padded = (counts + MOE_BLOCK - 1) // MOE_BLOCK * MOE_BLOCK
    pad_end = jnp.cumsum(padded)
    pad_start = pad_end - padded
    eids = jnp.arange(N_EXPERTS, dtype=jnp.int32)
    base = jnp.sum(jnp.where(expert[:, :, None] == eids, pad_start, 0), axis=-1)
    slot = base + rank
    n_blocks = -(-(a + N_EXPERTS * (MOE_BLOCK - 1)) // MOE_BLOCK)
    blk_start = jnp.arange(n_blocks, dtype=jnp.int32) * MOE_BLOCK
    blk_e = jnp.minimum(jnp.sum((pad_end[None, :] <= blk_start[:, None]).astype(jnp.int32), axis=1),
                        N_EXPERTS - 1)
    n_used = (pad_end[-1] // MOE_BLOCK).reshape(1)

    by_slot = jnp.argsort(slot.reshape(a)).astype(jnp.int32) // TOP_K
    first = jnp.cumsum(counts) - counts
    blk_sel = blk_e[:, None] == eids[None, :]
    blk_off = blk_start - jnp.sum(jnp.where(blk_sel, pad_start, 0), axis=1)
    blk_src = jnp.sum(jnp.where(blk_sel, first, 0), axis=1) + blk_off
    blk_cnt = jnp.sum(jnp.where(blk_sel, counts, 0), axis=1) - blk_off
    within = jnp.arange(MOE_BLOCK, dtype=jnp.int32)[None, :]
    src = jnp.minimum(blk_src[:, None] + within, a - 1)
    filler = (blk_start[:, None] + within) % t
    tok_of_slot = jnp.where(within < blk_cnt[:, None], by_slot.at[src].get(mode="promise_in_bounds"),
                            filler)
    return slot, tok_of_slot.reshape(n_blocks * MOE_BLOCK), blk_e, n_used


def kernel(x, c, ctx, c_ctx, w_mod, b_mod, norm_mix_g, norm_ffn_g, w_in, sink, q_norm_g, w_uq, kv_norm_g,
           w_ukv, w_pool, pool_scale, w_br_a, w_br_b, w_br_c, w_out, w_rg, b_rg, w_re, b_re, w_gu, w_dn,
           final_g):
    batch, n, d = x.shape
    lc = ctx.shape[1]
    depth = w_mod.shape[0]
    tm = ROW_TILE
    assert n % tm == 0 and lc % tm == 0 and n % WINDOW == 0 and (batch * n) % lc == 0
    t = batch * n
    tc = batch * lc
    nt_lat, nt_ctx = t // tm, tc // tm
    per_b, per_c = n // tm, lc // tm

    lat_i = np.arange(nt_lat)
    ctx_i = np.arange(nt_ctx)
    tile_mod = jnp.asarray(np.concatenate([lat_i // per_b, np.full(nt_ctx, batch)]), jnp.int32)
    tile_rope = jnp.asarray(np.concatenate([lat_i % per_b, np.full(nt_ctx, per_b)]), jnp.int32)
    tile_pos = jnp.asarray(np.concatenate([(lat_i % per_b) * tm, (ctx_i % per_c) * tm]), jnp.int32)
    tile_len = jnp.asarray(np.concatenate([np.full(nt_lat, n), np.full(nt_ctx, lc)]), jnp.int32)
    tables = _rope_tables(n, tm)

    mod_rows = 16
    cvec = jnp.concatenate([c, c_ctx[None, :], jnp.zeros((mod_rows - batch - 1, d), F32)], axis=0)
    h_all = jnp.concatenate([x.reshape(t, d), ctx.reshape(tc, d)], axis=0)

    mod_all = _modvec(cvec, w_mod, b_mod).reshape(depth, mod_rows, 6, d)
    mod_all = jnp.concatenate([mod_all, jnp.zeros((depth, mod_rows, 2, d), F32)], axis=2)

    out = None
    pending = None
    for l in range(depth):
        last = l == depth - 1
        lw = _layer_weights(l, w_in, q_norm_g, w_uq, kv_norm_g, w_ukv, w_pool, pool_scale, w_br_a,
                            w_br_b, w_br_c, w_out, w_rg, b_rg, w_re, b_re)
        mod = mod_all[l]

        proj = _inproj(h_all, mod, norm_mix_g[l].reshape(1, d), lw, tables, tile_mod, tile_rope,
                       pending)
        qa, ka, va, qb, kb, vb, u, gs = proj[:8]
        if pending is not None:
            h_all = proj[8]
        rows = t if last else t + tc
        oa = _window_attn(qa, ka, va, sink[l], batch, n, lc, rows)
        ob = _mla_attn(qb, kb, vb, batch, n, lc, rows)
        if not last:
            oa = _ctx_gqa_attn(qa, ka, va, sink[l], oa, batch, n, lc)
            ob = _mla_ctx_attn(qb, kb, vb, ob, batch, n, lc)
        hn, fx, info, cnt = _merge(h_all, mod, norm_ffn_g[l].reshape(1, d), oa, ob, u, gs, lw, rows,
                                   tile_mod, tile_pos, tile_len)

        slot, tok_of_slot, blk_e, n_used = _block_layout(info, cnt)
        xg = fx.at[tok_of_slot].get(mode="promise_in_bounds")
        yb = _moe(xg, blk_e, n_used, w_gu, w_dn, l)
        y0 = yb.at[slot[:, 0]].get(mode="promise_in_bounds")
        y1 = yb.at[slot[:, 1]].get(mode="promise_in_bounds")
        if last:
            out = _resid(hn, mod, y0, y1, info, final_g.reshape(1, d), tile_mod, True)
        else:
            h_all, pending = hn, (y0, y1, info, mod)
    return out.reshape(batch, n, d)
```

```python
import functools
import math

import jax
import jax.numpy as jnp
import numpy as np
from jax import lax
from jax.experimental import pallas as pl
from jax.experimental.pallas import tpu as pltpu

GRID_W = 64
ROPE_BASE = 10000.0
EPS = 1e-6
NEG_INF = -1e30

A_HEADS = 8
A_KV_HEADS = 2
A_GROUP = A_HEADS // A_KV_HEADS
A_HEAD_DIM = 64
A_WIDTH = A_HEADS * A_HEAD_DIM
A_KV_WIDTH = A_KV_HEADS * A_HEAD_DIM
WINDOW = 128

B_HEADS = 8
B_NOPE = 64
B_ROPE = 32
B_VDIM = 64
B_WIDTH = B_HEADS * B_VDIM
Q_LORA = 256
KV_LORA = 256

POOL_WINDOWS = (2, 4, 8, 16)
C_GROUPS = 4
C_GROUP_DIM = 128
C_WIDTH = C_GROUPS * C_GROUP_DIM

N_BRANCH = 3
N_GROUPS = 4
EXPERTS_PER_GROUP = 8
N_EXPERTS = N_GROUPS * EXPERTS_PER_GROUP
TOP_K = 2
EXPERT_FF = 256

LANES = 128
BF16_SUBLANES = 16
VMEM_LIMIT_BYTES = 56 * 1024 * 1024

B_HEAD_PAD = LANES
B_QK_WIDTH = B_HEADS * B_HEAD_PAD
POOL_HALO = BF16_SUBLANES
ROW_TILE = 256
MLA_Q_TILE = 256
MLA_KEY_CHUNK = 256
WIN_Q_TILE = 512
MOE_BLOCK = 512
ROUTER_LANES = LANES

_SEG = {}
_off = 0
for _name, _w in (("qa", A_WIDTH), ("ka", A_KV_WIDTH), ("va", A_KV_WIDTH), ("cq", Q_LORA),
                  ("ckv", KV_LORA), ("kr", LANES), ("u", C_WIDTH), ("gates", None)):
    _SEG[_name] = _off
    if _w is not None:
        _off += _w
W1_GATES_OFF = _SEG["gates"]
assert A_KV_HEADS * A_HEAD_DIM == LANES
VB_OUT = 5

F32 = jnp.float32
BF16 = jnp.bfloat16


def _dot(a, b):
    return jnp.dot(a, b, preferred_element_type=F32)


def _dot_nt(a, b):
    return lax.dot_general(a, b, (((1,), (1,)), ((), ())), preferred_element_type=F32)


def _cparams(sem):
    return pltpu.CompilerParams(dimension_semantics=sem, vmem_limit_bytes=VMEM_LIMIT_BYTES)


def _rms(x, g):
    return x * lax.rsqrt(jnp.mean(x * x, axis=-1, keepdims=True) + EPS) * g


def _modvec_kernel(c_ref, w_ref, b_ref, o_ref):
    c = c_ref[...]
    a = (c * jax.nn.sigmoid(c)).astype(BF16)
    o_ref[0] = _dot(a, w_ref[0].astype(BF16)) + b_ref[0]


def _modvec(cvec, w_mod, b_mod):
    rows, d = cvec.shape
    depth, _, n = w_mod.shape
    tn = d
    return pl.pallas_call(
        _modvec_kernel,
        grid=(depth, n // tn),
        in_specs=[pl.BlockSpec((rows, d), lambda l, j: (0, 0)),
                  pl.BlockSpec((1, d, tn), lambda l, j: (l, 0, j)),
                  pl.BlockSpec((1, 1, tn), lambda l, j: (l, 0, j))],
        out_specs=pl.BlockSpec((1, rows, tn), lambda l, j: (l, 0, j)),
        out_shape=jax.ShapeDtypeStruct((depth, rows, n), F32),
        compiler_params=_cparams(("arbitrary", "arbitrary")),
        name="modvec",
    )(cvec, w_mod, b_mod.reshape(depth, 1, n))


def _rope_tile(x, cos, sin, half):
    lane = lax.broadcasted_iota(jnp.int32, x.shape, 1)
    fwd = pltpu.roll(x, LANES - half, 1)
    bwd = pltpu.roll(x, half, 1)
    partner = jnp.where((lane % (2 * half)) < half, fwd, bwd)
    return x * cos + partner * sin


def _ffn_residual(h, mod_rows, y0, y1, info):
    ffn = y0.astype(F32) * info[:, 4:5] + y1.astype(F32) * info[:, 5:6]
    return h + mod_rows[5:6] * ffn


def _stream_specs(h, tm):
    if isinstance(h, tuple):
        lat, ctx = h
        d = lat.shape[1]
        nl = lat.shape[0] // tm
        return ([pl.BlockSpec((tm, d), lambda i, *_: (jnp.minimum(i, nl - 1), 0)),
                 pl.BlockSpec((tm, d), lambda i, *_: (jnp.maximum(i - nl, 0), 0))], [lat, ctx], nl)
    return [pl.BlockSpec((tm, h.shape[1]), lambda i, *_: (i, 0))], [h], None


def _stream_tile(refs, n_lat_tiles):
    if n_lat_tiles is None:
        return (lambda: refs[0][...]), refs[1:]
    is_lat = pl.program_id(0) < n_lat_tiles
    return (lambda: jnp.where(is_lat, refs[0][...], refs[1][...])), refs[2:]


def _inproj_kernel(tmod_ref, trope_ref, *refs, pending, n_lat_tiles):
    del tmod_ref, trope_ref
    load_x, refs = _stream_tile(refs, n_lat_tiles)
    if pending:
        y0_ref, y1_ref, info_ref, modp_ref = refs[:4]
        refs = refs[4:]
    (mod_ref, g_ref, w1_ref, qng_ref, kvng_ref, wuq_ref, wuk_ref, wuv_ref, ca_ref, sa_ref, cb_ref,
     sb_ref, qa_o, ka_o, va_o, qb_o, kb_o, vb_o, u_o, gs_o) = refs[:20]
    x = load_x()
    if pending:
        x = _ffn_residual(x, modp_ref[0], y0_ref[...], y1_ref[...], info_ref[...])
        refs[20][...] = x
    m = mod_ref[0]
    hx = _rms(x, g_ref[...]) * (1.0 + m[1:2]) + m[0:1]
    hb = hx.astype(BF16)

    def seg(name, width):
        off = _SEG[name]
        return _dot(hb, w1_ref[:, off:off + width])

    ca, sa = ca_ref[...], sa_ref[...]
    cb, sb = cb_ref[...], sb_ref[...]

    qa = seg("qa", A_WIDTH)
    for t in range(A_WIDTH // LANES):
        sl = slice(t * LANES, (t + 1) * LANES)
        qa_o[:, sl] = _rope_tile(qa[:, sl], ca, sa, A_HEAD_DIM // 2).astype(BF16)
    ka_o[...] = _rope_tile(seg("ka", A_KV_WIDTH), ca, sa, A_HEAD_DIM // 2).astype(BF16)
    va_o[...] = seg("va", A_KV_WIDTH).astype(BF16)

    cqn = _rms(seg("cq", Q_LORA), qng_ref[...]).astype(BF16)
    ckvn = _rms(seg("ckv", KV_LORA), kvng_ref[...]).astype(BF16)
    qb = _dot(cqn, wuq_ref[...])
    kb = _dot(ckvn, wuk_ref[...])
    kr = _rope_tile(seg("kr", LANES), cb, sb, B_ROPE // 2)
    qscale = (B_NOPE + B_ROPE) ** -0.5 * math.log2(math.e)
    for t in range(B_HEADS):
        sl = slice(t * LANES, (t + 1) * LANES)
        qb_o[:, sl] = (_rope_tile(qb[:, sl], cb, sb, B_ROPE // 2) * qscale).astype(BF16)
        kb_o[:, sl] = (kb[:, sl] + kr).astype(BF16)
    vb_o[...] = _dot(ckvn, wuv_ref[...]).T.astype(BF16)

    u_o[...] = seg("u", C_WIDTH).astype(BF16)
    ngate = gs_o.shape[1]
    for t in range(ngate // 512):
        off = W1_GATES_OFF + t * 512
        gs_o[:, t * 512:(t + 1) * 512] = jax.nn.sigmoid(_dot(hb, w1_ref[:, off:off + 512])).astype(BF16)


def _inproj(h_all, mod, norm_g, lw, tables, tile_mod, tile_rope, pending=None):
    tm = ROW_TILE
    lead_specs, lead_args, n_lat_tiles = _stream_specs(h_all, tm)
    t_all = sum(a.shape[0] for a in lead_args)
    d = lead_args[0].shape[1]
    nt = t_all // tm
    w1 = lw["w1"]
    ngate = N_BRANCH * d
    row = lambda i, *_: (i, 0)
    const = lambda i, *_: (0, 0)
    rope = lambda i, tmod, trope: (trope[i], 0)
    modrow = lambda i, tmod, trope: (tmod[i], 0, 0)
    widths = (A_WIDTH, A_KV_WIDTH, A_KV_WIDTH, B_QK_WIDTH, B_QK_WIDTH, B_WIDTH, C_WIDTH, ngate)
    out_specs = [pl.BlockSpec((B_WIDTH, tm), lambda i, *_: (0, i)) if k == VB_OUT
                 else pl.BlockSpec((tm, w), row) for k, w in enumerate(widths)]
    out_shape = [jax.ShapeDtypeStruct((B_WIDTH, t_all) if k == VB_OUT else (t_all, w), BF16)
                 for k, w in enumerate(widths)]
    if pending is not None:
        y0, y1, info, mod_prev = pending
        lead_specs += [pl.BlockSpec((tm, d), row), pl.BlockSpec((tm, d), row),
                       pl.BlockSpec((tm, ROUTER_LANES), row), pl.BlockSpec((1, 8, d), modrow)]
        lead_args += [y0, y1, info, mod_prev]
        out_specs.append(pl.BlockSpec((tm, d), row))
        out_shape.append(jax.ShapeDtypeStruct((t_all, d), F32))
    grid_spec = pltpu.PrefetchScalarGridSpec(
        num_scalar_prefetch=2,
        grid=(nt,),
        in_specs=lead_specs + [
                  pl.BlockSpec((1, 8, d), modrow),
                  pl.BlockSpec((1, d), const),
                  pl.BlockSpec(w1.shape, const),
                  pl.BlockSpec((1, Q_LORA), const),
                  pl.BlockSpec((1, KV_LORA), const),
                  pl.BlockSpec(lw["wuq"].shape, const),
                  pl.BlockSpec(lw["wuk"].shape, const),
                  pl.BlockSpec(lw["wuv"].shape, const),
                  pl.BlockSpec((tm, LANES), rope),
                  pl.BlockSpec((tm, LANES), rope),
                  pl.BlockSpec((tm, LANES), rope),
                  pl.BlockSpec((tm, LANES), rope)],
        out_specs=out_specs,
    )
    return pl.pallas_call(
        functools.partial(_inproj_kernel, pending=pending is not None, n_lat_tiles=n_lat_tiles),
        grid_spec=grid_spec,
        out_shape=out_shape,
        compiler_params=_cparams(("arbitrary",)),
        name="inproj",
    )(tile_mod, tile_rope, *lead_args, mod, norm_g, w1, lw["qng"], lw["kvng"], lw["wuq"], lw["wuk"],
      lw["wuv"], tables["ca"], tables["sa"], tables["cb"], tables["sb"])


def _gqa_kernel(sink_ref, q_ref, *refs, local, ntile):
    if local:
        kp_ref, kc_ref, kn_ref, vp_ref, vc_ref, vn_ref, kx_ref, vx_ref, o_ref = refs
    else:
        kx_ref, vx_ref, _, o_ref = refs
    tq = q_ref.shape[0]
    hd = A_HEAD_DIM
    sub = WINDOW if local else tq
    nsub = tq // sub
    rows = A_GROUP * sub
    lane = lax.broadcasted_iota(jnp.int32, (1, LANES), 1)
    head_row = lax.broadcasted_iota(jnp.int32, (rows, 1), 0) // sub
    sinks = []
    for g in range(A_KV_HEADS):
        col = jnp.zeros((rows, 1), F32)
        for j in range(A_GROUP):
            col = jnp.where(head_row == j, sink_ref[g * A_GROUP + j], col)
        sinks.append(col)
    kx, vx = kx_ref[...], vx_ref[...]
    if local:
        j_tile = pl.program_id(1)
        kloc = jnp.concatenate([kp_ref[...], kc_ref[...], kn_ref[...]], axis=0)
        vloc = jnp.concatenate([vp_ref[...], vc_ref[...], vn_ref[...]], axis=0)
        r = lax.broadcasted_iota(jnp.int32, (sub, 3 * WINDOW), 0)
        c = lax.broadcasted_iota(jnp.int32, (sub, 3 * WINDOW), 1)
        band = jnp.abs(c - WINDOW - r) <= WINDOW
    def block_scores(s):
        q = jnp.concatenate([q_ref[s * sub:(s + 1) * sub, j * LANES:(j + 1) * LANES]
                             for j in range(A_GROUP)], axis=0)
        if local:
            kcat = jnp.concatenate([kloc[s * sub:s * sub + 3 * WINDOW], kx], axis=0)
            vcat = jnp.concatenate([vloc[s * sub:s * sub + 3 * WINDOW], vx], axis=0)
            ok = band
            if s == 0:
                ok = jnp.logical_and(ok, jnp.logical_or(c >= WINDOW, j_tile > 0))
            if s == nsub - 1:
                ok = jnp.logical_and(ok, jnp.logical_or(c < 2 * WINDOW, j_tile < ntile - 1))
            bias = jnp.where(ok, 0.0, NEG_INF)
            bias = jnp.concatenate([bias] * A_GROUP, axis=0)
        else:
            kcat, vcat = kx, vx
        scs = []
        for g in range(A_KV_HEADS):
            in_g = jnp.logical_and(lane >= g * hd, lane < (g + 1) * hd)
            kg = jnp.where(in_g, kcat, jnp.zeros_like(kcat))
            sc = _dot_nt(q, kg)
            if local:
                sc = jnp.concatenate([sc[:, :3 * WINDOW] + bias, sc[:, 3 * WINDOW:]], axis=1)
            scs.append(sc)
        return scs, vcat

    def block_output(s, scs, vcat):
        outs = []
        for g in range(A_KV_HEADS):
            mx = jnp.maximum(jnp.max(scs[g], axis=-1, keepdims=True), sinks[g])
            e = jnp.exp(scs[g] - mx)
            denom = jnp.sum(e, axis=-1, keepdims=True) + jnp.exp(sinks[g] - mx)
            outs.append(_dot(e.astype(BF16), vcat) / denom)
        o = jnp.where(lane < hd, outs[0], outs[1]).astype(BF16)
        for j in range(A_GROUP):
            o_ref[s * sub:(s + 1) * sub, j * LANES:(j + 1) * LANES] = o[j * sub:(j + 1) * sub]

    pending = block_scores(0)
    for s in range(1, nsub):
        nxt = block_scores(s)
        block_output(s - 1, *pending)
        pending = nxt
    block_output(nsub - 1, *pending)


def _window_attn(qa, ka, va, sink, batch, n, lc, out_rows):
    t = batch * n
    tq = min(WIN_Q_TILE, n)
    ntile = n // tq
    sub = tq // WINDOW
    nblk = n // WINDOW
    cblk = t // lc
    qmap = lambda b, j: (b * ntile + j, 0)
    prev = lambda b, j: (b * nblk + jnp.maximum(j * sub - 1, 0), 0)
    nxt = lambda b, j: (b * nblk + jnp.minimum((j + 1) * sub, nblk - 1), 0)
    ctx = lambda b, j: (cblk + b, 0)
    kvw = A_KV_WIDTH
    return pl.pallas_call(
        functools.partial(_gqa_kernel, local=True, ntile=ntile),
        grid=(batch, ntile),
        in_specs=[pl.BlockSpec(memory_space=pltpu.SMEM),
                  pl.BlockSpec((tq, A_WIDTH), qmap),
                  pl.BlockSpec((WINDOW, kvw), prev), pl.BlockSpec((tq, kvw), qmap),
                  pl.BlockSpec((WINDOW, kvw), nxt),
                  pl.BlockSpec((WINDOW, kvw), prev), pl.BlockSpec((tq, kvw), qmap),
                  pl.BlockSpec((WINDOW, kvw), nxt),
                  pl.BlockSpec((lc, kvw), ctx), pl.BlockSpec((lc, kvw), ctx)],
        out_specs=pl.BlockSpec((tq, A_WIDTH), qmap),
        out_shape=jax.ShapeDtypeStruct((out_rows, A_WIDTH), BF16),
        compiler_params=_cparams(("arbitrary", "arbitrary")),
        name="window_attn",
    )(sink, qa, ka, ka, ka, va, va, va, ka, va)


def _ctx_gqa_attn(qa, ka, va, sink, oa, batch, n, lc):
    cblk = batch * n // lc
    cmap = lambda b: (cblk + b, 0)
    return pl.pallas_call(
        functools.partial(_gqa_kernel, local=False, ntile=0),
        grid=(batch,),
        in_specs=[pl.BlockSpec(memory_space=pltpu.SMEM),
                  pl.BlockSpec((lc, A_WIDTH), cmap),
                  pl.BlockSpec((lc, A_KV_WIDTH), cmap), pl.BlockSpec((lc, A_KV_WIDTH), cmap),
                  pl.BlockSpec(memory_space=pl.ANY)],
        out_specs=pl.BlockSpec((lc, A_WIDTH), cmap),
        out_shape=jax.ShapeDtypeStruct(oa.shape, BF16),
        input_output_aliases={4: 0},
        compiler_params=_cparams(("arbitrary",)),
        name="ctx_gqa_attn",
    )(sink, qa, ka, va, oa)


def _mla_kernel(q_ref, kc_ref, vc_ref, *refs, with_latent):
    if with_latent:
        kx_ref, vx_ref, o_ref, s_ref = refs
    else:
        _, o_ref, s_ref = refs
    tq = q_ref.shape[0]
    lc = kc_ref.shape[0]
    chunks = [(kc_ref, vc_ref, 0, lc, 0)]
    if with_latent:
        n = kx_ref.shape[0]
        for r0 in range(0, n, MLA_KEY_CHUNK):
            chunks.append((kx_ref, vx_ref, r0, min(MLA_KEY_CHUNK, n - r0), lc + r0))

    def fold(x, op):
        while x.shape[0] > 8:
            half = x.shape[0] // 2
            x = op(x[:half], x[half:])
        return x

    def scores(h):
        ql = slice(h * B_HEAD_PAD, (h + 1) * B_HEAD_PAD)
        qh = q_ref[:, ql]
        m8 = None
        for k_ref, _, r0, nr, c0 in chunks:
            st = _dot_nt(k_ref[r0:r0 + nr, ql], qh)
            s_ref[h % 2, c0:c0 + nr, :] = st
            part = fold(st, jnp.maximum)
            m8 = part if m8 is None else jnp.maximum(m8, part)
        return jnp.max(m8, axis=0, keepdims=True)

    def values(h, mx):
        vl = slice(h * B_VDIM, (h + 1) * B_VDIM)
        l8 = None
        acc = None
        for _, vt_ref, r0, nr, c0 in chunks:
            et = jnp.exp2(s_ref[h % 2, c0:c0 + nr, :] - mx)
            part = fold(et, jnp.add)
            l8 = part if l8 is None else l8 + part
            pv = _dot(vt_ref[vl, r0:r0 + nr], et.astype(BF16))
            acc = pv if acc is None else acc + pv
        return acc / jnp.sum(l8, axis=0, keepdims=True)

    outs = []
    mx_prev = scores(0)
    for h in range(1, B_HEADS):
        mx = scores(h)
        outs.append(values(h - 1, mx_prev))
        mx_prev = mx
    outs.append(values(B_HEADS - 1, mx_prev))
    o_ref[...] = jnp.concatenate(outs, axis=0).T.astype(BF16)


def _mla_attn(qb, kb, vb, batch, n, lc, out_rows):
    t = batch * n
    tq = min(MLA_Q_TILE, n)
    nq = n // tq
    cblk = t // lc
    qmap = lambda b, j: (b * nq + j, 0)
    ctx = lambda b, j: (cblk + b, 0)
    lat = lambda b, j: (b, 0)
    ctx_t = lambda b, j: (0, cblk + b)
    lat_t = lambda b, j: (0, b)
    return pl.pallas_call(
        functools.partial(_mla_kernel, with_latent=True),
        grid=(batch, nq),
        in_specs=[pl.BlockSpec((tq, B_QK_WIDTH), qmap),
                  pl.BlockSpec((lc, B_QK_WIDTH), ctx), pl.BlockSpec((B_WIDTH, lc), ctx_t),
                  pl.BlockSpec((n, B_QK_WIDTH), lat), pl.BlockSpec((B_WIDTH, n), lat_t)],
        out_specs=pl.BlockSpec((tq, B_WIDTH), qmap),
        out_shape=jax.ShapeDtypeStruct((out_rows, B_WIDTH), BF16),
        scratch_shapes=[pltpu.VMEM((2, lc + n, tq), F32)],
        compiler_params=_cparams(("arbitrary", "arbitrary")),
        name="mla_attn",
    )(qb, kb, vb, kb, vb)


def _mla_ctx_attn(qb, kb, vb, ob, batch, n, lc):
    cblk = batch * n // lc
    cmap = lambda b: (cblk + b, 0)
    return pl.pallas_call(
        functools.partial(_mla_kernel, with_latent=False),
        grid=(batch,),
        in_specs=[pl.BlockSpec((lc, B_QK_WIDTH), cmap),
                  pl.BlockSpec((lc, B_QK_WIDTH), cmap),
                  pl.BlockSpec((B_WIDTH, lc), lambda b: (0, cblk + b)),
                  pl.BlockSpec(memory_space=pl.ANY)],
        out_specs=pl.BlockSpec((lc, B_WIDTH), cmap),
        out_shape=jax.ShapeDtypeStruct(ob.shape, BF16),
        input_output_aliases={3: 0},
        scratch_shapes=[pltpu.VMEM((2, lc, lc), F32)],
        compiler_params=_cparams(("arbitrary",)),
        name="mla_ctx_attn",
    )(qb, kb, vb, ob)


def _route_tile(logits, run):
    tm = logits.shape[0]
    lane_i = lax.broadcasted_iota(jnp.int32, logits.shape, 1)
    lane = lane_i.astype(F32)
    nolane = float(LANES)

    def first_argmax(v):
        mx = jnp.max(v, axis=-1, keepdims=True)
        return mx, jnp.min(jnp.where(v == mx, lane, nolane), axis=-1, keepdims=True)

    gl = jnp.where(lane_i < N_GROUPS, logits, NEG_INF)
    gmax, g_sel = first_argmax(gl)
    g_p = 1.0 / jnp.sum(jnp.exp(gl - gmax), axis=-1, keepdims=True)
    lo = N_GROUPS + EXPERTS_PER_GROUP * g_sel
    el = jnp.where(jnp.logical_and(lane >= lo, lane < lo + EXPERTS_PER_GROUP), logits, NEG_INF)
    v1, i1 = first_argmax(el)
    sel1 = lane == i1
    v2, i2 = first_argmax(jnp.where(sel1, NEG_INF, el))
    sel2 = lane == i2
    t2 = jnp.exp(v2 - v1)
    gate1 = g_p / (1.0 + t2)
    gate2 = gate1 * t2

    onehot = jnp.where(jnp.logical_or(sel1, sel2), 1.0, 0.0)
    r_i = lax.broadcasted_iota(jnp.int32, (tm, tm), 0)
    c_i = lax.broadcasted_iota(jnp.int32, (tm, tm), 1)
    lower = jnp.where(c_i < r_i, 1.0, 0.0).astype(BF16)
    before = _dot(lower, onehot.astype(BF16)) + run
    rank1 = jnp.sum(jnp.where(sel1, before, 0.0), axis=-1, keepdims=True)
    rank2 = jnp.sum(jnp.where(sel2, before, 0.0), axis=-1, keepdims=True)
    info = jnp.zeros_like(logits)
    for k, val in enumerate((i1 - N_GROUPS, i2 - N_GROUPS, rank1, rank2, gate1, gate2)):
        info = jnp.where(lane_i == k, val, info)
    return info, run + jnp.sum(onehot, axis=0, keepdims=True)


def _merge_kernel(tmod_ref, tpos_ref, tlen_ref, *refs, n_lat_tiles):
    del tmod_ref
    load_h, refs = _stream_tile(refs, n_lat_tiles)
    (mod_ref, g_ref, oa_ref, ob_ref, u_ref, up_ref, un_ref, gs_ref, wpool_ref, pscale_ref, wa_ref,
     wb_ref, wc_ref, wo_ref, wr_ref, br_ref, hn_o, fx_o, info_o, cnt_o, run_ref) = refs
    i = pl.program_id(0)

    @pl.when(i == 0)
    def _():
        run_ref[...] = jnp.zeros_like(run_ref)

    pos0 = tpos_ref[i]
    seq_len = tlen_ref[i]
    tm = u_ref.shape[0]
    d = hn_o.shape[1]

    u = u_ref[...]
    kdim = tm + LANES
    zpad = jnp.zeros((LANES - 2 * POOL_HALO, u.shape[1]), BF16)
    ucat = jnp.concatenate([up_ref[...], u, un_ref[...], zpad], axis=0)
    uf = u.astype(F32)
    r_i = lax.broadcasted_iota(jnp.int32, (tm, kdim), 0)
    c_i = lax.broadcasted_iota(jnp.int32, (tm, kdim), 1)
    rel = c_i - POOL_HALO - r_i
    kpos = pos0 + c_i - POOL_HALO
    valid = jnp.logical_and(kpos >= 0, kpos < seq_len)
    tpos = pos0 + lax.broadcasted_iota(jnp.int32, (tm, 1), 0)
    win_sums = []
    for gi, w in enumerate(POOL_WINDOWS):
        rad = w // 2
        sl = slice(gi * C_GROUP_DIM, (gi + 1) * C_GROUP_DIM)
        band = jnp.where(jnp.logical_and(jnp.abs(rel) <= rad, valid), 1.0, 0.0).astype(BF16)
        win_sums.append(_dot(band, ucat[:, sl]))
    ya = _dot(oa_ref[...], wa_ref[...])
    oc_parts = []
    for gi, w in enumerate(POOL_WINDOWS):
        rad = w // 2
        sl = slice(gi * C_GROUP_DIM, (gi + 1) * C_GROUP_DIM)
        cnt = (jnp.minimum(tpos + rad + 1, seq_len) - jnp.maximum(tpos - rad, 0)).astype(F32)
        pooled = (win_sums[gi] / cnt - uf[:, sl]).astype(BF16)
        oc_parts.append(_dot(pooled, wpool_ref[gi]))
    yb = _dot(ob_ref[...], wb_ref[...])
    oc = (jnp.concatenate(oc_parts, axis=1) * pscale_ref[...]).astype(BF16)

    y = gs_ref[:, 0:d].astype(F32) * ya
    y = y + gs_ref[:, d:2 * d].astype(F32) * yb
    y = y + gs_ref[:, 2 * d:3 * d].astype(F32) * _dot(oc, wc_ref[...])
    mix = _dot(y.astype(BF16), wo_ref[...])

    m = mod_ref[0]
    hn = load_h() + m[2:3] * mix
    hn_o[...] = hn
    fx = _rms(hn, g_ref[...]) * (1.0 + m[4:5]) + m[3:4]
    fx_hi = fx.astype(BF16)
    fx_lo = (fx - fx_hi.astype(F32)).astype(BF16)
    fx_o[...] = fx_hi
    both = _dot(fx_hi, wr_ref[...])
    logits = (both[:, :ROUTER_LANES] + _dot(fx_lo, wr_ref[:, :ROUTER_LANES])
              + both[:, ROUTER_LANES:]) + br_ref[...]
    info, run = _route_tile(logits, run_ref[...])
    info_o[...] = info
    run_ref[...] = run
    cnt_o[...] = run


def _merge(h_all, mod, norm_g, oa, ob, u, gs, lw, rows, tile_mod, tile_pos, tile_len):
    tm = ROW_TILE
    stream_specs, stream_args, n_lat_tiles = _stream_specs(h_all, tm)
    d = stream_args[0].shape[1]
    nt = rows // tm
    hb = tm // POOL_HALO
    nhalo = u.shape[0] // POOL_HALO
    row = lambda i, *_: (i, 0)
    const = lambda i, *_: (0, 0)
    const3 = lambda i, *_: (0, 0, 0)
    grid_spec = pltpu.PrefetchScalarGridSpec(
        num_scalar_prefetch=3,
        grid=(nt,),
        in_specs=stream_specs + [
                  pl.BlockSpec((1, 8, d), lambda i, tmod, *_: (tmod[i], 0, 0)),
                  pl.BlockSpec((1, d), const),
                  pl.BlockSpec((tm, A_WIDTH), row),
                  pl.BlockSpec((tm, B_WIDTH), row),
                  pl.BlockSpec((tm, C_WIDTH), row),
                  pl.BlockSpec((POOL_HALO, C_WIDTH), lambda i, *_: (jnp.maximum(i * hb - 1, 0), 0)),
                  pl.BlockSpec((POOL_HALO, C_WIDTH), lambda i, *_: (jnp.minimum((i + 1) * hb, nhalo - 1), 0)),
                  pl.BlockSpec((tm, N_BRANCH * d), row),
                  pl.BlockSpec(lw["wpool"].shape, const3),
                  pl.BlockSpec((1, C_WIDTH), const),
                  pl.BlockSpec(lw["wa"].shape, const),
                  pl.BlockSpec(lw["wb"].shape, const),
                  pl.BlockSpec(lw["wc"].shape, const),
                  pl.BlockSpec(lw["wo"].shape, const),
                  pl.BlockSpec(lw["wr"].shape, const),
                  pl.BlockSpec((1, ROUTER_LANES), const)],
        out_specs=[pl.BlockSpec((tm, d), row), pl.BlockSpec((tm, d), row),
                   pl.BlockSpec((tm, ROUTER_LANES), row), pl.BlockSpec((1, ROUTER_LANES), const)],
        scratch_shapes=[pltpu.VMEM((1, ROUTER_LANES), F32)],
    )
    return pl.pallas_call(
        functools.partial(_merge_kernel, n_lat_tiles=n_lat_tiles),
        grid_spec=grid_spec,
        out_shape=[jax.ShapeDtypeStruct((rows, d), F32), jax.ShapeDtypeStruct((rows, d), BF16),
                   jax.ShapeDtypeStruct((rows, ROUTER_LANES), F32),
                   jax.ShapeDtypeStruct((1, ROUTER_LANES), F32)],
        compiler_params=_cparams(("arbitrary",)),
        name="merge",
    )(tile_mod, tile_pos, tile_len, *stream_args, mod, norm_g, oa, ob, u, u, u, gs, lw["wpool"],
      lw["pscale"], lw["wa"], lw["wb"], lw["wc"], lw["wo"], lw["wr"], lw["br"])


def _moe_kernel(blk_e_ref, nused_ref, x_ref, wgu_ref, wdn_ref, o_ref):
    del blk_e_ref
    i = pl.program_id(0)

    @pl.when(i < nused_ref[0])
    def _():
        gu = _dot(x_ref[...], wgu_ref[0, 0].astype(BF16))
        gt, up = gu[:, :EXPERT_FF], gu[:, EXPERT_FF:]
        act = (gt * jax.nn.sigmoid(gt) * up).astype(BF16)
        o_ref[...] = _dot(act, wdn_ref[0, 0].astype(BF16)).astype(o_ref.dtype)

    @pl.when(i >= nused_ref[0])
    def _():
        o_ref[...] = jnp.zeros_like(o_ref)


def _moe(xg, blk_e, n_used, w_gu, w_dn, layer):
    p, d = xg.shape
    nb = p // MOE_BLOCK
    grid_spec = pltpu.PrefetchScalarGridSpec(
        num_scalar_prefetch=2,
        grid=(nb,),
        in_specs=[pl.BlockSpec((MOE_BLOCK, d), lambda i, *_: (i, 0)),
                  pl.BlockSpec((1, 1, d, 2 * EXPERT_FF), lambda i, be, nu: (layer, be[i], 0, 0)),
                  pl.BlockSpec((1, 1, EXPERT_FF, d), lambda i, be, nu: (layer, be[i], 0, 0))],
        out_specs=pl.BlockSpec((MOE_BLOCK, d), lambda i, *_: (i, 0)),
    )
    return pl.pallas_call(
        _moe_kernel,
        grid_spec=grid_spec,
        out_shape=jax.ShapeDtypeStruct((p, d), BF16),
        compiler_params=_cparams(("arbitrary",)),
        name="moe_experts",
    )(blk_e, n_used, xg, w_gu, w_dn)


def _resid_kernel(tmod_ref, h_ref, mod_ref, y0_ref, y1_ref, info_ref, fg_ref, o_ref, *, final):
    del tmod_ref
    hn = _ffn_residual(h_ref[...], mod_ref[0], y0_ref[...], y1_ref[...], info_ref[...])
    if final:
        hn = _rms(hn, fg_ref[...])
    o_ref[...] = hn


def _resid(h, mod, y0, y1, info, final_g, tile_mod, final):
    rows, d = h.shape
    tm = ROW_TILE
    row = lambda i, *_: (i, 0)
    grid_spec = pltpu.PrefetchScalarGridSpec(
        num_scalar_prefetch=1,
        grid=(rows // tm,),
        in_specs=[pl.BlockSpec((tm, d), row),
                  pl.BlockSpec((1, 8, d), lambda i, tmod: (tmod[i], 0, 0)),
                  pl.BlockSpec((tm, d), row), pl.BlockSpec((tm, d), row),
                  pl.BlockSpec((tm, ROUTER_LANES), row),
                  pl.BlockSpec((1, d), lambda i, *_: (0, 0))],
        out_specs=pl.BlockSpec((tm, d), row),
    )
    return pl.pallas_call(
        functools.partial(_resid_kernel, final=final),
        grid_spec=grid_spec,
        out_shape=jax.ShapeDtypeStruct((rows, d), F32),
        compiler_params=_cparams(("arbitrary",)),
        name="ffn_residual",
    )(tile_mod, h, mod, y0, y1, info, final_g)


def _deinterleave(n):
    return np.concatenate([np.arange(0, n, 2), np.arange(1, n, 2)])


def _rope_tables(n, tm):
    rows = n // GRID_W
    row = np.repeat(np.arange(rows), GRID_W).astype(np.float32)
    col = np.tile(np.arange(GRID_W), rows).astype(np.float32)

    def cs(rot_dim):
        axis_dim = rot_dim // 2
        inv = jnp.asarray(ROPE_BASE, F32) ** (-jnp.arange(0, axis_dim, 2, dtype=F32) / axis_dim)
        ang = jnp.concatenate([jnp.asarray(row)[:, None] * inv, jnp.asarray(col)[:, None] * inv], axis=-1)
        return jnp.cos(ang), jnp.sin(ang)

    cos_a, sin_a = cs(A_HEAD_DIM)
    ca = jnp.tile(jnp.concatenate([cos_a, cos_a], axis=1), (1, LANES // A_HEAD_DIM))
    sa = jnp.tile(jnp.concatenate([-sin_a, sin_a], axis=1), (1, LANES // A_HEAD_DIM))
    cos_b, sin_b = cs(B_ROPE)
    ones = jnp.ones((n, B_NOPE), F32)
    tail = LANES - B_NOPE - B_ROPE
    cb = jnp.concatenate([ones, cos_b, cos_b, jnp.ones((n, tail), F32)], axis=1)
    sb = jnp.concatenate([0 * ones, -sin_b, sin_b, jnp.zeros((n, tail), F32)], axis=1)
    ident_c = jnp.ones((tm, LANES), F32)
    ident_s = jnp.zeros((tm, LANES), F32)
    return {"ca": jnp.concatenate([ca, ident_c]), "sa": jnp.concatenate([sa, ident_s]),
            "cb": jnp.concatenate([cb, ident_c]), "sb": jnp.concatenate([sb, ident_s])}


def _layer_weights(l, w_in, q_norm_g, w_uq, kv_norm_g, w_ukv, w_pool, pool_scale, w_br_a, w_br_b,
                   w_br_c, w_out, w_rg, b_rg, w_re, b_re):
    d = w_in.shape[1]
    wi = w_in[l]
    splits = np.cumsum([A_WIDTH, A_KV_WIDTH, A_KV_WIDTH, Q_LORA, KV_LORA, B_ROPE, C_WIDTH])
    qa, ka, va, cq, ckv, kr, u, gates = jnp.split(wi, splits, axis=1)
    pa = _deinterleave(A_HEAD_DIM)
    ho = np.arange(A_HEADS).reshape(A_KV_HEADS, A_GROUP).T.reshape(-1)
    qa = qa.reshape(d, A_HEADS, A_HEAD_DIM)[:, ho][:, :, pa].reshape(d, A_WIDTH) * (A_HEAD_DIM ** -0.5)
    wa = w_br_a[l].reshape(A_HEADS, A_HEAD_DIM, -1)[ho].reshape(A_WIDTH, -1)
    ka = ka.reshape(d, A_KV_HEADS, A_HEAD_DIM)[:, :, pa].reshape(d, A_KV_WIDTH)
    pb = _deinterleave(B_ROPE)
    tail = LANES - B_NOPE - B_ROPE
    kr128 = jnp.concatenate([jnp.zeros((d, B_NOPE), F32), kr[:, pb], jnp.zeros((d, tail), F32)], axis=1)
    w1 = jnp.concatenate([qa, ka, va, cq, ckv, kr128, u, gates], axis=1).astype(BF16)

    uq = w_uq[l].reshape(Q_LORA, B_HEADS, B_NOPE + B_ROPE)
    uq = jnp.concatenate([uq[:, :, :B_NOPE], uq[:, :, B_NOPE:][:, :, pb],
                          jnp.zeros((Q_LORA, B_HEADS, tail), F32)], axis=2)
    ukv = w_ukv[l].reshape(KV_LORA, B_HEADS, B_NOPE + B_VDIM)
    uk = jnp.concatenate([ukv[:, :, :B_NOPE], jnp.zeros((KV_LORA, B_HEADS, LANES - B_NOPE), F32)], axis=2)
    uv = ukv[:, :, B_NOPE:]

    wr = jnp.concatenate([w_rg[l], w_re[l], jnp.zeros((d, ROUTER_LANES - N_GROUPS - N_EXPERTS), F32)], axis=1)
    wr_hi = wr.astype(BF16)
    wr_lo = (wr - wr_hi.astype(F32)).astype(BF16)
    br = jnp.concatenate([b_rg[l], b_re[l], jnp.zeros((ROUTER_LANES - N_GROUPS - N_EXPERTS,), F32)])
    return {
        "w1": w1,
        "qng": q_norm_g[l].reshape(1, Q_LORA), "kvng": kv_norm_g[l].reshape(1, KV_LORA),
        "wuq": uq.reshape(Q_LORA, B_QK_WIDTH).astype(BF16),
        "wuk": uk.reshape(KV_LORA, B_QK_WIDTH).astype(BF16),
        "wuv": uv.reshape(KV_LORA, B_WIDTH).astype(BF16),
        "wpool": w_pool[l].astype(BF16), "pscale": pool_scale[l].reshape(1, C_WIDTH),
        "wa": wa.astype(BF16), "wb": w_br_b[l].astype(BF16), "wc": w_br_c[l].astype(BF16),
        "wo": w_out[l].astype(BF16),
        "wr": jnp.concatenate([wr_hi, wr_lo], axis=1), "br": br.reshape(1, ROUTER_LANES),
    }


def _block_layout(info, cnt):
    t = info.shape[0]
    a = t * TOP_K
    expert = info[:, 0:TOP_K].astype(jnp.int32)
    rank = info[:, TOP_K:2 * TOP_K].astype(jnp.int32)
    counts = cnt[0, N_GROUPS:N_GROUPS + N_EXPERTS].astype(jnp.int32)
    padded = (counts + MOE_BLOCK - 1) // MOE_BLOCK * MOE_BLOCK
    pad_end = jnp.cumsum(padded)
    pad_start = pad_end - padded
    eids = jnp.arange(N_EXPERTS, dtype=jnp.int32)
    base = jnp.sum(jnp.where(expert[:, :, None] == eids, pad_start, 0), axis=-1)
    slot = base + rank
    n_blocks = -(-(a + N_EXPERTS * (MOE_BLOCK - 1)) // MOE_BLOCK)
    blk_start = jnp.arange(n_blocks, dtype=jnp.int32) * MOE_BLOCK
    blk_e = jnp.minimum(jnp.sum((pad_end[None, :] <= blk_start[:, None]).astype(jnp.int32), axis=1),
                        N_EXPERTS - 1)
    n_used = (pad_end[-1] // MOE_BLOCK).reshape(1)

    by_slot = jnp.argsort(slot.reshape(a)).astype(jnp.int32) // TOP_K
    first = jnp.cumsum(counts) - counts
    blk_sel = blk_e[:, None] == eids[None, :]
    blk_off = blk_start - jnp.sum(jnp.where(blk_sel, pad_start, 0), axis=1)
    blk_src = jnp.sum(jnp.where(blk_sel, first, 0), axis=1) + blk_off
    blk_cnt = jnp.sum(jnp.where(blk_sel, counts, 0), axis=1) - blk_off
    within = jnp.arange(MOE_BLOCK, dtype=jnp.int32)[None, :]
    src = jnp.minimum(blk_src[:, None] + within, a - 1)
    filler = (blk_start[:, None] + within) % t
    tok_of_slot = jnp.where(within < blk_cnt[:, None], by_slot.at[src].get(mode="promise_in_bounds"),
                            filler)
    return slot, tok_of_slot.reshape(n_blocks * MOE_BLOCK), blk_e, n_used


def kernel(x, c, ctx, c_ctx, w_mod, b_mod, norm_mix_g, norm_ffn_g, w_in, sink, q_norm_g, w_uq, kv_norm_g,
           w_ukv, w_pool, pool_scale, w_br_a, w_br_b, w_br_c, w_out, w_rg, b_rg, w_re, b_re, w_gu, w_dn,
           final_g):
    batch, n, d = x.shape
    lc = ctx.shape[1]
    depth = w_mod.shape[0]
    tm = ROW_TILE
    assert n % tm == 0 and lc % tm == 0 and n % WINDOW == 0 and (batch * n) % lc == 0
    t = batch * n
    tc = batch * lc
    nt_lat, nt_ctx = t // tm, tc // tm
    per_b, per_c = n // tm, lc // tm

    lat_i = np.arange(nt_lat)
    ctx_i = np.arange(nt_ctx)
    tile_mod = jnp.asarray(np.concatenate([lat_i // per_b, np.full(nt_ctx, batch)]), jnp.int32)
    tile_rope = jnp.asarray(np.concatenate([lat_i % per_b, np.full(nt_ctx, per_b)]), jnp.int32)
    tile_pos = jnp.asarray(np.concatenate([(lat_i % per_b) * tm, (ctx_i % per_c) * tm]), jnp.int32)
    tile_len = jnp.asarray(np.concatenate([np.full(nt_lat, n), np.full(nt_ctx, lc)]), jnp.int32)
    tables = _rope_tables(n, tm)

    mod_rows = 16
    cvec = jnp.concatenate([c, c_ctx[None, :], jnp.zeros((mod_rows - batch - 1, d), F32)], axis=0)
    h_all = (x.reshape(t, d), ctx.reshape(tc, d))

    mod_all = _modvec(cvec, w_mod, b_mod).reshape(depth, mod_rows, 6, d)
    mod_all = jnp.concatenate([mod_all, jnp.zeros((depth, mod_rows, 2, d), F32)], axis=2)

    out = None
    pending = None
    for l in range(depth):
        last = l == depth - 1
        lw = _layer_weights(l, w_in, q_norm_g, w_uq, kv_norm_g, w_ukv, w_pool, pool_scale, w_br_a,
                            w_br_b, w_br_c, w_out, w_rg, b_rg, w_re, b_re)
        mod = mod_all[l]

        proj = _inproj(h_all, mod, norm_mix_g[l].reshape(1, d), lw, tables, tile_mod, tile_rope,
                       pending)
        qa, ka, va, qb, kb, vb, u, gs = proj[:8]
        if pending is not None:
            h_all = proj[8]
        rows = t if last else t + tc
        oa = _window_attn(qa, ka, va, sink[l], batch, n, lc, rows)
        ob = _mla_attn(qb, kb, vb, batch, n, lc, rows)
        if not last:
            oa = _ctx_gqa_attn(qa, ka, va, sink[l], oa, batch, n, lc)
            ob = _mla_ctx_attn(qb, kb, vb, ob, batch, n, lc)
        hn, fx, info, cnt = _merge(h_all, mod, norm_ffn_g[l].reshape(1, d), oa, ob, u, gs, lw, rows,
                                   tile_mod, tile_pos, tile_len)

        slot, tok_of_slot, blk_e, n_used = _block_layout(info, cnt)
        xg = fx.at[tok_of_slot].get(mode="promise_in_bounds")
        yb = _moe(xg, blk_e, n_used, w_gu, w_dn, l)
        y0 = yb.at[slot[:, 0]].get(mode="promise_in_bounds")
        y1 = yb.at[slot[:, 1]].get(mode="promise_in_bounds")
        if last:
            out = _resid(hn, mod, y0, y1, info, final_g.reshape(1, d), tile_mod, True)
        else:
            h_all, pending = hn, (y0, y1, info, mod)
    return out.reshape(batch, n, d)
```

```python
import functools
import math

import jax
import jax.numpy as jnp
import numpy as np
from jax import lax
from jax.experimental import pallas as pl
from jax.experimental.pallas import tpu as pltpu

GRID_W = 64
ROPE_BASE = 10000.0
EPS = 1e-6
NEG_INF = -1e30

A_HEADS = 8
A_KV_HEADS = 2
A_GROUP = A_HEADS // A_KV_HEADS
A_HEAD_DIM = 64
A_WIDTH = A_HEADS * A_HEAD_DIM
A_KV_WIDTH = A_KV_HEADS * A_HEAD_DIM
WINDOW = 128

B_HEADS = 8
B_NOPE = 64
B_ROPE = 32
B_VDIM = 64
B_WIDTH = B_HEADS * B_VDIM
Q_LORA = 256
KV_LORA = 256

POOL_WINDOWS = (2, 4, 8, 16)
C_GROUPS = 4
C_GROUP_DIM = 128
C_WIDTH = C_GROUPS * C_GROUP_DIM

N_BRANCH = 3
N_GROUPS = 4
EXPERTS_PER_GROUP = 8
N_EXPERTS = N_GROUPS * EXPERTS_PER_GROUP
TOP_K = 2
EXPERT_FF = 256

LANES = 128
BF16_SUBLANES = 16
VMEM_LIMIT_BYTES = 56 * 1024 * 1024

B_HEAD_PAD = LANES
B_QK_WIDTH = B_HEADS * B_HEAD_PAD
POOL_HALO = BF16_SUBLANES
ROW_TILE = 256
MLA_Q_TILE = 256
MLA_KEY_CHUNK = 256
WIN_Q_TILE = 512
MOE_BLOCK = 512
ROUTER_LANES = LANES

_SEG = {}
_off = 0
for _name, _w in (("qa", A_WIDTH), ("ka", A_KV_WIDTH), ("va", A_KV_WIDTH), ("cq", Q_LORA),
                  ("ckv", KV_LORA), ("kr", LANES), ("u", C_WIDTH), ("gates", None)):
    _SEG[_name] = _off
    if _w is not None:
        _off += _w
W1_GATES_OFF = _SEG["gates"]
assert A_KV_HEADS * A_HEAD_DIM == LANES
VB_OUT = 5

F32 = jnp.float32
BF16 = jnp.bfloat16


def _dot(a, b):
    return jnp.dot(a, b, preferred_element_type=F32)


def _dot_nt(a, b):
    return lax.dot_general(a, b, (((1,), (1,)), ((), ())), preferred_element_type=F32)


def _cparams(sem):
    return pltpu.CompilerParams(dimension_semantics=sem, vmem_limit_bytes=VMEM_LIMIT_BYTES)


def _rms(x, g):
    return x * lax.rsqrt(jnp.mean(x * x, axis=-1, keepdims=True) + EPS) * g


def _modvec_kernel(c_ref, w_ref, b_ref, o_ref):
    c = c_ref[...]
    a = (c * jax.nn.sigmoid(c)).astype(BF16)
    o_ref[0] = _dot(a, w_ref[0].astype(BF16)) + b_ref[0]


def _modvec(cvec, w_mod, b_mod):
    rows, d = cvec.shape
    depth, _, n = w_mod.shape
    tn = d
    return pl.pallas_call(
        _modvec_kernel,
        grid=(depth, n // tn),
        in_specs=[pl.BlockSpec((rows, d), lambda l, j: (0, 0)),
                  pl.BlockSpec((1, d, tn), lambda l, j: (l, 0, j)),
                  pl.BlockSpec((1, 1, tn), lambda l, j: (l, 0, j))],
        out_specs=pl.BlockSpec((1, rows, tn), lambda l, j: (l, 0, j)),
        out_shape=jax.ShapeDtypeStruct((depth, rows, n), F32),
        compiler_params=_cparams(("arbitrary", "arbitrary")),
        name="modvec",
    )(cvec, w_mod, b_mod.reshape(depth, 1, n))


def _rope_tile(x, cos, sin, half):
    lane = lax.broadcasted_iota(jnp.int32, x.shape, 1)
    fwd = pltpu.roll(x, LANES - half, 1)
    bwd = pltpu.roll(x, half, 1)
    partner = jnp.where((lane % (2 * half)) < half, fwd, bwd)
    return x * cos + partner * sin


def _ffn_residual(h, mod_rows, y0, y1, info):
    ffn = y0.astype(F32) * info[:, 4:5] + y1.astype(F32) * info[:, 5:6]
    return h + mod_rows[5:6] * ffn


def _stream_specs(h, tm):
    if isinstance(h, tuple):
        lat, ctx = h
        d = lat.shape[1]
        nl = lat.shape[0] // tm
        return ([pl.BlockSpec((tm, d), lambda i, *_: (jnp.minimum(i, nl - 1), 0)),
                 pl.BlockSpec((tm, d), lambda i, *_: (jnp.maximum(i - nl, 0), 0))], [lat, ctx], nl)
    return [pl.BlockSpec((tm, h.shape[1]), lambda i, *_: (i, 0))], [h], None


def _stream_tile(refs, n_lat_tiles):
    if n_lat_tiles is None:
        return (lambda: refs[0][...]), refs[1:]
    is_lat = pl.program_id(0) < n_lat_tiles
    return (lambda: jnp.where(is_lat, refs[0][...], refs[1][...])), refs[2:]


def _inproj_kernel(tmod_ref, trope_ref, *refs, pending, n_lat_tiles):
    del tmod_ref, trope_ref
    load_x, refs = _stream_tile(refs, n_lat_tiles)
    if pending:
        y0_ref, y1_ref, info_ref, modp_ref = refs[:4]
        refs = refs[4:]
    (mod_ref, g_ref, w1_ref, qng_ref, kvng_ref, wuq_ref, wuk_ref, wuv_ref, ca_ref, sa_ref, cb_ref,
     sb_ref, qa_o, ka_o, va_o, qb_o, kb_o, vb_o, u_o, gs_o) = refs[:20]
    x = load_x()
    if pending:
        x = _ffn_residual(x, modp_ref[0], y0_ref[...], y1_ref[...], info_ref[...])
        refs[20][...] = x
    m = mod_ref[0]
    hx = _rms(x, g_ref[...]) * (1.0 + m[1:2]) + m[0:1]
    hb = hx.astype(BF16)

    def seg(name, width):
        off = _SEG[name]
        return _dot(hb, w1_ref[:, off:off + width])

    ca, sa = ca_ref[...], sa_ref[...]
    cb, sb = cb_ref[...], sb_ref[...]

    qa = seg("qa", A_WIDTH)
    for t in range(A_WIDTH // LANES):
        sl = slice(t * LANES, (t + 1) * LANES)
        qa_o[:, sl] = _rope_tile(qa[:, sl], ca, sa, A_HEAD_DIM // 2).astype(BF16)
    ka_o[...] = _rope_tile(seg("ka", A_KV_WIDTH), ca, sa, A_HEAD_DIM // 2).astype(BF16)
    va_o[...] = seg("va", A_KV_WIDTH).astype(BF16)

    cqn = _rms(seg("cq", Q_LORA), qng_ref[...]).astype(BF16)
    ckvn = _rms(seg("ckv", KV_LORA), kvng_ref[...]).astype(BF16)
    qb = _dot(cqn, wuq_ref[...])
    kb = _dot(ckvn, wuk_ref[...])
    kr = _rope_tile(seg("kr", LANES), cb, sb, B_ROPE // 2)
    qscale = (B_NOPE + B_ROPE) ** -0.5 * math.log2(math.e)
    for t in range(B_HEADS):
        sl = slice(t * LANES, (t + 1) * LANES)
        qb_o[:, sl] = (_rope_tile(qb[:, sl], cb, sb, B_ROPE // 2) * qscale).astype(BF16)
        kb_o[:, sl] = (kb[:, sl] + kr).astype(BF16)
    vb_o[...] = _dot(ckvn, wuv_ref[...]).T.astype(BF16)

    u_o[...] = seg("u", C_WIDTH).astype(BF16)
    ngate = gs_o.shape[1]
    for t in range(ngate // 512):
        off = W1_GATES_OFF + t * 512
        gs_o[:, t * 512:(t + 1) * 512] = jax.nn.sigmoid(_dot(hb, w1_ref[:, off:off + 512])).astype(BF16)


def _inproj(h_all, mod, norm_g, lw, tables, tile_mod, tile_rope, pending=None):
    tm = ROW_TILE
    lead_specs, lead_args, n_lat_tiles = _stream_specs(h_all, tm)
    t_all = sum(a.shape[0] for a in lead_args)
    d = lead_args[0].shape[1]
    nt = t_all // tm
    w1 = lw["w1"]
    ngate = N_BRANCH * d
    row = lambda i, *_: (i, 0)
    const = lambda i, *_: (0, 0)
    rope = lambda i, tmod, trope: (trope[i], 0)
    modrow = lambda i, tmod, trope: (tmod[i], 0, 0)
    widths = (A_WIDTH, A_KV_WIDTH, A_KV_WIDTH, B_QK_WIDTH, B_QK_WIDTH, B_WIDTH, C_WIDTH, ngate)
    out_specs = [pl.BlockSpec((B_WIDTH, tm), lambda i, *_: (0, i)) if k == VB_OUT
                 else pl.BlockSpec((tm, w), row) for k, w in enumerate(widths)]
    out_shape = [jax.ShapeDtypeStruct((B_WIDTH, t_all) if k == VB_OUT else (t_all, w), BF16)
                 for k, w in enumerate(widths)]
    if pending is not None:
        y0, y1, info, mod_prev = pending
        lead_specs += [pl.BlockSpec((tm, d), row), pl.BlockSpec((tm, d), row),
                       pl.BlockSpec((tm, ROUTER_LANES), row), pl.BlockSpec((1, 8, d), modrow)]
        lead_args += [y0, y1, info, mod_prev]
        out_specs.append(pl.BlockSpec((tm, d), row))
        out_shape.append(jax.ShapeDtypeStruct((t_all, d), F32))
    grid_spec = pltpu.PrefetchScalarGridSpec(
        num_scalar_prefetch=2,
        grid=(nt,),
        in_specs=lead_specs + [
                  pl.BlockSpec((1, 8, d), modrow),
                  pl.BlockSpec((1, d), const),
                  pl.BlockSpec(w1.shape, const),
                  pl.BlockSpec((1, Q_LORA), const),
                  pl.BlockSpec((1, KV_LORA), const),
                  pl.BlockSpec(lw["wuq"].shape, const),
                  pl.BlockSpec(lw["wuk"].shape, const),
                  pl.BlockSpec(lw["wuv"].shape, const),
                  pl.BlockSpec((tm, LANES), rope),
                  pl.BlockSpec((tm, LANES), rope),
                  pl.BlockSpec((tm, LANES), rope),
                  pl.BlockSpec((tm, LANES), rope)],
        out_specs=out_specs,
    )
    return pl.pallas_call(
        functools.partial(_inproj_kernel, pending=pending is not None, n_lat_tiles=n_lat_tiles),
        grid_spec=grid_spec,
        out_shape=out_shape,
        compiler_params=_cparams(("arbitrary",)),
        name="inproj",
    )(tile_mod, tile_rope, *lead_args, mod, norm_g, w1, lw["qng"], lw["kvng"], lw["wuq"], lw["wuk"],
      lw["wuv"], tables["ca"], tables["sa"], tables["cb"], tables["sb"])


def _gqa_kernel(sink_ref, q_ref, *refs, local, ntile):
    if local:
        kp_ref, kc_ref, kn_ref, vp_ref, vc_ref, vn_ref, kx_ref, vx_ref, o_ref = refs
    else:
        kx_ref, vx_ref, _, o_ref = refs
    tq = q_ref.shape[0]
    hd = A_HEAD_DIM
    sub = WINDOW if local else tq
    nsub = tq // sub
    rows = A_GROUP * sub
    lane = lax.broadcasted_iota(jnp.int32, (1, LANES), 1)
    head_row = lax.broadcasted_iota(jnp.int32, (rows, 1), 0) // sub
    sinks = []
    for g in range(A_KV_HEADS):
        col = jnp.zeros((rows, 1), F32)
        for j in range(A_GROUP):
            col = jnp.where(head_row == j, sink_ref[g * A_GROUP + j], col)
        sinks.append(col)
    kx, vx = kx_ref[...], vx_ref[...]
    if local:
        j_tile = pl.program_id(1)
        kloc = jnp.concatenate([kp_ref[...], kc_ref[...], kn_ref[...]], axis=0)
        vloc = jnp.concatenate([vp_ref[...], vc_ref[...], vn_ref[...]], axis=0)
        r = lax.broadcasted_iota(jnp.int32, (sub, 3 * WINDOW), 0)
        c = lax.broadcasted_iota(jnp.int32, (sub, 3 * WINDOW), 1)
        band = jnp.abs(c - WINDOW - r) <= WINDOW
    def block_scores(s):
        q = jnp.concatenate([q_ref[s * sub:(s + 1) * sub, j * LANES:(j + 1) * LANES]
                             for j in range(A_GROUP)], axis=0)
        if local:
            kcat = jnp.concatenate([kloc[s * sub:s * sub + 3 * WINDOW], kx], axis=0)
            vcat = jnp.concatenate([vloc[s * sub:s * sub + 3 * WINDOW], vx], axis=0)
            ok = band
            if s == 0:
                ok = jnp.logical_and(ok, jnp.logical_or(c >= WINDOW, j_tile > 0))
            if s == nsub - 1:
                ok = jnp.logical_and(ok, jnp.logical_or(c < 2 * WINDOW, j_tile < ntile - 1))
            bias = jnp.where(ok, 0.0, NEG_INF)
            bias = jnp.concatenate([bias] * A_GROUP, axis=0)
        else:
            kcat, vcat = kx, vx
        scs = []
        for g in range(A_KV_HEADS):
            in_g = jnp.logical_and(lane >= g * hd, lane < (g + 1) * hd)
            kg = jnp.where(in_g, kcat, jnp.zeros_like(kcat))
            sc = _dot_nt(q, kg)
            if local:
                sc = jnp.concatenate([sc[:, :3 * WINDOW] + bias, sc[:, 3 * WINDOW:]], axis=1)
            scs.append(sc)
        return scs, vcat

    def block_output(s, scs, vcat):
        outs = []
        for g in range(A_KV_HEADS):
            mx = jnp.maximum(jnp.max(scs[g], axis=-1, keepdims=True), sinks[g])
            e = jnp.exp(scs[g] - mx)
            denom = jnp.sum(e, axis=-1, keepdims=True) + jnp.exp(sinks[g] - mx)
            outs.append(_dot(e.astype(BF16), vcat) / denom)
        o = jnp.where(lane < hd, outs[0], outs[1]).astype(BF16)
        for j in range(A_GROUP):
            o_ref[s * sub:(s + 1) * sub, j * LANES:(j + 1) * LANES] = o[j * sub:(j + 1) * sub]

    pending = block_scores(0)
    for s in range(1, nsub):
        nxt = block_scores(s)
        block_output(s - 1, *pending)
        pending = nxt
    block_output(nsub - 1, *pending)


def _window_attn(qa, ka, va, sink, batch, n, lc, out_rows):
    t = batch * n
    tq = min(WIN_Q_TILE, n)
    ntile = n // tq
    sub = tq // WINDOW
    nblk = n // WINDOW
    cblk = t // lc
    qmap = lambda b, j: (b * ntile + j, 0)
    prev = lambda b, j: (b * nblk + jnp.maximum(j * sub - 1, 0), 0)
    nxt = lambda b, j: (b * nblk + jnp.minimum((j + 1) * sub, nblk - 1), 0)
    ctx = lambda b, j: (cblk + b, 0)
    kvw = A_KV_WIDTH
    return pl.pallas_call(
        functools.partial(_gqa_kernel, local=True, ntile=ntile),
        grid=(batch, ntile),
        in_specs=[pl.BlockSpec(memory_space=pltpu.SMEM),
                  pl.BlockSpec((tq, A_WIDTH), qmap),
                  pl.BlockSpec((WINDOW, kvw), prev), pl.BlockSpec((tq, kvw), qmap),
                  pl.BlockSpec((WINDOW, kvw), nxt),
                  pl.BlockSpec((WINDOW, kvw), prev), pl.BlockSpec((tq, kvw), qmap),
                  pl.BlockSpec((WINDOW, kvw), nxt),
                  pl.BlockSpec((lc, kvw), ctx), pl.BlockSpec((lc, kvw), ctx)],
        out_specs=pl.BlockSpec((tq, A_WIDTH), qmap),
        out_shape=jax.ShapeDtypeStruct((out_rows, A_WIDTH), BF16),
        compiler_params=_cparams(("arbitrary", "arbitrary")),
        name="window_attn",
    )(sink, qa, ka, ka, ka, va, va, va, ka, va)


def _ctx_gqa_attn(qa, ka, va, sink, oa, batch, n, lc):
    cblk = batch * n // lc
    cmap = lambda b: (cblk + b, 0)
    return pl.pallas_call(
        functools.partial(_gqa_kernel, local=False, ntile=0),
        grid=(batch,),
        in_specs=[pl.BlockSpec(memory_space=pltpu.SMEM),
                  pl.BlockSpec((lc, A_WIDTH), cmap),
                  pl.BlockSpec((lc, A_KV_WIDTH), cmap), pl.BlockSpec((lc, A_KV_WIDTH), cmap),
                  pl.BlockSpec(memory_space=pl.ANY)],
        out_specs=pl.BlockSpec((lc, A_WIDTH), cmap),
        out_shape=jax.ShapeDtypeStruct(oa.shape, BF16),
        input_output_aliases={4: 0},
        compiler_params=_cparams(("arbitrary",)),
        name="ctx_gqa_attn",
    )(sink, qa, ka, va, oa)


def _mla_kernel(q_ref, kc_ref, vc_ref, *refs, with_latent):
    if with_latent:
        kx_ref, vx_ref, o_ref, s_ref = refs
    else:
        _, o_ref, s_ref = refs
    tq = q_ref.shape[0]
    lc = kc_ref.shape[0]
    chunks = [(kc_ref, vc_ref, 0, lc, 0)]
    if with_latent:
        n = kx_ref.shape[0]
        for r0 in range(0, n, MLA_KEY_CHUNK):
            chunks.append((kx_ref, vx_ref, r0, min(MLA_KEY_CHUNK, n - r0), lc + r0))

    def fold(x, op):
        while x.shape[0] > 8:
            half = x.shape[0] // 2
            x = op(x[:half], x[half:])
        return x

    def scores(h):
        ql = slice(h * B_HEAD_PAD, (h + 1) * B_HEAD_PAD)
        qh = q_ref[:, ql]
        m8 = None
        for k_ref, _, r0, nr, c0 in chunks:
            st = _dot_nt(k_ref[r0:r0 + nr, ql], qh)
            s_ref[h % 2, c0:c0 + nr, :] = st
            part = fold(st, jnp.maximum)
            m8 = part if m8 is None else jnp.maximum(m8, part)
        return jnp.max(m8, axis=0, keepdims=True)

    def values(h, mx):
        vl = slice(h * B_VDIM, (h + 1) * B_VDIM)
        l8 = None
        acc = None
        for _, vt_ref, r0, nr, c0 in chunks:
            et = jnp.exp2(s_ref[h % 2, c0:c0 + nr, :] - mx)
            part = fold(et, jnp.add)
            l8 = part if l8 is None else l8 + part
            pv = _dot(vt_ref[vl, r0:r0 + nr], et.astype(BF16))
            acc = pv if acc is None else acc + pv
        return acc / jnp.sum(l8, axis=0, keepdims=True)

    outs = []
    mx_prev = scores(0)
    for h in range(1, B_HEADS):
        mx = scores(h)
        outs.append(values(h - 1, mx_prev))
        mx_prev = mx
    outs.append(values(B_HEADS - 1, mx_prev))
    o_ref[...] = jnp.concatenate(outs, axis=0).T.astype(BF16)


def _mla_attn(qb, kb, vb, batch, n, lc, out_rows):
    t = batch * n
    tq = min(MLA_Q_TILE, n)
    nq = n // tq
    cblk = t // lc
    qmap = lambda b, j: (b * nq + j, 0)
    ctx = lambda b, j: (cblk + b, 0)
    lat = lambda b, j: (b, 0)
    ctx_t = lambda b, j: (0, cblk + b)
    lat_t = lambda b, j: (0, b)
    return pl.pallas_call(
        functools.partial(_mla_kernel, with_latent=True),
        grid=(batch, nq),
        in_specs=[pl.BlockSpec((tq, B_QK_WIDTH), qmap),
                  pl.BlockSpec((lc, B_QK_WIDTH), ctx), pl.BlockSpec((B_WIDTH, lc), ctx_t),
                  pl.BlockSpec((n, B_QK_WIDTH), lat), pl.BlockSpec((B_WIDTH, n), lat_t)],
        out_specs=pl.BlockSpec((tq, B_WIDTH), qmap),
        out_shape=jax.ShapeDtypeStruct((out_rows, B_WIDTH), BF16),
        scratch_shapes=[pltpu.VMEM((2, lc + n, tq), F32)],
        compiler_params=_cparams(("arbitrary", "arbitrary")),
        name="mla_attn",
    )(qb, kb, vb, kb, vb)


def _mla_ctx_attn(qb, kb, vb, ob, batch, n, lc):
    cblk = batch * n // lc
    cmap = lambda b: (cblk + b, 0)
    return pl.pallas_call(
        functools.partial(_mla_kernel, with_latent=False),
        grid=(batch,),
        in_specs=[pl.BlockSpec((lc, B_QK_WIDTH), cmap),
                  pl.BlockSpec((lc, B_QK_WIDTH), cmap),
                  pl.BlockSpec((B_WIDTH, lc), lambda b: (0, cblk + b)),
                  pl.BlockSpec(memory_space=pl.ANY)],
        out_specs=pl.BlockSpec((lc, B_WIDTH), cmap),
        out_shape=jax.ShapeDtypeStruct(ob.shape, BF16),
        input_output_aliases={3: 0},
        scratch_shapes=[pltpu.VMEM((2, lc, lc), F32)],
        compiler_params=_cparams(("arbitrary",)),
        name="mla_ctx_attn",
    )(qb, kb, vb, ob)


def _route_tile(logits, run):
    tm = logits.shape[0]
    lane_i = lax.broadcasted_iota(jnp.int32, logits.shape, 1)
    lane = lane_i.astype(F32)
    nolane = float(LANES)

    def first_argmax(v):
        mx = jnp.max(v, axis=-1, keepdims=True)
        return mx, jnp.min(jnp.where(v == mx, lane, nolane), axis=-1, keepdims=True)

    gl = jnp.where(lane_i < N_GROUPS, logits, NEG_INF)
    gmax, g_sel = first_argmax(gl)
    g_p = 1.0 / jnp.sum(jnp.exp(gl - gmax), axis=-1, keepdims=True)
    lo = N_GROUPS + EXPERTS_PER_GROUP * g_sel
    el = jnp.where(jnp.logical_and(lane >= lo, lane < lo + EXPERTS_PER_GROUP), logits, NEG_INF)
    v1, i1 = first_argmax(el)
    sel1 = lane == i1
    v2, i2 = first_argmax(jnp.where(sel1, NEG_INF, el))
    sel2 = lane == i2
    t2 = jnp.exp(v2 - v1)
    gate1 = g_p / (1.0 + t2)
    gate2 = gate1 * t2

    onehot = jnp.where(jnp.logical_or(sel1, sel2), 1.0, 0.0)
    r_i = lax.broadcasted_iota(jnp.int32, (tm, tm), 0)
    c_i = lax.broadcasted_iota(jnp.int32, (tm, tm), 1)
    lower = jnp.where(c_i < r_i, 1.0, 0.0).astype(BF16)
    before = _dot(lower, onehot.astype(BF16)) + run
    rank1 = jnp.sum(jnp.where(sel1, before, 0.0), axis=-1, keepdims=True)
    rank2 = jnp.sum(jnp.where(sel2, before, 0.0), axis=-1, keepdims=True)
    info = jnp.zeros_like(logits)
    for k, val in enumerate((i1 - N_GROUPS, i2 - N_GROUPS, rank1, rank2, gate1, gate2)):
        info = jnp.where(lane_i == k, val, info)
    return info, run + jnp.sum(onehot, axis=0, keepdims=True)


def _merge_kernel(tmod_ref, tpos_ref, tlen_ref, *refs, n_lat_tiles):
    del tmod_ref
    load_h, refs = _stream_tile(refs, n_lat_tiles)
    (mod_ref, g_ref, oa_ref, ob_ref, u_ref, up_ref, un_ref, gs_ref, wpool_ref, pscale_ref, wa_ref,
     wb_ref, wc_ref, wo_ref, wr_ref, br_ref, hn_o, fx_o, info_o, cnt_o, run_ref) = refs
    i = pl.program_id(0)

    @pl.when(i == 0)
    def _():
        run_ref[...] = jnp.zeros_like(run_ref)

    pos0 = tpos_ref[i]
    seq_len = tlen_ref[i]
    tm = u_ref.shape[0]
    d = hn_o.shape[1]

    u = u_ref[...]
    kdim = tm + LANES
    zpad = jnp.zeros((LANES - 2 * POOL_HALO, u.shape[1]), BF16)
    ucat = jnp.concatenate([up_ref[...], u, un_ref[...], zpad], axis=0)
    uf = u.astype(F32)
    r_i = lax.broadcasted_iota(jnp.int32, (tm, kdim), 0)
    c_i = lax.broadcasted_iota(jnp.int32, (tm, kdim), 1)
    rel = c_i - POOL_HALO - r_i
    kpos = pos0 + c_i - POOL_HALO
    valid = jnp.logical_and(kpos >= 0, kpos < seq_len)
    tpos = pos0 + lax.broadcasted_iota(jnp.int32, (tm, 1), 0)
    win_sums = []
    for gi, w in enumerate(POOL_WINDOWS):
        rad = w // 2
        sl = slice(gi * C_GROUP_DIM, (gi + 1) * C_GROUP_DIM)
        band = jnp.where(jnp.logical_and(jnp.abs(rel) <= rad, valid), 1.0, 0.0).astype(BF16)
        win_sums.append(_dot(band, ucat[:, sl]))
    ya = _dot(oa_ref[...], wa_ref[...])
    oc_parts = []
    for gi, w in enumerate(POOL_WINDOWS):
        rad = w // 2
        sl = slice(gi * C_GROUP_DIM, (gi + 1) * C_GROUP_DIM)
        cnt = (jnp.minimum(tpos + rad + 1, seq_len) - jnp.maximum(tpos - rad, 0)).astype(F32)
        pooled = (win_sums[gi] / cnt - uf[:, sl]).astype(BF16)
        oc_parts.append(_dot(pooled, wpool_ref[gi]))
    yb = _dot(ob_ref[...], wb_ref[...])
    oc = (jnp.concatenate(oc_parts, axis=1) * pscale_ref[...]).astype(BF16)

    y = gs_ref[:, 0:d].astype(F32) * ya
    y = y + gs_ref[:, d:2 * d].astype(F32) * yb
    y = y + gs_ref[:, 2 * d:3 * d].astype(F32) * _dot(oc, wc_ref[...])
    mix = _dot(y.astype(BF16), wo_ref[...])

    m = mod_ref[0]
    hn = load_h() + m[2:3] * mix
    hn_o[...] = hn
    fx = _rms(hn, g_ref[...]) * (1.0 + m[4:5]) + m[3:4]
    fx_hi = fx.astype(BF16)
    fx_lo = (fx - fx_hi.astype(F32)).astype(BF16)
    fx_o[...] = fx_hi
    both = _dot(fx_hi, wr_ref[...])
    logits = (both[:, :ROUTER_LANES] + _dot(fx_lo, wr_ref[:, :ROUTER_LANES])
              + both[:, ROUTER_LANES:]) + br_ref[...]
    info, run = _route_tile(logits, run_ref[...])
    info_o[...] = info
    run_ref[...] = run
    cnt_o[...] = run


def _merge(h_all, mod, norm_g, oa, ob, u, gs, lw, rows, tile_mod, tile_pos, tile_len):
    tm = ROW_TILE
    stream_specs, stream_args, n_lat_tiles = _stream_specs(h_all, tm)
    d = stream_args[0].shape[1]
    nt = rows // tm
    hb = tm // POOL_HALO
    nhalo = u.shape[0] // POOL_HALO
    row = lambda i, *_: (i, 0)
    const = lambda i, *_: (0, 0)
    const3 = lambda i, *_: (0, 0, 0)
    grid_spec = pltpu.PrefetchScalarGridSpec(
        num_scalar_prefetch=3,
        grid=(nt,),
        in_specs=stream_specs + [
                  pl.BlockSpec((1, 8, d), lambda i, tmod, *_: (tmod[i], 0, 0)),
                  pl.BlockSpec((1, d), const),
                  pl.BlockSpec((tm, A_WIDTH), row),
                  pl.BlockSpec((tm, B_WIDTH), row),
                  pl.BlockSpec((tm, C_WIDTH), row),
                  pl.BlockSpec((POOL_HALO, C_WIDTH), lambda i, *_: (jnp.maximum(i * hb - 1, 0), 0)),
                  pl.BlockSpec((POOL_HALO, C_WIDTH), lambda i, *_: (jnp.minimum((i + 1) * hb, nhalo - 1), 0)),
                  pl.BlockSpec((tm, N_BRANCH * d), row),
                  pl.BlockSpec(lw["wpool"].shape, const3),
                  pl.BlockSpec((1, C_WIDTH), const),
                  pl.BlockSpec(lw["wa"].shape, const),
                  pl.BlockSpec(lw["wb"].shape, const),
                  pl.BlockSpec(lw["wc"].shape, const),
                  pl.BlockSpec(lw["wo"].shape, const),
                  pl.BlockSpec(lw["wr"].shape, const),
                  pl.BlockSpec((1, ROUTER_LANES), const)],
        out_specs=[pl.BlockSpec((tm, d), row), pl.BlockSpec((tm, d), row),
                   pl.BlockSpec((tm, ROUTER_LANES), row), pl.BlockSpec((1, ROUTER_LANES), const)],
        scratch_shapes=[pltpu.VMEM((1, ROUTER_LANES), F32)],
    )
    return pl.pallas_call(
        functools.partial(_merge_kernel, n_lat_tiles=n_lat_tiles),
        grid_spec=grid_spec,
        out_shape=[jax.ShapeDtypeStruct((rows, d), F32), jax.ShapeDtypeStruct((rows, d), BF16),
                   jax.ShapeDtypeStruct((rows, ROUTER_LANES), F32),
                   jax.ShapeDtypeStruct((1, ROUTER_LANES), F32)],
        compiler_params=_cparams(("arbitrary",)),
        name="merge",
    )(tile_mod, tile_pos, tile_len, *stream_args, mod, norm_g, oa, ob, u, u, u, gs, lw["wpool"],
      lw["pscale"], lw["wa"], lw["wb"], lw["wc"], lw["wo"], lw["wr"], lw["br"])


def _moe_kernel(blk_e_ref, nused_ref, x_ref, wgu_ref, wdn_ref, *refs, first_block):
    del blk_e_ref
    o_ref = refs[-1]
    i = first_block + pl.program_id(0)

    @pl.when(i < nused_ref[0])
    def _():
        gu = _dot(x_ref[...], wgu_ref[0, 0].astype(BF16))
        gt, up = gu[:, :EXPERT_FF], gu[:, EXPERT_FF:]
        act = (gt * jax.nn.sigmoid(gt) * up).astype(BF16)
        o_ref[...] = _dot(act, wdn_ref[0, 0].astype(BF16)).astype(o_ref.dtype)

    @pl.when(i >= nused_ref[0])
    def _():
        o_ref[...] = jnp.zeros_like(o_ref)


def _moe(xg, blk_e, n_used, w_gu, w_dn, layer, first_block, out):
    d = xg.shape[1]
    nb = xg.shape[0] // MOE_BLOCK
    total = blk_e.shape[0] * MOE_BLOCK
    in_specs = [pl.BlockSpec((MOE_BLOCK, d), lambda i, *_: (i, 0)),
                pl.BlockSpec((1, 1, d, 2 * EXPERT_FF),
                             lambda i, be, nu: (layer, be[first_block + i], 0, 0)),
                pl.BlockSpec((1, 1, EXPERT_FF, d),
                             lambda i, be, nu: (layer, be[first_block + i], 0, 0))]
    args = [blk_e, n_used, xg, w_gu, w_dn]
    aliases = {}
    if out is not None:
        in_specs.append(pl.BlockSpec(memory_space=pl.ANY))
        args.append(out)
        aliases = {len(args) - 1: 0}
    grid_spec = pltpu.PrefetchScalarGridSpec(
        num_scalar_prefetch=2,
        grid=(nb,),
        in_specs=in_specs,
        out_specs=pl.BlockSpec((MOE_BLOCK, d), lambda i, *_: (first_block + i, 0)),
    )
    return pl.pallas_call(
        functools.partial(_moe_kernel, first_block=first_block),
        grid_spec=grid_spec,
        out_shape=jax.ShapeDtypeStruct((total, d), BF16),
        input_output_aliases=aliases,
        compiler_params=_cparams(("arbitrary",)),
        name="moe_experts",
    )(*args)


def _final_kernel(tmod_ref, h_ref, mod_ref, y0_ref, y1_ref, info_ref, fg_ref, o_ref):
    del tmod_ref
    hn = _ffn_residual(h_ref[...], mod_ref[0], y0_ref[...], y1_ref[...], info_ref[...])
    o_ref[...] = _rms(hn, fg_ref[...])


def _final(h, mod, y0, y1, info, final_g, tile_mod):
    rows, d = h.shape
    tm = ROW_TILE
    row = lambda i, *_: (i, 0)
    grid_spec = pltpu.PrefetchScalarGridSpec(
        num_scalar_prefetch=1,
        grid=(rows // tm,),
        in_specs=[pl.BlockSpec((tm, d), row),
                  pl.BlockSpec((1, 8, d), lambda i, tmod: (tmod[i], 0, 0)),
                  pl.BlockSpec((tm, d), row), pl.BlockSpec((tm, d), row),
                  pl.BlockSpec((tm, ROUTER_LANES), row),
                  pl.BlockSpec((1, d), lambda i, *_: (0, 0))],
        out_specs=pl.BlockSpec((tm, d), row),
    )
    return pl.pallas_call(
        _final_kernel,
        grid_spec=grid_spec,
        out_shape=jax.ShapeDtypeStruct((rows, d), F32),
        compiler_params=_cparams(("arbitrary",)),
        name="final_residual_norm",
    )(tile_mod, h, mod, y0, y1, info, final_g)


def _deinterleave(n):
    return np.concatenate([np.arange(0, n, 2), np.arange(1, n, 2)])


def _rope_tables(n, tm):
    rows = n // GRID_W
    row = np.repeat(np.arange(rows), GRID_W).astype(np.float32)
    col = np.tile(np.arange(GRID_W), rows).astype(np.float32)

    def cs(rot_dim):
        axis_dim = rot_dim // 2
        inv = jnp.asarray(ROPE_BASE, F32) ** (-jnp.arange(0, axis_dim, 2, dtype=F32) / axis_dim)
        ang = jnp.concatenate([jnp.asarray(row)[:, None] * inv, jnp.asarray(col)[:, None] * inv], axis=-1)
        return jnp.cos(ang), jnp.sin(ang)

    cos_a, sin_a = cs(A_HEAD_DIM)
    ca = jnp.tile(jnp.concatenate([cos_a, cos_a], axis=1), (1, LANES // A_HEAD_DIM))
    sa = jnp.tile(jnp.concatenate([-sin_a, sin_a], axis=1), (1, LANES // A_HEAD_DIM))
    cos_b, sin_b = cs(B_ROPE)
    ones = jnp.ones((n, B_NOPE), F32)
    tail = LANES - B_NOPE - B_ROPE
    cb = jnp.concatenate([ones, cos_b, cos_b, jnp.ones((n, tail), F32)], axis=1)
    sb = jnp.concatenate([0 * ones, -sin_b, sin_b, jnp.zeros((n, tail), F32)], axis=1)
    ident_c = jnp.ones((tm, LANES), F32)
    ident_s = jnp.zeros((tm, LANES), F32)
    return {"ca": jnp.concatenate([ca, ident_c]), "sa": jnp.concatenate([sa, ident_s]),
            "cb": jnp.concatenate([cb, ident_c]), "sb": jnp.concatenate([sb, ident_s])}


def _layer_weights(l, w_in, q_norm_g, w_uq, kv_norm_g, w_ukv, w_pool, pool_scale, w_br_a, w_br_b,
                   w_br_c, w_out, w_rg, b_rg, w_re, b_re):
    d = w_in.shape[1]
    wi = w_in[l]
    splits = np.cumsum([A_WIDTH, A_KV_WIDTH, A_KV_WIDTH, Q_LORA, KV_LORA, B_ROPE, C_WIDTH])
    qa, ka, va, cq, ckv, kr, u, gates = jnp.split(wi, splits, axis=1)
    pa = _deinterleave(A_HEAD_DIM)
    ho = np.arange(A_HEADS).reshape(A_KV_HEADS, A_GROUP).T.reshape(-1)
    qa = qa.reshape(d, A_HEADS, A_HEAD_DIM)[:, ho][:, :, pa].reshape(d, A_WIDTH) * (A_HEAD_DIM ** -0.5)
    wa = w_br_a[l].reshape(A_HEADS, A_HEAD_DIM, -1)[ho].reshape(A_WIDTH, -1)
    ka = ka.reshape(d, A_KV_HEADS, A_HEAD_DIM)[:, :, pa].reshape(d, A_KV_WIDTH)
    pb = _deinterleave(B_ROPE)
    tail = LANES - B_NOPE - B_ROPE
    kr128 = jnp.concatenate([jnp.zeros((d, B_NOPE), F32), kr[:, pb], jnp.zeros((d, tail), F32)], axis=1)
    w1 = jnp.concatenate([qa, ka, va, cq, ckv, kr128, u, gates], axis=1).astype(BF16)

    uq = w_uq[l].reshape(Q_LORA, B_HEADS, B_NOPE + B_ROPE)
    uq = jnp.concatenate([uq[:, :, :B_NOPE], uq[:, :, B_NOPE:][:, :, pb],
                          jnp.zeros((Q_LORA, B_HEADS, tail), F32)], axis=2)
    ukv = w_ukv[l].reshape(KV_LORA, B_HEADS, B_NOPE + B_VDIM)
    uk = jnp.concatenate([ukv[:, :, :B_NOPE], jnp.zeros((KV_LORA, B_HEADS, LANES - B_NOPE), F32)], axis=2)
    uv = ukv[:, :, B_NOPE:]

    wr = jnp.concatenate([w_rg[l], w_re[l], jnp.zeros((d, ROUTER_LANES - N_GROUPS - N_EXPERTS), F32)], axis=1)
    wr_hi = wr.astype(BF16)
    wr_lo = (wr - wr_hi.astype(F32)).astype(BF16)
    br = jnp.concatenate([b_rg[l], b_re[l], jnp.zeros((ROUTER_LANES - N_GROUPS - N_EXPERTS,), F32)])
    return {
        "w1": w1,
        "qng": q_norm_g[l].reshape(1, Q_LORA), "kvng": kv_norm_g[l].reshape(1, KV_LORA),
        "wuq": uq.reshape(Q_LORA, B_QK_WIDTH).astype(BF16),
        "wuk": uk.reshape(KV_LORA, B_QK_WIDTH).astype(BF16),
        "wuv": uv.reshape(KV_LORA, B_WIDTH).astype(BF16),
        "wpool": w_pool[l].astype(BF16), "pscale": pool_scale[l].reshape(1, C_WIDTH),
        "wa": wa.astype(BF16), "wb": w_br_b[l].astype(BF16), "wc": w_br_c[l].astype(BF16),
        "wo": w_out[l].astype(BF16),
        "wr": jnp.concatenate([wr_hi, wr_lo], axis=1), "br": br.reshape(1, ROUTER_LANES),
    }


def _block_layout(info, cnt):
    t = info.shape[0]
    a = t * TOP_K
    expert = info[:, 0:TOP_K].astype(jnp.int32)
    rank = info[:, TOP_K:2 * TOP_K].astype(jnp.int32)
    counts = cnt[0, N_GROUPS:N_GROUPS + N_EXPERTS].astype(jnp.int32)
    padded = (counts + MOE_BLOCK - 1) // MOE_BLOCK * MOE_BLOCK
    pad_end = jnp.cumsum(padded)
    pad_start = pad_end - padded
    eids = jnp.arange(N_EXPERTS, dtype=jnp.int32)
    base = jnp.sum(jnp.where(expert[:, :, None] == eids, pad_start, 0), axis=-1)
    slot = base + rank
    n_blocks = -(-(a + N_EXPERTS * (MOE_BLOCK - 1)) // MOE_BLOCK)
    blk_start = jnp.arange(n_blocks, dtype=jnp.int32) * MOE_BLOCK
    blk_e = jnp.minimum(jnp.sum((pad_end[None, :] <= blk_start[:, None]).astype(jnp.int32), axis=1),
                        N_EXPERTS - 1)
    n_used = (pad_end[-1] // MOE_BLOCK).reshape(1)

    by_slot = jnp.argsort(slot.reshape(a)).astype(jnp.int32) // TOP_K
    first = jnp.cumsum(counts) - counts
    blk_sel = blk_e[:, None] == eids[None, :]
    blk_off = blk_start - jnp.sum(jnp.where(blk_sel, pad_start, 0), axis=1)
    blk_src = jnp.sum(jnp.where(blk_sel, first, 0), axis=1) + blk_off
    blk_cnt = jnp.sum(jnp.where(blk_sel, counts, 0), axis=1) - blk_off
    within = jnp.arange(MOE_BLOCK, dtype=jnp.int32)[None, :]
    src = jnp.minimum(blk_src[:, None] + within, a - 1)
    filler = (blk_start[:, None] + within) % t
    tok_of_slot = jnp.where(within < blk_cnt[:, None], by_slot.at[src].get(mode="promise_in_bounds"),
                            filler)
    return slot, tok_of_slot.reshape(n_blocks * MOE_BLOCK), blk_e, n_used


def kernel(x, c, ctx, c_ctx, w_mod, b_mod, norm_mix_g, norm_ffn_g, w_in, sink, q_norm_g, w_uq, kv_norm_g,
           w_ukv, w_pool, pool_scale, w_br_a, w_br_b, w_br_c, w_out, w_rg, b_rg, w_re, b_re, w_gu, w_dn,
           final_g):
    batch, n, d = x.shape
    lc = ctx.shape[1]
    depth = w_mod.shape[0]
    tm = ROW_TILE
    assert n % tm == 0 and lc % tm == 0 and n % WINDOW == 0 and (batch * n) % lc == 0
    t = batch * n
    tc = batch * lc
    nt_lat, nt_ctx = t // tm, tc // tm
    per_b, per_c = n // tm, lc // tm

    lat_i = np.arange(nt_lat)
    ctx_i = np.arange(nt_ctx)
    tile_mod = jnp.asarray(np.concatenate([lat_i // per_b, np.full(nt_ctx, batch)]), jnp.int32)
    tile_rope = jnp.asarray(np.concatenate([lat_i % per_b, np.full(nt_ctx, per_b)]), jnp.int32)
    tile_pos = jnp.asarray(np.concatenate([(lat_i % per_b) * tm, (ctx_i % per_c) * tm]), jnp.int32)
    tile_len = jnp.asarray(np.concatenate([np.full(nt_lat, n), np.full(nt_ctx, lc)]), jnp.int32)
    tables = _rope_tables(n, tm)

    mod_rows = 16
    cvec = jnp.concatenate([c, c_ctx[None, :], jnp.zeros((mod_rows - batch - 1, d), F32)], axis=0)
    h_all = (x.reshape(t, d), ctx.reshape(tc, d))

    mod_all = _modvec(cvec, w_mod, b_mod).reshape(depth, mod_rows, 6, d)
    mod_all = jnp.concatenate([mod_all, jnp.zeros((depth, mod_rows, 2, d), F32)], axis=2)

    out = None
    pending = None
    for l in range(depth):
        last = l == depth - 1
        lw = _layer_weights(l, w_in, q_norm_g, w_uq, kv_norm_g, w_ukv, w_pool, pool_scale, w_br_a,
                            w_br_b, w_br_c, w_out, w_rg, b_rg, w_re, b_re)
        mod = mod_all[l]

        proj = _inproj(h_all, mod, norm_mix_g[l].reshape(1, d), lw, tables, tile_mod, tile_rope,
                       pending)
        qa, ka, va, qb, kb, vb, u, gs = proj[:8]
        if pending is not None:
            h_all = proj[8]
        rows = t if last else t + tc
        oa = _window_attn(qa, ka, va, sink[l], batch, n, lc, rows)
        ob = _mla_attn(qb, kb, vb, batch, n, lc, rows)
        if not last:
            oa = _ctx_gqa_attn(qa, ka, va, sink[l], oa, batch, n, lc)
            ob = _mla_ctx_attn(qb, kb, vb, ob, batch, n, lc)
        hn, fx, info, cnt = _merge(h_all, mod, norm_ffn_g[l].reshape(1, d), oa, ob, u, gs, lw, rows,
                                   tile_mod, tile_pos, tile_len)

        slot, tok_of_slot, blk_e, n_used = _block_layout(info, cnt)
        split = (blk_e.shape[0] // 2) * MOE_BLOCK
        yb = None
        for lo, hi in ((0, split), (split, tok_of_slot.shape[0])):
            xg = fx.at[tok_of_slot[lo:hi]].get(mode="promise_in_bounds")
            yb = _moe(xg, blk_e, n_used, w_gu, w_dn, l, lo // MOE_BLOCK, yb)
        y0 = yb.at[slot[:, 0]].get(mode="promise_in_bounds")
        y1 = yb.at[slot[:, 1]].get(mode="promise_in_bounds")
        if last:
            out = _final(hn, mod, y0, y1, info, final_g.reshape(1, d), tile_mod)
        else:
            h_all, pending = hn, (y0, y1, info, mod)
    return out.reshape(batch, n, d)
```

```python
import functools
import math

import jax
import jax.numpy as jnp
import numpy as np
from jax import lax
from jax.experimental import pallas as pl
from jax.experimental.pallas import tpu as pltpu

GRID_W = 64
ROPE_BASE = 10000.0
EPS = 1e-6
NEG_INF = -1e30

A_HEADS = 8
A_KV_HEADS = 2
A_GROUP = A_HEADS // A_KV_HEADS
A_HEAD_DIM = 64
A_WIDTH = A_HEADS * A_HEAD_DIM
A_KV_WIDTH = A_KV_HEADS * A_HEAD_DIM
WINDOW = 128

B_HEADS = 8
B_NOPE = 64
B_ROPE = 32
B_VDIM = 64
B_WIDTH = B_HEADS * B_VDIM
Q_LORA = 256
KV_LORA = 256

POOL_WINDOWS = (2, 4, 8, 16)
C_GROUPS = 4
C_GROUP_DIM = 128
C_WIDTH = C_GROUPS * C_GROUP_DIM

N_BRANCH = 3
N_GROUPS = 4
EXPERTS_PER_GROUP = 8
N_EXPERTS = N_GROUPS * EXPERTS_PER_GROUP
TOP_K = 2
EXPERT_FF = 256

LANES = 128
BF16_SUBLANES = 16
VMEM_LIMIT_BYTES = 56 * 1024 * 1024

B_HEAD_PAD = LANES
B_QK_WIDTH = B_HEADS * B_HEAD_PAD
POOL_HALO = BF16_SUBLANES
ROW_TILE = 256
MLA_Q_TILE = 256
MLA_KEY_CHUNK = 256
WIN_Q_TILE = 512
MOE_BLOCK = 512
ROUTER_LANES = LANES

_SEG = {}
_off = 0
for _name, _w in (("qa", A_WIDTH), ("ka", A_KV_WIDTH), ("va", A_KV_WIDTH), ("cq", Q_LORA),
                  ("ckv", KV_LORA), ("kr", LANES), ("u", C_WIDTH), ("gates", None)):
    _SEG[_name] = _off
    if _w is not None:
        _off += _w
W1_GATES_OFF = _SEG["gates"]
assert A_KV_HEADS * A_HEAD_DIM == LANES
VB_OUT = 5

F32 = jnp.float32
BF16 = jnp.bfloat16


def _dot(a, b):
    return jnp.dot(a, b, preferred_element_type=F32)


def _dot_nt(a, b):
    return lax.dot_general(a, b, (((1,), (1,)), ((), ())), preferred_element_type=F32)


def _cparams(sem):
    return pltpu.CompilerParams(dimension_semantics=sem, vmem_limit_bytes=VMEM_LIMIT_BYTES)


def _rms(x, g):
    return x * lax.rsqrt(jnp.mean(x * x, axis=-1, keepdims=True) + EPS) * g


def _modvec_kernel(c_ref, w_ref, b_ref, o_ref):
    c = c_ref[...]
    a = (c * jax.nn.sigmoid(c)).astype(BF16)
    o_ref[0] = _dot(a, w_ref[0].astype(BF16)) + b_ref[0]


def _modvec(cvec, w_mod, b_mod):
    rows, d = cvec.shape
    depth, _, n = w_mod.shape
    tn = d
    return pl.pallas_call(
        _modvec_kernel,
        grid=(depth, n // tn),
        in_specs=[pl.BlockSpec((rows, d), lambda l, j: (0, 0)),
                  pl.BlockSpec((1, d, tn), lambda l, j: (l, 0, j)),
                  pl.BlockSpec((1, 1, tn), lambda l, j: (l, 0, j))],
        out_specs=pl.BlockSpec((1, rows, tn), lambda l, j: (l, 0, j)),
        out_shape=jax.ShapeDtypeStruct((depth, rows, n), F32),
        compiler_params=_cparams(("arbitrary", "arbitrary")),
        name="modvec",
    )(cvec, w_mod, b_mod.reshape(depth, 1, n))


def _rope_tile(x, cos, sin, half):
    lane = lax.broadcasted_iota(jnp.int32, x.shape, 1)
    fwd = pltpu.roll(x, LANES - half, 1)
    bwd = pltpu.roll(x, half, 1)
    partner = jnp.where((lane % (2 * half)) < half, fwd, bwd)
    return x * cos + partner * sin


def _ffn_residual(h, mod_rows, y0, y1, info):
    ffn = y0.astype(F32) * info[:, 4:5] + y1.astype(F32) * info[:, 5:6]
    return h + mod_rows[5:6] * ffn


def _stream_specs(h, tm):
    if isinstance(h, tuple):
        lat, ctx = h
        d = lat.shape[1]
        nl = lat.shape[0] // tm
        return ([pl.BlockSpec((tm, d), lambda i, *_: (jnp.minimum(i, nl - 1), 0)),
                 pl.BlockSpec((tm, d), lambda i, *_: (jnp.maximum(i - nl, 0), 0))], [lat, ctx], nl)
    return [pl.BlockSpec((tm, h.shape[1]), lambda i, *_: (i, 0))], [h], None


def _stream_tile(refs, n_lat_tiles):
    if n_lat_tiles is None:
        return (lambda: refs[0][...]), refs[1:]
    is_lat = pl.program_id(0) < n_lat_tiles
    return (lambda: jnp.where(is_lat, refs[0][...], refs[1][...])), refs[2:]


def _inproj_kernel(tmod_ref, trope_ref, *refs, pending, n_lat_tiles):
    del tmod_ref, trope_ref
    load_x, refs = _stream_tile(refs, n_lat_tiles)
    if pending:
        y0_ref, y1_ref, info_ref, modp_ref = refs[:4]
        refs = refs[4:]
    (mod_ref, g_ref, w1_ref, qng_ref, kvng_ref, wuq_ref, wuk_ref, wuv_ref, ca_ref, sa_ref, cb_ref,
     sb_ref, qa_o, ka_o, va_o, qb_o, kb_o, vb_o, u_o, gs_o) = refs[:20]
    x = load_x()
    if pending:
        x = _ffn_residual(x, modp_ref[0], y0_ref[...], y1_ref[...], info_ref[...])
        refs[20][...] = x
    m = mod_ref[0]
    hx = _rms(x, g_ref[...]) * (1.0 + m[1:2]) + m[0:1]
    hb = hx.astype(BF16)

    def seg(name, width):
        off = _SEG[name]
        return _dot(hb, w1_ref[:, off:off + width])

    ca, sa = ca_ref[...], sa_ref[...]
    cb, sb = cb_ref[...], sb_ref[...]

    qa = seg("qa", A_WIDTH)
    for t in range(A_WIDTH // LANES):
        sl = slice(t * LANES, (t + 1) * LANES)
        qa_o[:, sl] = _rope_tile(qa[:, sl], ca, sa, A_HEAD_DIM // 2).astype(BF16)
    ka_o[...] = _rope_tile(seg("ka", A_KV_WIDTH), ca, sa, A_HEAD_DIM // 2).astype(BF16)
    va_o[...] = seg("va", A_KV_WIDTH).astype(BF16)

    cqn = _rms(seg("cq", Q_LORA), qng_ref[...]).astype(BF16)
    ckvn = _rms(seg("ckv", KV_LORA), kvng_ref[...]).astype(BF16)
    qb = _dot(cqn, wuq_ref[...])
    kb = _dot(ckvn, wuk_ref[...])
    kr = _rope_tile(seg("kr", LANES), cb, sb, B_ROPE // 2)
    qscale = (B_NOPE + B_ROPE) ** -0.5 * math.log2(math.e)
    for t in range(B_HEADS):
        sl = slice(t * LANES, (t + 1) * LANES)
        qb_o[:, sl] = (_rope_tile(qb[:, sl], cb, sb, B_ROPE // 2) * qscale).astype(BF16)
        kb_o[:, sl] = (kb[:, sl] + kr).astype(BF16)
    vb_o[...] = _dot(ckvn, wuv_ref[...]).T.astype(BF16)

    u_o[...] = seg("u", C_WIDTH).astype(BF16)
    ngate = gs_o.shape[1]
    for t in range(ngate // 512):
        off = W1_GATES_OFF + t * 512
        gs_o[:, t * 512:(t + 1) * 512] = jax.nn.sigmoid(_dot(hb, w1_ref[:, off:off + 512])).astype(BF16)


def _inproj(h_all, mod, norm_g, lw, tables, tile_mod, tile_rope, pending=None):
    tm = ROW_TILE
    lead_specs, lead_args, n_lat_tiles = _stream_specs(h_all, tm)
    t_all = sum(a.shape[0] for a in lead_args)
    d = lead_args[0].shape[1]
    nt = t_all // tm
    w1 = lw["w1"]
    ngate = N_BRANCH * d
    row = lambda i, *_: (i, 0)
    const = lambda i, *_: (0, 0)
    rope = lambda i, tmod, trope: (trope[i], 0)
    modrow = lambda i, tmod, trope: (tmod[i], 0, 0)
    widths = (A_WIDTH, A_KV_WIDTH, A_KV_WIDTH, B_QK_WIDTH, B_QK_WIDTH, B_WIDTH, C_WIDTH, ngate)
    out_specs = [pl.BlockSpec((B_WIDTH, tm), lambda i, *_: (0, i)) if k == VB_OUT
                 else pl.BlockSpec((tm, w), row) for k, w in enumerate(widths)]
    out_shape = [jax.ShapeDtypeStruct((B_WIDTH, t_all) if k == VB_OUT else (t_all, w), BF16)
                 for k, w in enumerate(widths)]
    if pending is not None:
        y0, y1, info, mod_prev = pending
        lead_specs += [pl.BlockSpec((tm, d), row), pl.BlockSpec((tm, d), row),
                       pl.BlockSpec((tm, ROUTER_LANES), row), pl.BlockSpec((1, 8, d), modrow)]
        lead_args += [y0, y1, info, mod_prev]
        out_specs.append(pl.BlockSpec((tm, d), row))
        out_shape.append(jax.ShapeDtypeStruct((t_all, d), F32))
    grid_spec = pltpu.PrefetchScalarGridSpec(
        num_scalar_prefetch=2,
        grid=(nt,),
        in_specs=lead_specs + [
                  pl.BlockSpec((1, 8, d), modrow),
                  pl.BlockSpec((1, d), const),
                  pl.BlockSpec(w1.shape, const),
                  pl.BlockSpec((1, Q_LORA), const),
                  pl.BlockSpec((1, KV_LORA), const),
                  pl.BlockSpec(lw["wuq"].shape, const),
                  pl.BlockSpec(lw["wuk"].shape, const),
                  pl.BlockSpec(lw["wuv"].shape, const),
                  pl.BlockSpec((tm, LANES), rope),
                  pl.BlockSpec((tm, LANES), rope),
                  pl.BlockSpec((tm, LANES), rope),
                  pl.BlockSpec((tm, LANES), rope)],
        out_specs=out_specs,
    )
    return pl.pallas_call(
        functools.partial(_inproj_kernel, pending=pending is not None, n_lat_tiles=n_lat_tiles),
        grid_spec=grid_spec,
        out_shape=out_shape,
        compiler_params=_cparams(("arbitrary",)),
        name="inproj",
    )(tile_mod, tile_rope, *lead_args, mod, norm_g, w1, lw["qng"], lw["kvng"], lw["wuq"], lw["wuk"],
      lw["wuv"], tables["ca"], tables["sa"], tables["cb"], tables["sb"])


def _gqa_kernel(sink_ref, q_ref, *refs, local, ntile):
    if local:
        kp_ref, kc_ref, kn_ref, vp_ref, vc_ref, vn_ref, kx_ref, vx_ref = refs[:8]
        o_ref = refs[-1]
    else:
        kx_ref, vx_ref, _, o_ref = refs
    tq = q_ref.shape[0]
    hd = A_HEAD_DIM
    sub = WINDOW if local else tq
    nsub = tq // sub
    rows = A_GROUP * sub
    lane = lax.broadcasted_iota(jnp.int32, (1, LANES), 1)
    head_row = lax.broadcasted_iota(jnp.int32, (rows, 1), 0) // sub
    sinks = []
    for g in range(A_KV_HEADS):
        col = jnp.zeros((rows, 1), F32)
        for j in range(A_GROUP):
            col = jnp.where(head_row == j, sink_ref[g * A_GROUP + j], col)
        sinks.append(col)
    kx, vx = kx_ref[...], vx_ref[...]
    if local:
        j_tile = pl.program_id(1)
        kloc = jnp.concatenate([kp_ref[...], kc_ref[...], kn_ref[...]], axis=0)
        vloc = jnp.concatenate([vp_ref[...], vc_ref[...], vn_ref[...]], axis=0)
        r = lax.broadcasted_iota(jnp.int32, (sub, 3 * WINDOW), 0)
        c = lax.broadcasted_iota(jnp.int32, (sub, 3 * WINDOW), 1)
        band = jnp.abs(c - WINDOW - r) <= WINDOW
    def block_scores(s):
        q = jnp.concatenate([q_ref[s * sub:(s + 1) * sub, j * LANES:(j + 1) * LANES]
                             for j in range(A_GROUP)], axis=0)
        if local:
            kcat = jnp.concatenate([kloc[s * sub:s * sub + 3 * WINDOW], kx], axis=0)
            vcat = jnp.concatenate([vloc[s * sub:s * sub + 3 * WINDOW], vx], axis=0)
            ok = band
            if s == 0:
                ok = jnp.logical_and(ok, jnp.logical_or(c >= WINDOW, j_tile > 0))
            if s == nsub - 1:
                ok = jnp.logical_and(ok, jnp.logical_or(c < 2 * WINDOW, j_tile < ntile - 1))
            bias = jnp.where(ok, 0.0, NEG_INF)
            bias = jnp.concatenate([bias] * A_GROUP, axis=0)
        else:
            kcat, vcat = kx, vx
        scs = []
        for g in range(A_KV_HEADS):
            in_g = jnp.logical_and(lane >= g * hd, lane < (g + 1) * hd)
            kg = jnp.where(in_g, kcat, jnp.zeros_like(kcat))
            sc = _dot_nt(q, kg)
            if local:
                sc = jnp.concatenate([sc[:, :3 * WINDOW] + bias, sc[:, 3 * WINDOW:]], axis=1)
            scs.append(sc)
        return scs, vcat

    def block_output(s, scs, vcat):
        outs = []
        for g in range(A_KV_HEADS):
            mx = jnp.maximum(jnp.max(scs[g], axis=-1, keepdims=True), sinks[g])
            e = jnp.exp(scs[g] - mx)
            denom = jnp.sum(e, axis=-1, keepdims=True) + jnp.exp(sinks[g] - mx)
            outs.append(_dot(e.astype(BF16), vcat) / denom)
        o = jnp.where(lane < hd, outs[0], outs[1]).astype(BF16)
        for j in range(A_GROUP):
            o_ref[s * sub:(s + 1) * sub, j * LANES:(j + 1) * LANES] = o[j * sub:(j + 1) * sub]

    pending = block_scores(0)
    for s in range(1, nsub):
        nxt = block_scores(s)
        block_output(s - 1, *pending)
        pending = nxt
    block_output(nsub - 1, *pending)


def _window_attn(qa, ka, va, sink, batch, n, lc, out_rows):
    t = batch * n
    tq = min(WIN_Q_TILE, n)
    ntile = n // tq
    sub = tq // WINDOW
    nblk = n // WINDOW
    cblk = t // lc
    qmap = lambda b, j: (b * ntile + j, 0)
    prev = lambda b, j: (b * nblk + jnp.maximum(j * sub - 1, 0), 0)
    nxt = lambda b, j: (b * nblk + jnp.minimum((j + 1) * sub, nblk - 1), 0)
    ctx = lambda b, j: (cblk + b, 0)
    kvw = A_KV_WIDTH
    return pl.pallas_call(
        functools.partial(_gqa_kernel, local=True, ntile=ntile),
        grid=(batch, ntile),
        in_specs=[pl.BlockSpec(memory_space=pltpu.SMEM),
                  pl.BlockSpec((tq, A_WIDTH), qmap),
                  pl.BlockSpec((WINDOW, kvw), prev), pl.BlockSpec((tq, kvw), qmap),
                  pl.BlockSpec((WINDOW, kvw), nxt),
                  pl.BlockSpec((WINDOW, kvw), prev), pl.BlockSpec((tq, kvw), qmap),
                  pl.BlockSpec((WINDOW, kvw), nxt),
                  pl.BlockSpec((lc, kvw), ctx), pl.BlockSpec((lc, kvw), ctx),
                  pl.BlockSpec(memory_space=pl.ANY)],
        out_specs=pl.BlockSpec((tq, A_WIDTH), qmap),
        out_shape=jax.ShapeDtypeStruct((out_rows, A_WIDTH), BF16),
        input_output_aliases={10: 0},
        compiler_params=_cparams(("arbitrary", "arbitrary")),
        name="window_attn",
    )(sink, qa, ka, ka, ka, va, va, va, ka, va, jnp.zeros((out_rows, A_WIDTH), BF16))


def _ctx_gqa_attn(qa, ka, va, sink, oa, batch, n, lc):
    cblk = batch * n // lc
    cmap = lambda b: (cblk + b, 0)
    return pl.pallas_call(
        functools.partial(_gqa_kernel, local=False, ntile=0),
        grid=(batch,),
        in_specs=[pl.BlockSpec(memory_space=pltpu.SMEM),
                  pl.BlockSpec((lc, A_WIDTH), cmap),
                  pl.BlockSpec((lc, A_KV_WIDTH), cmap), pl.BlockSpec((lc, A_KV_WIDTH), cmap),
                  pl.BlockSpec(memory_space=pl.ANY)],
        out_specs=pl.BlockSpec((lc, A_WIDTH), cmap),
        out_shape=jax.ShapeDtypeStruct(oa.shape, BF16),
        input_output_aliases={4: 0},
        compiler_params=_cparams(("arbitrary",)),
        name="ctx_gqa_attn",
    )(sink, qa, ka, va, oa)


def _mla_kernel(q_ref, kc_ref, vc_ref, *refs, with_latent):
    if with_latent:
        kx_ref, vx_ref = refs[:2]
        o_ref, s_ref = refs[-2:]
    else:
        _, o_ref, s_ref = refs
    tq = q_ref.shape[0]
    lc = kc_ref.shape[0]
    chunks = [(kc_ref, vc_ref, 0, lc, 0)]
    if with_latent:
        n = kx_ref.shape[0]
        for r0 in range(0, n, MLA_KEY_CHUNK):
            chunks.append((kx_ref, vx_ref, r0, min(MLA_KEY_CHUNK, n - r0), lc + r0))

    def fold(x, op):
        while x.shape[0] > 8:
            half = x.shape[0] // 2
            x = op(x[:half], x[half:])
        return x

    def scores(h):
        ql = slice(h * B_HEAD_PAD, (h + 1) * B_HEAD_PAD)
        qh = q_ref[:, ql]
        m8 = None
        for k_ref, _, r0, nr, c0 in chunks:
            st = _dot_nt(k_ref[r0:r0 + nr, ql], qh)
            s_ref[h % 2, c0:c0 + nr, :] = st
            part = fold(st, jnp.maximum)
            m8 = part if m8 is None else jnp.maximum(m8, part)
        return jnp.max(m8, axis=0, keepdims=True)

    def values(h, mx):
        vl = slice(h * B_VDIM, (h + 1) * B_VDIM)
        l8 = None
        acc = None
        for _, vt_ref, r0, nr, c0 in chunks:
            et = jnp.exp2(s_ref[h % 2, c0:c0 + nr, :] - mx)
            part = fold(et, jnp.add)
            l8 = part if l8 is None else l8 + part
            pv = _dot(vt_ref[vl, r0:r0 + nr], et.astype(BF16))
            acc = pv if acc is None else acc + pv
        return acc / jnp.sum(l8, axis=0, keepdims=True)

    outs = []
    mx_prev = scores(0)
    for h in range(1, B_HEADS):
        mx = scores(h)
        outs.append(values(h - 1, mx_prev))
        mx_prev = mx
    outs.append(values(B_HEADS - 1, mx_prev))
    o_ref[...] = jnp.concatenate(outs, axis=0).T.astype(BF16)


def _mla_attn(qb, kb, vb, batch, n, lc, out_rows):
    t = batch * n
    tq = min(MLA_Q_TILE, n)
    nq = n // tq
    cblk = t // lc
    qmap = lambda b, j: (b * nq + j, 0)
    ctx = lambda b, j: (cblk + b, 0)
    lat = lambda b, j: (b, 0)
    ctx_t = lambda b, j: (0, cblk + b)
    lat_t = lambda b, j: (0, b)
    return pl.pallas_call(
        functools.partial(_mla_kernel, with_latent=True),
        grid=(batch, nq),
        in_specs=[pl.BlockSpec((tq, B_QK_WIDTH), qmap),
                  pl.BlockSpec((lc, B_QK_WIDTH), ctx), pl.BlockSpec((B_WIDTH, lc), ctx_t),
                  pl.BlockSpec((n, B_QK_WIDTH), lat), pl.BlockSpec((B_WIDTH, n), lat_t),
                  pl.BlockSpec(memory_space=pl.ANY)],
        out_specs=pl.BlockSpec((tq, B_WIDTH), qmap),
        out_shape=jax.ShapeDtypeStruct((out_rows, B_WIDTH), BF16),
        input_output_aliases={5: 0},
        scratch_shapes=[pltpu.VMEM((2, lc + n, tq), F32)],
        compiler_params=_cparams(("arbitrary", "arbitrary")),
        name="mla_attn",
    )(qb, kb, vb, kb, vb, jnp.zeros((out_rows, B_WIDTH), BF16))


def _mla_ctx_attn(qb, kb, vb, ob, batch, n, lc):
    cblk = batch * n // lc
    cmap = lambda b: (cblk + b, 0)
    return pl.pallas_call(
        functools.partial(_mla_kernel, with_latent=False),
        grid=(batch,),
        in_specs=[pl.BlockSpec((lc, B_QK_WIDTH), cmap),
                  pl.BlockSpec((lc, B_QK_WIDTH), cmap),
                  pl.BlockSpec((B_WIDTH, lc), lambda b: (0, cblk + b)),
                  pl.BlockSpec(memory_space=pl.ANY)],
        out_specs=pl.BlockSpec((lc, B_WIDTH), cmap),
        out_shape=jax.ShapeDtypeStruct(ob.shape, BF16),
        input_output_aliases={3: 0},
        scratch_shapes=[pltpu.VMEM((2, lc, lc), F32)],
        compiler_params=_cparams(("arbitrary",)),
        name="mla_ctx_attn",
    )(qb, kb, vb, ob)


def _route_tile(logits, run):
    tm = logits.shape[0]
    lane_i = lax.broadcasted_iota(jnp.int32, logits.shape, 1)
    lane = lane_i.astype(F32)
    nolane = float(LANES)

    def first_argmax(v):
        mx = jnp.max(v, axis=-1, keepdims=True)
        return mx, jnp.min(jnp.where(v == mx, lane, nolane), axis=-1, keepdims=True)

    gl = jnp.where(lane_i < N_GROUPS, logits, NEG_INF)
    gmax, g_sel = first_argmax(gl)
    g_p = 1.0 / jnp.sum(jnp.exp(gl - gmax), axis=-1, keepdims=True)
    lo = N_GROUPS + EXPERTS_PER_GROUP * g_sel
    el = jnp.where(jnp.logical_and(lane >= lo, lane < lo + EXPERTS_PER_GROUP), logits, NEG_INF)
    v1, i1 = first_argmax(el)
    sel1 = lane == i1
    v2, i2 = first_argmax(jnp.where(sel1, NEG_INF, el))
    sel2 = lane == i2
    t2 = jnp.exp(v2 - v1)
    gate1 = g_p / (1.0 + t2)
    gate2 = gate1 * t2

    onehot = jnp.where(jnp.logical_or(sel1, sel2), 1.0, 0.0)
    r_i = lax.broadcasted_iota(jnp.int32, (tm, tm), 0)
    c_i = lax.broadcasted_iota(jnp.int32, (tm, tm), 1)
    lower = jnp.where(c_i < r_i, 1.0, 0.0).astype(BF16)
    before = _dot(lower, onehot.astype(BF16)) + run
    rank1 = jnp.sum(jnp.where(sel1, before, 0.0), axis=-1, keepdims=True)
    rank2 = jnp.sum(jnp.where(sel2, before, 0.0), axis=-1, keepdims=True)
    info = jnp.zeros_like(logits)
    for k, val in enumerate((i1 - N_GROUPS, i2 - N_GROUPS, rank1, rank2, gate1, gate2)):
        info = jnp.where(lane_i == k, val, info)
    return info, run + jnp.sum(onehot, axis=0, keepdims=True)


def _merge_kernel(tmod_ref, tpos_ref, tlen_ref, *refs, n_lat_tiles):
    del tmod_ref
    load_h, refs = _stream_tile(refs, n_lat_tiles)
    (mod_ref, g_ref, oa_ref, ob_ref, u_ref, up_ref, un_ref, gs_ref, wpool_ref, pscale_ref, wa_ref,
     wb_ref, wc_ref, wo_ref, wr_ref, br_ref, hn_o, fx_o, info_o, cnt_o, run_ref) = refs
    i = pl.program_id(0)

    @pl.when(i == 0)
    def _():
        run_ref[...] = jnp.zeros_like(run_ref)

    pos0 = tpos_ref[i]
    seq_len = tlen_ref[i]
    tm = u_ref.shape[0]
    d = hn_o.shape[1]

    u = u_ref[...]
    kdim = tm + LANES
    zpad = jnp.zeros((LANES - 2 * POOL_HALO, u.shape[1]), BF16)
    ucat = jnp.concatenate([up_ref[...], u, un_ref[...], zpad], axis=0)
    uf = u.astype(F32)
    r_i = lax.broadcasted_iota(jnp.int32, (tm, kdim), 0)
    c_i = lax.broadcasted_iota(jnp.int32, (tm, kdim), 1)
    rel = c_i - POOL_HALO - r_i
    kpos = pos0 + c_i - POOL_HALO
    valid = jnp.logical_and(kpos >= 0, kpos < seq_len)
    tpos = pos0 + lax.broadcasted_iota(jnp.int32, (tm, 1), 0)
    win_sums = []
    for gi, w in enumerate(POOL_WINDOWS):
        rad = w // 2
        sl = slice(gi * C_GROUP_DIM, (gi + 1) * C_GROUP_DIM)
        band = jnp.where(jnp.logical_and(jnp.abs(rel) <= rad, valid), 1.0, 0.0).astype(BF16)
        win_sums.append(_dot(band, ucat[:, sl]))
    ya = _dot(oa_ref[...], wa_ref[...])
    oc_parts = []
    for gi, w in enumerate(POOL_WINDOWS):
        rad = w // 2
        sl = slice(gi * C_GROUP_DIM, (gi + 1) * C_GROUP_DIM)
        cnt = (jnp.minimum(tpos + rad + 1, seq_len) - jnp.maximum(tpos - rad, 0)).astype(F32)
        pooled = (win_sums[gi] / cnt - uf[:, sl]).astype(BF16)
        oc_parts.append(_dot(pooled, wpool_ref[gi]))
    yb = _dot(ob_ref[...], wb_ref[...])
    oc = (jnp.concatenate(oc_parts, axis=1) * pscale_ref[...]).astype(BF16)

    y = gs_ref[:, 0:d].astype(F32) * ya
    y = y + gs_ref[:, d:2 * d].astype(F32) * yb
    y = y + gs_ref[:, 2 * d:3 * d].astype(F32) * _dot(oc, wc_ref[...])
    mix = _dot(y.astype(BF16), wo_ref[...])

    m = mod_ref[0]
    hn = load_h() + m[2:3] * mix
    hn_o[...] = hn
    fx = _rms(hn, g_ref[...]) * (1.0 + m[4:5]) + m[3:4]
    fx_hi = fx.astype(BF16)
    fx_lo = (fx - fx_hi.astype(F32)).astype(BF16)
    fx_o[...] = fx_hi
    both = _dot(fx_hi, wr_ref[...])
    logits = (both[:, :ROUTER_LANES] + _dot(fx_lo, wr_ref[:, :ROUTER_LANES])
              + both[:, ROUTER_LANES:]) + br_ref[...]
    info, run = _route_tile(logits, run_ref[...])
    info_o[...] = info
    run_ref[...] = run
    cnt_o[...] = run


def _merge(h_all, mod, norm_g, oa, ob, u, gs, lw, rows, tile_mod, tile_pos, tile_len):
    tm = ROW_TILE
    stream_specs, stream_args, n_lat_tiles = _stream_specs(h_all, tm)
    d = stream_args[0].shape[1]
    nt = rows // tm
    hb = tm // POOL_HALO
    nhalo = u.shape[0] // POOL_HALO
    row = lambda i, *_: (i, 0)
    const = lambda i, *_: (0, 0)
    const3 = lambda i, *_: (0, 0, 0)
    grid_spec = pltpu.PrefetchScalarGridSpec(
        num_scalar_prefetch=3,
        grid=(nt,),
        in_specs=stream_specs + [
                  pl.BlockSpec((1, 8, d), lambda i, tmod, *_: (tmod[i], 0, 0)),
                  pl.BlockSpec((1, d), const),
                  pl.BlockSpec((tm, A_WIDTH), row),
                  pl.BlockSpec((tm, B_WIDTH), row),
                  pl.BlockSpec((tm, C_WIDTH), row),
                  pl.BlockSpec((POOL_HALO, C_WIDTH), lambda i, *_: (jnp.maximum(i * hb - 1, 0), 0)),
                  pl.BlockSpec((POOL_HALO, C_WIDTH), lambda i, *_: (jnp.minimum((i + 1) * hb, nhalo - 1), 0)),
                  pl.BlockSpec((tm, N_BRANCH * d), row),
                  pl.BlockSpec(lw["wpool"].shape, const3),
                  pl.BlockSpec((1, C_WIDTH), const),
                  pl.BlockSpec(lw["wa"].shape, const),
                  pl.BlockSpec(lw["wb"].shape, const),
                  pl.BlockSpec(lw["wc"].shape, const),
                  pl.BlockSpec(lw["wo"].shape, const),
                  pl.BlockSpec(lw["wr"].shape, const),
                  pl.BlockSpec((1, ROUTER_LANES), const)],
        out_specs=[pl.BlockSpec((tm, d), row), pl.BlockSpec((tm, d), row),
                   pl.BlockSpec((tm, ROUTER_LANES), row), pl.BlockSpec((1, ROUTER_LANES), const)],
        scratch_shapes=[pltpu.VMEM((1, ROUTER_LANES), F32)],
    )
    return pl.pallas_call(
        functools.partial(_merge_kernel, n_lat_tiles=n_lat_tiles),
        grid_spec=grid_spec,
        out_shape=[jax.ShapeDtypeStruct((rows, d), F32), jax.ShapeDtypeStruct((rows, d), BF16),
                   jax.ShapeDtypeStruct((rows, ROUTER_LANES), F32),
                   jax.ShapeDtypeStruct((1, ROUTER_LANES), F32)],
        compiler_params=_cparams(("arbitrary",)),
        name="merge",
    )(tile_mod, tile_pos, tile_len, *stream_args, mod, norm_g, oa, ob, u, u, u, gs, lw["wpool"],
      lw["pscale"], lw["wa"], lw["wb"], lw["wc"], lw["wo"], lw["wr"], lw["br"])


def _moe_kernel(blk_e_ref, nused_ref, x_ref, wgu_ref, wdn_ref, *refs, first_block):
    del blk_e_ref
    o_ref = refs[-1]
    i = first_block + pl.program_id(0)

    @pl.when(i < nused_ref[0])
    def _():
        gu = _dot(x_ref[...], wgu_ref[0, 0].astype(BF16))
        gt, up = gu[:, :EXPERT_FF], gu[:, EXPERT_FF:]
        act = (gt * jax.nn.sigmoid(gt) * up).astype(BF16)
        o_ref[...] = _dot(act, wdn_ref[0, 0].astype(BF16)).astype(o_ref.dtype)

    @pl.when(i >= nused_ref[0])
    def _():
        o_ref[...] = jnp.zeros_like(o_ref)


def _moe(xg, blk_e, n_used, w_gu, w_dn, layer, first_block, out):
    d = xg.shape[1]
    nb = xg.shape[0] // MOE_BLOCK
    total = blk_e.shape[0] * MOE_BLOCK
    in_specs = [pl.BlockSpec((MOE_BLOCK, d), lambda i, *_: (i, 0)),
                pl.BlockSpec((1, 1, d, 2 * EXPERT_FF),
                             lambda i, be, nu: (layer, be[first_block + i], 0, 0)),
                pl.BlockSpec((1, 1, EXPERT_FF, d),
                             lambda i, be, nu: (layer, be[first_block + i], 0, 0))]
    args = [blk_e, n_used, xg, w_gu, w_dn]
    aliases = {}
    if out is not None:
        in_specs.append(pl.BlockSpec(memory_space=pl.ANY))
        args.append(out)
        aliases = {len(args) - 1: 0}
    grid_spec = pltpu.PrefetchScalarGridSpec(
        num_scalar_prefetch=2,
        grid=(nb,),
        in_specs=in_specs,
        out_specs=pl.BlockSpec((MOE_BLOCK, d), lambda i, *_: (first_block + i, 0)),
    )
    return pl.pallas_call(
        functools.partial(_moe_kernel, first_block=first_block),
        grid_spec=grid_spec,
        out_shape=jax.ShapeDtypeStruct((total, d), BF16),
        input_output_aliases=aliases,
        compiler_params=_cparams(("arbitrary",)),
        name="moe_experts",
    )(*args)


def _final_kernel(tmod_ref, h_ref, mod_ref, y0_ref, y1_ref, info_ref, fg_ref, o_ref):
    del tmod_ref
    hn = _ffn_residual(h_ref[...], mod_ref[0], y0_ref[...], y1_ref[...], info_ref[...])
    o_ref[...] = _rms(hn, fg_ref[...])


def _final(h, mod, y0, y1, info, final_g, tile_mod):
    rows, d = h.shape
    tm = ROW_TILE
    row = lambda i, *_: (i, 0)
    grid_spec = pltpu.PrefetchScalarGridSpec(
        num_scalar_prefetch=1,
        grid=(rows // tm,),
        in_specs=[pl.BlockSpec((tm, d), row),
                  pl.BlockSpec((1, 8, d), lambda i, tmod: (tmod[i], 0, 0)),
                  pl.BlockSpec((tm, d), row), pl.BlockSpec((tm, d), row),
                  pl.BlockSpec((tm, ROUTER_LANES), row),
                  pl.BlockSpec((1, d), lambda i, *_: (0, 0))],
        out_specs=pl.BlockSpec((tm, d), row),
    )
    return pl.pallas_call(
        _final_kernel,
        grid_spec=grid_spec,
        out_shape=jax.ShapeDtypeStruct((rows, d), F32),
        compiler_params=_cparams(("arbitrary",)),
        name="final_residual_norm",
    )(tile_mod, h, mod, y0, y1, info, final_g)


def _deinterleave(n):
    return np.concatenate([np.arange(0, n, 2), np.arange(1, n, 2)])


def _rope_tables(n, tm):
    rows = n // GRID_W
    row = np.repeat(np.arange(rows), GRID_W).astype(np.float32)
    col = np.tile(np.arange(GRID_W), rows).astype(np.float32)

    def cs(rot_dim):
        axis_dim = rot_dim // 2
        inv = jnp.asarray(ROPE_BASE, F32) ** (-jnp.arange(0, axis_dim, 2, dtype=F32) / axis_dim)
        ang = jnp.concatenate([jnp.asarray(row)[:, None] * inv, jnp.asarray(col)[:, None] * inv], axis=-1)
        return jnp.cos(ang), jnp.sin(ang)

    cos_a, sin_a = cs(A_HEAD_DIM)
    ca = jnp.tile(jnp.concatenate([cos_a, cos_a], axis=1), (1, LANES // A_HEAD_DIM))
    sa = jnp.tile(jnp.concatenate([-sin_a, sin_a], axis=1), (1, LANES // A_HEAD_DIM))
    cos_b, sin_b = cs(B_ROPE)
    ones = jnp.ones((n, B_NOPE), F32)
    tail = LANES - B_NOPE - B_ROPE
    cb = jnp.concatenate([ones, cos_b, cos_b, jnp.ones((n, tail), F32)], axis=1)
    sb = jnp.concatenate([0 * ones, -sin_b, sin_b, jnp.zeros((n, tail), F32)], axis=1)
    ident_c = jnp.ones((tm, LANES), F32)
    ident_s = jnp.zeros((tm, LANES), F32)
    return {"ca": jnp.concatenate([ca, ident_c]), "sa": jnp.concatenate([sa, ident_s]),
            "cb": jnp.concatenate([cb, ident_c]), "sb": jnp.concatenate([sb, ident_s])}


def _layer_weights(l, w_in, q_norm_g, w_uq, kv_norm_g, w_ukv, w_pool, pool_scale, w_br_a, w_br_b,
                   w_br_c, w_out, w_rg, b_rg, w_re, b_re):
    d = w_in.shape[1]
    wi = w_in[l]
    splits = np.cumsum([A_WIDTH, A_KV_WIDTH, A_KV_WIDTH, Q_LORA, KV_LORA, B_ROPE, C_WIDTH])
    qa, ka, va, cq, ckv, kr, u, gates = jnp.split(wi, splits, axis=1)
    pa = _deinterleave(A_HEAD_DIM)
    ho = np.arange(A_HEADS).reshape(A_KV_HEADS, A_GROUP).T.reshape(-1)
    qa = qa.reshape(d, A_HEADS, A_HEAD_DIM)[:, ho][:, :, pa].reshape(d, A_WIDTH) * (A_HEAD_DIM ** -0.5)
    wa = w_br_a[l].reshape(A_HEADS, A_HEAD_DIM, -1)[ho].reshape(A_WIDTH, -1)
    ka = ka.reshape(d, A_KV_HEADS, A_HEAD_DIM)[:, :, pa].reshape(d, A_KV_WIDTH)
    pb = _deinterleave(B_ROPE)
    tail = LANES - B_NOPE - B_ROPE
    kr128 = jnp.concatenate([jnp.zeros((d, B_NOPE), F32), kr[:, pb], jnp.zeros((d, tail), F32)], axis=1)
    w1 = jnp.concatenate([qa, ka, va, cq, ckv, kr128, u, gates], axis=1).astype(BF16)

    uq = w_uq[l].reshape(Q_LORA, B_HEADS, B_NOPE + B_ROPE)
    uq = jnp.concatenate([uq[:, :, :B_NOPE], uq[:, :, B_NOPE:][:, :, pb],
                          jnp.zeros((Q_LORA, B_HEADS, tail), F32)], axis=2)
    ukv = w_ukv[l].reshape(KV_LORA, B_HEADS, B_NOPE + B_VDIM)
    uk = jnp.concatenate([ukv[:, :, :B_NOPE], jnp.zeros((KV_LORA, B_HEADS, LANES - B_NOPE), F32)], axis=2)
    uv = ukv[:, :, B_NOPE:]

    wr = jnp.concatenate([w_rg[l], w_re[l], jnp.zeros((d, ROUTER_LANES - N_GROUPS - N_EXPERTS), F32)], axis=1)
    wr_hi = wr.astype(BF16)
    wr_lo = (wr - wr_hi.astype(F32)).astype(BF16)
    br = jnp.concatenate([b_rg[l], b_re[l], jnp.zeros((ROUTER_LANES - N_GROUPS - N_EXPERTS,), F32)])
    return {
        "w1": w1,
        "qng": q_norm_g[l].reshape(1, Q_LORA), "kvng": kv_norm_g[l].reshape(1, KV_LORA),
        "wuq": uq.reshape(Q_LORA, B_QK_WIDTH).astype(BF16),
        "wuk": uk.reshape(KV_LORA, B_QK_WIDTH).astype(BF16),
        "wuv": uv.reshape(KV_LORA, B_WIDTH).astype(BF16),
        "wpool": w_pool[l].astype(BF16), "pscale": pool_scale[l].reshape(1, C_WIDTH),
        "wa": wa.astype(BF16), "wb": w_br_b[l].astype(BF16), "wc": w_br_c[l].astype(BF16),
        "wo": w_out[l].astype(BF16),
        "wr": jnp.concatenate([wr_hi, wr_lo], axis=1), "br": br.reshape(1, ROUTER_LANES),
    }


def _block_layout(info, cnt):
    t = info.shape[0]
    a = t * TOP_K
    expert = info[:, 0:TOP_K].astype(jnp.int32)
    rank = info[:, TOP_K:2 * TOP_K].astype(jnp.int32)
    counts = cnt[0, N_GROUPS:N_GROUPS + N_EXPERTS].astype(jnp.int32)
    padded = (counts + MOE_BLOCK - 1) // MOE_BLOCK * MOE_BLOCK
    pad_end = jnp.cumsum(padded)
    pad_start = pad_end - padded
    eids = jnp.arange(N_EXPERTS, dtype=jnp.int32)
    base = jnp.sum(jnp.where(expert[:, :, None] == eids, pad_start, 0), axis=-1)
    slot = base + rank
    n_blocks = -(-(a + N_EXPERTS * (MOE_BLOCK - 1)) // MOE_BLOCK)
    blk_start = jnp.arange(n_blocks, dtype=jnp.int32) * MOE_BLOCK
    blk_e = jnp.minimum(jnp.sum((pad_end[None, :] <= blk_start[:, None]).astype(jnp.int32), axis=1),
                        N_EXPERTS - 1)
    n_used = (pad_end[-1] // MOE_BLOCK).reshape(1)

    by_slot = jnp.argsort(slot.reshape(a)).astype(jnp.int32) // TOP_K
    first = jnp.cumsum(counts) - counts
    blk_sel = blk_e[:, None] == eids[None, :]
    blk_off = blk_start - jnp.sum(jnp.where(blk_sel, pad_start, 0), axis=1)
    blk_src = jnp.sum(jnp.where(blk_sel, first, 0), axis=1) + blk_off
    blk_cnt = jnp.sum(jnp.where(blk_sel, counts, 0), axis=1) - blk_off
    within = jnp.arange(MOE_BLOCK, dtype=jnp.int32)[None, :]
    src = jnp.minimum(blk_src[:, None] + within, a - 1)
    filler = (blk_start[:, None] + within) % t
    tok_of_slot = jnp.where(within < blk_cnt[:, None], by_slot.at[src].get(mode="promise_in_bounds"),
                            filler)
    return slot, tok_of_slot.reshape(n_blocks * MOE_BLOCK), blk_e, n_used


def kernel(x, c, ctx, c_ctx, w_mod, b_mod, norm_mix_g, norm_ffn_g, w_in, sink, q_norm_g, w_uq, kv_norm_g,
           w_ukv, w_pool, pool_scale, w_br_a, w_br_b, w_br_c, w_out, w_rg, b_rg, w_re, b_re, w_gu, w_dn,
           final_g):
    batch, n, d = x.shape
    lc = ctx.shape[1]
    depth = w_mod.shape[0]
    tm = ROW_TILE
    assert n % tm == 0 and lc % tm == 0 and n % WINDOW == 0 and (batch * n) % lc == 0
    t = batch * n
    tc = batch * lc
    nt_lat, nt_ctx = t // tm, tc // tm
    per_b, per_c = n // tm, lc // tm

    lat_i = np.arange(nt_lat)
    ctx_i = np.arange(nt_ctx)
    tile_mod = jnp.asarray(np.concatenate([lat_i // per_b, np.full(nt_ctx, batch)]), jnp.int32)
    tile_rope = jnp.asarray(np.concatenate([lat_i % per_b, np.full(nt_ctx, per_b)]), jnp.int32)
    tile_pos = jnp.asarray(np.concatenate([(lat_i % per_b) * tm, (ctx_i % per_c) * tm]), jnp.int32)
    tile_len = jnp.asarray(np.concatenate([np.full(nt_lat, n), np.full(nt_ctx, lc)]), jnp.int32)
    tables = _rope_tables(n, tm)

    mod_rows = 16
    cvec = jnp.concatenate([c, c_ctx[None, :], jnp.zeros((mod_rows - batch - 1, d), F32)], axis=0)
    h_all = (x.reshape(t, d), ctx.reshape(tc, d))

    mod_all = _modvec(cvec, w_mod, b_mod).reshape(depth, mod_rows, 6, d)
    mod_all = jnp.concatenate([mod_all, jnp.zeros((depth, mod_rows, 2, d), F32)], axis=2)

    out = None
    pending = None
    for l in range(depth):
        last = l == depth - 1
        lw = _layer_weights(l, w_in, q_norm_g, w_uq, kv_norm_g, w_ukv, w_pool, pool_scale, w_br_a,
                            w_br_b, w_br_c, w_out, w_rg, b_rg, w_re, b_re)
        mod = mod_all[l]

        proj = _inproj(h_all, mod, norm_mix_g[l].reshape(1, d), lw, tables, tile_mod, tile_rope,
                       pending)
        qa, ka, va, qb, kb, vb, u, gs = proj[:8]
        if pending is not None:
            h_all = proj[8]
        rows = t if last else t + tc
        oa = _window_attn(qa, ka, va, sink[l], batch, n, lc, rows)
        ob = _mla_attn(qb, kb, vb, batch, n, lc, rows)
        if not last:
            oa = _ctx_gqa_attn(qa, ka, va, sink[l], oa, batch, n, lc)
            ob = _mla_ctx_attn(qb, kb, vb, ob, batch, n, lc)
        hn, fx, info, cnt = _merge(h_all, mod, norm_ffn_g[l].reshape(1, d), oa, ob, u, gs, lw, rows,
                                   tile_mod, tile_pos, tile_len)

        slot, tok_of_slot, blk_e, n_used = _block_layout(info, cnt)
        split = (blk_e.shape[0] // 2) * MOE_BLOCK
        yb = jnp.zeros((blk_e.shape[0] * MOE_BLOCK, d), BF16)
        for lo, hi in ((0, split), (split, tok_of_slot.shape[0])):
            xg = fx.at[tok_of_slot[lo:hi]].get(mode="promise_in_bounds")
            yb = _moe(xg, blk_e, n_used, w_gu, w_dn, l, lo // MOE_BLOCK, yb)
        y0 = yb.at[slot[:, 0]].get(mode="promise_in_bounds")
        y1 = yb.at[slot[:, 1]].get(mode="promise_in_bounds")
        if last:
            out = _final(hn, mod, y0, y1, info, final_g.reshape(1, d), tile_mod)
        else:
            h_all, pending = hn, (y0, y1, info, mod)
    return out.reshape(batch, n, d)
```
